```python
import math
import jax, jax.numpy as jnp
from jax import lax
import numpy as np

D_MODEL = 1024
BATCH = 8
SEQ = 2048
DEPTH = 2
DEC_BATCH = 128
DEC_SEQ = 4
PAST_LEN = 2048
PAGE_SIZE = 128

N_EVEN = (DEPTH + 1) // 2
N_ODD = DEPTH // 2
POOL_WINDOWS = (2, 4, 8, 16)
N_POOL_GROUPS = len(POOL_WINDOWS)
POOL_WIDTH = D_MODEL // 2
POOL_GROUP = POOL_WIDTH // N_POOL_GROUPS
POOL_BUF = max(POOL_WINDOWS) - 1
SB_HEADS = 8
SB_HEAD_DIM = (D_MODEL // 2) // SB_HEADS
SB_WIDTH = SB_HEADS * SB_HEAD_DIM
SB_BIAS_INIT = -8.0
Q_BLOCK = 128
EVEN_MIX = POOL_WIDTH + SB_WIDTH
EVEN_IN = POOL_WIDTH + 3 * SB_WIDTH + EVEN_MIX
DN_HEADS = 8
DN_KEY_DIM = 128
DN_VAL_DIM = 128
DN_QK = DN_HEADS * DN_KEY_DIM
DN_V = DN_HEADS * DN_VAL_DIM
DN_CONV_DIM = 2 * DN_QK + DN_V
CONV_WIDTH = 4
DN_CHUNK = 64
ODD_IN = DN_CONV_DIM + DN_V + 2 * DN_HEADS
NORM_EPS = 1e-6

kernel_name = 'pool_stickbreak_gdn_hybrid_step'

f32 = jnp.float32


def rms_norm(x, g):
    xf = x.astype(f32)
    y = xf * lax.rsqrt(jnp.mean(xf * xf, axis=-1, keepdims=True) + NORM_EPS)
    return (y * g.astype(f32)).astype(x.dtype)


def l2_norm(x):
    xf = x.astype(f32)
    return xf * lax.rsqrt(jnp.sum(xf * xf, axis=-1, keepdims=True) + NORM_EPS)


def pool_mix(u, buf, start, w_pool, scale):
    B, L, P = u.shape
    ext = jnp.concatenate([buf.astype(f32), u.astype(f32)], axis=1)
    csum = jnp.concatenate([jnp.zeros_like(ext[:, :1]), jnp.cumsum(ext, axis=1)], axis=1)
    pos = start + jnp.arange(L)
    end = csum[:, POOL_BUF + 1:]
    cur = ext[:, POOL_BUF:]
    outs = []
    for gi, w in enumerate(POOL_WINDOWS):
        sl = slice(gi * POOL_GROUP, (gi + 1) * POOL_GROUP)
        win = end[..., sl] - csum[:, POOL_BUF + 1 - w:POOL_BUF + 1 - w + L, sl]
        cnt = jnp.minimum(pos + 1, w).astype(f32)[None, :, None]
        outs.append(win / cnt - cur[..., sl])
    d = jnp.stack(outs, axis=2)
    y = jnp.einsum('blgc,gcd->blgd', d, w_pool.astype(f32)).reshape(B, L, P)
    return y * scale.astype(f32)


def stick_breaking_block(q, k, v, bias, q_pos, k_pos):
    z = jnp.einsum('bqhd,bkhd->bhqk', q.astype(f32), k.astype(f32)) / math.sqrt(SB_HEAD_DIM)
    z = z + bias.astype(f32)[None, :, None, None]
    mask = k_pos[None, :] < q_pos[:, None]
    log_keep = jnp.where(mask, jax.nn.log_sigmoid(-z), 0.0)
    suffix = lax.cumsum(log_keep, axis=3, reverse=True) - log_keep
    a = jnp.where(mask, jnp.exp(jax.nn.log_sigmoid(z) + suffix), 0.0)
    return jnp.einsum('bhqk,bkhd->bqhd', a, v.astype(f32))


def stick_breaking_prompt(q, k, v, bias):
    B, L, H, d = q.shape
    qb_len = Q_BLOCK if L % Q_BLOCK == 0 else L
    nb = L // qb_len
    qb = jnp.moveaxis(q.reshape(B, nb, qb_len, H, d), 1, 0)
    k_pos = jnp.arange(L)

    def body(args):
        q_blk, blk = args
        q_pos = blk * qb_len + jnp.arange(qb_len)
        return stick_breaking_block(q_blk, k, v, bias, q_pos, k_pos)

    o = lax.map(body, (qb, jnp.arange(nb)))
    return jnp.moveaxis(o, 0, 1).reshape(B, L, H, d)


def even_layer(h, start, pool_buf, k_past, v_past, norm_g, w_in, w_pool, pool_scale, q_g, k_g, sb_bias, w_out):
    B, L, _ = h.shape
    proj = rms_norm(h, norm_g) @ w_in
    i1 = POOL_WIDTH
    i2 = i1 + SB_WIDTH
    i3 = i2 + SB_WIDTH
    i4 = i3 + SB_WIDTH
    u, q, k, v, gate = jnp.split(proj, [i1, i2, i3, i4], axis=-1)
    pool_y = pool_mix(u, pool_buf, start, w_pool, pool_scale)
    q = rms_norm(q.reshape(B, L, SB_HEADS, SB_HEAD_DIM), q_g)
    k = rms_norm(k.reshape(B, L, SB_HEADS, SB_HEAD_DIM), k_g)
    v = v.reshape(B, L, SB_HEADS, SB_HEAD_DIM)
    if k_past is None:
        sb = stick_breaking_prompt(q, k, v, sb_bias)
    else:
        k_all = jnp.concatenate([k_past.astype(k.dtype), k], axis=1)
        v_all = jnp.concatenate([v_past.astype(v.dtype), v], axis=1)
        sb = stick_breaking_block(q, k_all, v_all, sb_bias, start + jnp.arange(L), jnp.arange(start + L))
    mix = jnp.concatenate([pool_y, sb.reshape(B, L, SB_WIDTH)], axis=-1) * jax.nn.silu(gate.astype(f32))
    y = h + mix.astype(h.dtype) @ w_out
    new_buf = jnp.concatenate([pool_buf.astype(u.dtype), u], axis=1)[:, -POOL_BUF:]
    return y, new_buf, k, v


def gated_delta_rule(q, k, v, g, beta, state):
    B, L, H, dk = k.shape
    dv = v.shape[-1]
    C = DN_CHUNK if L % DN_CHUNK == 0 else L
    n = L // C

    def to_chunks(x):
        x = x.reshape(B, n, C, H, *x.shape[3:])
        return jnp.moveaxis(jnp.moveaxis(x, 1, 0), 3, 2)

    qc, kc, vc = to_chunks(q), to_chunks(k), to_chunks(v.astype(f32))
    gc = jnp.cumsum(to_chunks(g), axis=-1)
    bc = to_chunks(beta)
    incl = jnp.tril(jnp.ones((C, C), bool))
    strict = jnp.tril(jnp.ones((C, C), bool), -1)
    decay = jnp.where(incl, jnp.exp(jnp.where(incl, gc[..., :, None] - gc[..., None, :], 0.0)), 0.0)
    kb = kc * bc[..., None]
    lmat = jnp.where(strict, jnp.einsum('nbhid,nbhjd->nbhij', kb, kc) * decay, 0.0)
    eye = jnp.broadcast_to(jnp.eye(C, dtype=f32), lmat.shape)
    tmat = lax.linalg.triangular_solve(eye + lmat, eye, left_side=True, lower=True, unit_diagonal=True)
    u = jnp.einsum('nbhij,nbhjd->nbhid', tmat, vc * bc[..., None])
    w = jnp.einsum('nbhij,nbhjd->nbhid', tmat, kb * jnp.exp(gc)[..., None])

    def step(S, xs):
        q_i, k_i, u_i, w_i, g_i, dec_i = xs
        v_new = u_i - jnp.einsum('bhcd,bhde->bhce', w_i, S)
        att = jnp.einsum('bhid,bhjd->bhij', q_i, k_i) * dec_i
        o = jnp.einsum('bhcd,bhde->bhce', q_i * jnp.exp(g_i)[..., None], S) + jnp.einsum('bhij,bhje->bhie', att, v_new)
        g_last = g_i[..., -1:]
        S = S * jnp.exp(g_last)[..., None] + jnp.einsum('bhcd,bhce->bhde', k_i * jnp.exp(g_last - g_i)[..., None], v_new)
        return S, o

    S, o = lax.scan(step, state, (qc, kc, u, w, gc, decay))
    o = jnp.moveaxis(jnp.moveaxis(o, 2, 3), 0, 1).reshape(B, L, H, dv)
    return o, S


def odd_layer(h, conv_buf, state, norm_g, w_in, conv_w, a_log, dt_bias, o_g, w_out):
    B, L, _ = h.shape
    proj = rms_norm(h, norm_g) @ w_in
    qkv, z, b, a = jnp.split(proj, [DN_CONV_DIM, DN_CONV_DIM + DN_V, DN_CONV_DIM + DN_V + DN_HEADS], axis=-1)
    ext = jnp.concatenate([conv_buf.astype(qkv.dtype), qkv], axis=1)
    conv = ext[:, 0:L].astype(f32) * conv_w[0].astype(f32)
    for j in range(1, CONV_WIDTH):
        conv = conv + ext[:, j:j + L].astype(f32) * conv_w[j].astype(f32)
    conv = jax.nn.silu(conv)
    q, k, v = jnp.split(conv, [DN_QK, 2 * DN_QK], axis=-1)
    q = l2_norm(q.reshape(B, L, DN_HEADS, DN_KEY_DIM)) * (DN_KEY_DIM ** -0.5)
    k = l2_norm(k.reshape(B, L, DN_HEADS, DN_KEY_DIM))
    v = v.reshape(B, L, DN_HEADS, DN_VAL_DIM)
    beta = jax.nn.sigmoid(b.astype(f32))
    g = -jnp.exp(a_log.astype(f32)) * jax.nn.softplus(a.astype(f32) + dt_bias.astype(f32))
    o, new_state = gated_delta_rule(q, k, v, g, beta, state.astype(f32))
    o = rms_norm(o, o_g) * jax.nn.silu(z.reshape(B, L, DN_HEADS, DN_VAL_DIM).astype(f32))
    y = h + o.reshape(B, L, DN_V).astype(h.dtype) @ w_out
    return y, ext[:, -(CONV_WIDTH - 1):], new_state.astype(state.dtype)


def setup_inputs(seed: int = 0) -> dict:
    key = jax.random.key(seed)
    ks = jax.random.split(key, 24)
    n_pages = PAST_LEN // PAGE_SIZE
    n_phys = (DEC_BATCH * n_pages * 5) // 4

    def nrm(k, shape, scale=1.0):
        return jax.random.normal(k, shape, f32) * scale

    page_table = jax.random.permutation(ks[7], n_phys)[:DEC_BATCH * n_pages].reshape(DEC_BATCH, n_pages).astype(jnp.int32)
    a_init = jax.random.uniform(ks[17], (N_ODD, DN_HEADS), f32, 1.0, 16.0)
    dt = jnp.exp(jax.random.uniform(ks[18], (N_ODD, DN_HEADS), f32, math.log(1e-3), math.log(1e-1)))
    return {
        'x_prompt': nrm(ks[0], (BATCH, SEQ, D_MODEL)),
        'x_sample': nrm(ks[1], (DEC_BATCH, DEC_SEQ, D_MODEL)),
        'cache_pool': nrm(ks[2], (N_EVEN, DEC_BATCH, POOL_BUF, POOL_WIDTH)),
        'cache_k': nrm(ks[3], (N_EVEN, n_phys, PAGE_SIZE, SB_HEADS, SB_HEAD_DIM)),
        'cache_v': nrm(ks[4], (N_EVEN, n_phys, PAGE_SIZE, SB_HEADS, SB_HEAD_DIM)),
        'state_conv': nrm(ks[5], (N_ODD, DEC_BATCH, CONV_WIDTH - 1, DN_CONV_DIM)),
        'state_delta': nrm(ks[6], (N_ODD, DEC_BATCH, DN_HEADS, DN_KEY_DIM, DN_VAL_DIM), 0.1),
        'page_table': page_table,
        'norm_even': 1.0 + nrm(ks[8], (N_EVEN, D_MODEL), 0.1),
        'w_in_even': nrm(ks[9], (N_EVEN, D_MODEL, EVEN_IN), D_MODEL ** -0.5),
        'w_pool': nrm(ks[10], (N_EVEN, N_POOL_GROUPS, POOL_GROUP, POOL_GROUP), POOL_GROUP ** -0.5),
        'pool_scale': 1.0 + nrm(ks[11], (N_EVEN, POOL_WIDTH), 0.1),
        'q_norm': 1.0 + nrm(ks[12], (N_EVEN, SB_HEAD_DIM), 0.1),
        'k_norm': 1.0 + nrm(ks[13], (N_EVEN, SB_HEAD_DIM), 0.1),
        'sb_bias': SB_BIAS_INIT + nrm(ks[22], (N_EVEN, SB_HEADS), 0.1),
        'w_out_even': nrm(ks[14], (N_EVEN, EVEN_MIX, D_MODEL), EVEN_MIX ** -0.5),
        'norm_odd': 1.0 + nrm(ks[15], (N_ODD, D_MODEL), 0.1),
        'w_in_odd': nrm(ks[16], (N_ODD, D_MODEL, ODD_IN), D_MODEL ** -0.5),
        'conv_w': nrm(ks[19], (N_ODD, CONV_WIDTH, DN_CONV_DIM), 0.5),
        'a_log': jnp.log(a_init),
        'dt_bias': dt + jnp.log(-jnp.expm1(-dt)),
        'o_norm': 1.0 + nrm(ks[20], (N_ODD, DN_VAL_DIM), 0.1),
        'w_out_odd': nrm(ks[21], (N_ODD, DN_V, D_MODEL), DN_V ** -0.5),
    }


def reference(x_prompt, x_sample, cache_pool, cache_k, cache_v, state_conv, state_delta, page_table,
              norm_even, w_in_even, w_pool, pool_scale, q_norm, k_norm, sb_bias, w_out_even,
              norm_odd, w_in_odd, conv_w, a_log, dt_bias, o_norm, w_out_odd):
    n_dec = x_sample.shape[0]
    past_len = page_table.shape[1] * cache_k.shape[2]
    bp = x_prompt.shape[0]
    hp, hs = x_prompt, x_sample
    pool_p, pool_s, kp_l, vp_l, ks_l, vs_l = [], [], [], [], [], []
    conv_p, conv_s, dp_l, ds_l = [], [], [], []
    for layer in range(DEPTH):
        i = layer // 2
        if layer % 2 == 0:
            ev = (norm_even[i], w_in_even[i], w_pool[i], pool_scale[i], q_norm[i], k_norm[i], sb_bias[i], w_out_even[i])
            zero_buf = jnp.zeros((bp, POOL_BUF, POOL_WIDTH), hp.dtype)
            hp, pb, kn, vn = even_layer(hp, 0, zero_buf, None, None, *ev)
            pool_p.append(pb); kp_l.append(kn); vp_l.append(vn)
            k_past = cache_k[i][page_table].reshape(n_dec, past_len, SB_HEADS, SB_HEAD_DIM)
            v_past = cache_v[i][page_table].reshape(n_dec, past_len, SB_HEADS, SB_HEAD_DIM)
            hs, pb, kn, vn = even_layer(hs, past_len, cache_pool[i], k_past, v_past, *ev)
            pool_s.append(pb); ks_l.append(kn); vs_l.append(vn)
        else:
            od = (norm_odd[i], w_in_odd[i], conv_w[i], a_log[i], dt_bias[i], o_norm[i], w_out_odd[i])
            zero_conv = jnp.zeros((bp, CONV_WIDTH - 1, DN_CONV_DIM), hp.dtype)
            zero_state = jnp.zeros((bp, DN_HEADS, DN_KEY_DIM, DN_VAL_DIM), hp.dtype)
            hp, cb, st = odd_layer(hp, zero_conv, zero_state, *od)
            conv_p.append(cb); dp_l.append(st)
            hs, cb, st = odd_layer(hs, state_conv[i], state_delta[i], *od)
            conv_s.append(cb); ds_l.append(st)
    return (hp, hs,
            jnp.stack(pool_p), jnp.stack(pool_s),
            jnp.stack(kp_l), jnp.stack(vp_l), jnp.stack(ks_l), jnp.stack(vs_l),
            jnp.stack(conv_p), jnp.stack(conv_s),
            jnp.stack(dp_l), jnp.stack(ds_l))
```

```python
import functools
import math

import jax
import jax.numpy as jnp
from jax import lax
from jax.experimental import pallas as pl
from jax.experimental.pallas import tpu as pltpu

F32 = jnp.float32
BF16 = jnp.bfloat16

NORM_EPS = 1e-6
POOL_WINDOWS = (2, 4, 8, 16)
POOL_HIST = max(POOL_WINDOWS) - 1
POOL_PAD = 16
SB_HEADS = 8
SB_HEAD_DIM = 64
DN_HEADS = 8
DN_DIM = 128
CONV_WIDTH = 4
CONV_HIST = CONV_WIDTH - 1
CONV_PAD = 8
DN_CHUNK = 64
LANES = 128
PAGE = 128
VMEM_LIMIT_BYTES = 48 * 1024 * 1024


def _cparams(*sem):
    return pltpu.CompilerParams(dimension_semantics=sem, vmem_limit_bytes=VMEM_LIMIT_BYTES)


def _dot(a, b):
    return jnp.dot(a, b, preferred_element_type=F32)


def _dot_nt(a, b):
    return lax.dot_general(a, b, (((1,), (1,)), ((), ())), preferred_element_type=F32)


def _split2(x):
    hi = x.astype(BF16)
    return hi, (x - hi.astype(F32)).astype(BF16)


def _split3(x):
    hi = x.astype(BF16)
    r = x - hi.astype(F32)
    mid = r.astype(BF16)
    return hi, mid, (r - mid.astype(F32)).astype(BF16)


def _dot_exact_rhs(a, b_bf16):
    hi, mid, lo = _split3(a)
    return _dot(hi, b_bf16) + _dot(mid, b_bf16) + _dot(lo, b_bf16)


def _dot_exact_lhs(a_bf16, b):
    hi, mid, lo = _split3(b)
    return _dot(a_bf16, hi) + _dot(a_bf16, mid) + _dot(a_bf16, lo)


def _dot_f32(a, b):
    ah, al = _split2(a)
    bh, bl = _split2(b)
    return _dot(ah, bh) + (_dot(ah, bl) + _dot(al, bh))


def _sigmoid(x):
    return 1.0 / (1.0 + jnp.exp(-x))


def _silu(x):
    return x * _sigmoid(x)


def _softplus_neg_abs(x):
    return jnp.log1p(jnp.exp(-jnp.abs(x)))


def _rms_rows(x, g):
    return x * lax.rsqrt(jnp.mean(x * x, axis=-1, keepdims=True) + NORM_EPS) * g


def _row_tile(m, want):
    t = min(m, want)
    assert m % t == 0, (m, t)
    return t


def _even_proj_kernel(x_ref, g_ref, w_ref, seg_ref, qg_ref, kg_ref,
                      u_ref, q_ref, k_ref, v_ref, kb_ref, vb_ref, gate_ref):
    xn = _rms_rows(x_ref[...], g_ref[...]).astype(BF16)
    pw = u_ref.shape[-1]

    def proj(c0, c1):
        return _dot(xn, w_ref[:, c0:c1])

    def head_norm(y, gain):
        hi, lo = _split2(y * y)
        ms = (_dot(hi, seg_ref[...]) + _dot(lo, seg_ref[...])) * (1.0 / SB_HEAD_DIM)
        return y * lax.rsqrt(ms + NORM_EPS) * gain

    u_ref[...] = proj(0, pw)
    q = head_norm(proj(pw, 2 * pw), qg_ref[...])
    q_ref[...] = (q * (SB_HEAD_DIM ** -0.5)).astype(BF16)
    k = head_norm(proj(2 * pw, 3 * pw), kg_ref[...])
    k_ref[...] = k
    kb_ref[...] = k.astype(BF16)
    v = proj(3 * pw, 4 * pw)
    v_ref[...] = v
    vb_ref[...] = v.astype(BF16)
    gate_ref[...] = proj(4 * pw, 6 * pw)


def _even_proj(x2d, norm_g, w_in, q_g, k_g):
    m, d = x2d.shape
    pw = SB_HEADS * SB_HEAD_DIM
    assert w_in.shape == (d, 6 * pw)
    tm = _row_tile(m, 512)
    lane_head = jnp.arange(pw) // SB_HEAD_DIM
    seg = (lane_head[:, None] == lane_head[None, :]).astype(BF16)
    row = lambda i: (i, 0)
    fixed = lambda i: (0, 0)
    outs = [((m, pw), F32), ((m, pw), BF16), ((m, pw), F32), ((m, pw), F32), ((m, pw), BF16), ((m, pw), BF16),
            ((m, 2 * pw), F32)]
    return pl.pallas_call(
        _even_proj_kernel,
        grid=(m // tm,),
        in_specs=[pl.BlockSpec((tm, d), row), pl.BlockSpec((1, d), fixed), pl.BlockSpec((d, 6 * pw), fixed),
                  pl.BlockSpec((pw, pw), fixed), pl.BlockSpec((1, pw), fixed), pl.BlockSpec((1, pw), fixed)],
        out_specs=[pl.BlockSpec((tm, s[-1]), row) for s, _ in outs],
        out_shape=[jax.ShapeDtypeStruct(s, t) for s, t in outs],
        compiler_params=_cparams("parallel"),
        name="even_in_proj",
    )(x2d, norm_g[None, :], w_in.astype(BF16), seg,
      jnp.tile(q_g, SB_HEADS)[None, :], jnp.tile(k_g, SB_HEADS)[None, :])


def _pool_kernel(u_ref, buf_ref, wp_ref, scale_ref, y_ref, nb_ref, ext_ref, *, start, tl):
    l = pl.program_id(1)
    lo = POOL_PAD - POOL_HIST

    @pl.when(l == 0)
    def _():
        ext_ref[0:lo, :] = jnp.zeros((lo, ext_ref.shape[-1]), F32)
        ext_ref[lo:POOL_PAD, :] = buf_ref[0]

    @pl.when(l > 0)
    def _():
        ext_ref[lo:POOL_PAD, :] = ext_ref[tl + lo:tl + POOL_PAD, :]

    ext_ref[POOL_PAD:POOL_PAD + tl, :] = u_ref[0]
    pos = start + l * tl + lax.broadcasted_iota(jnp.int32, (tl, 1), 0)
    for gi, w in enumerate(POOL_WINDOWS):
        cols = slice(gi * LANES, (gi + 1) * LANES)
        cur = ext_ref[POOL_PAD:POOL_PAD + tl, cols]
        win = cur
        for i in range(1, w):
            win = win + ext_ref[POOL_PAD - i:POOL_PAD - i + tl, cols]
        cnt = jnp.minimum(pos + 1, w).astype(F32)
        d = win / cnt - cur
        y_ref[0, :, cols] = _dot(d.astype(BF16), wp_ref[gi]) * scale_ref[:, cols]

    @pl.when(l == pl.num_programs(1) - 1)
    def _():
        nb_ref[0] = ext_ref[tl + lo:tl + POOL_PAD, :]


def _pool_mix(u, buf, start, w_pool, scale):
    b, length, p = u.shape
    assert p == len(POOL_WINDOWS) * LANES and buf.shape == (b, POOL_HIST, p)
    tl = _row_tile(length, 512)
    assert tl == length or tl >= POOL_PAD
    seq = lambda i, l: (i, l, 0)
    per_b = lambda i, l: (i, 0, 0)
    return pl.pallas_call(
        functools.partial(_pool_kernel, start=start, tl=tl),
        grid=(b, length // tl),
        in_specs=[pl.BlockSpec((1, tl, p), seq), pl.BlockSpec((1, POOL_HIST, p), per_b),
                  pl.BlockSpec(w_pool.shape, lambda i, l: (0, 0, 0)), pl.BlockSpec((1, p), lambda i, l: (0, 0))],
        out_specs=[pl.BlockSpec((1, tl, p), seq), pl.BlockSpec((1, POOL_HIST, p), per_b)],
        out_shape=[jax.ShapeDtypeStruct((b, length, p), F32), jax.ShapeDtypeStruct((b, POOL_HIST, p), F32)],
        scratch_shapes=[pltpu.VMEM((POOL_PAD + tl, p), F32)],
        compiler_params=_cparams("parallel", "arbitrary"),
        name="pool_mix",
    )(u, buf, w_pool.astype(BF16), scale[None, :])


def _stick_break_tile(z, carry, tri, mask):
    t = _softplus_neg_abs(z)
    log_beta = jnp.minimum(z, 0.0) - t
    log_keep = -jnp.maximum(z, 0.0) - t
    if mask is not None:
        log_keep = jnp.where(mask, log_keep, 0.0)
    hi, lo = _split2(log_keep)
    suffix = _dot(hi, tri) + _dot(lo, tri) + carry
    a = jnp.exp(log_beta + suffix)
    if mask is not None:
        a = jnp.where(mask, a, 0.0)
    return a, carry + jnp.sum(log_keep, axis=-1, keepdims=True)


def _later_key_matrix(n):
    return (lax.broadcasted_iota(jnp.int32, (n, n), 0) > lax.broadcasted_iota(jnp.int32, (n, n), 1)).astype(BF16)


def _sb_prompt_kernel(bias_ref, q_ref, k_ref, v_ref, o_ref, acc_ref, *, t):
    hp = pl.program_id(1)
    qi = pl.program_id(2)
    q = q_ref[0]
    low_lanes = lax.broadcasted_iota(jnp.int32, (t, LANES), 1) < SB_HEAD_DIM
    tri = _later_key_matrix(t)
    causal = lax.broadcasted_iota(jnp.int32, (t, t), 1) < lax.broadcasted_iota(jnp.int32, (t, t), 0)
    for hh in range(2):
        qh = jnp.where(low_lanes if hh == 0 else jnp.logical_not(low_lanes), q, jnp.zeros_like(q))
        bias = bias_ref[2 * hp + hh]

        def tile(kb, carry, mask):
            start = pl.multiple_of(kb * t, t)
            z = _dot_nt(qh, k_ref[0, pl.ds(start, t), :]) + bias
            a, carry = _stick_break_tile(z, carry, tri, mask)
            acc_ref[hh] += _dot(a.astype(BF16), v_ref[0, pl.ds(start, t), :])
            return carry

        acc_ref[hh] = jnp.zeros((t, LANES), F32)
        carry = tile(qi, jnp.zeros((t, 1), F32), causal)
        lax.fori_loop(0, qi, lambda j, c: tile(qi - 1 - j, c, None), carry)
    o_ref[0] = jnp.where(low_lanes, acc_ref[0], acc_ref[1])


def _sb_prompt(q, k, v, bias):
    b, length, width = q.shape
    assert width == SB_HEADS * SB_HEAD_DIM and 2 * SB_HEAD_DIM == LANES
    t = _row_tile(length, 256)
    q_blk = lambda i, h, j: (i, j, h)
    kv_blk = lambda i, h, j: (i, 0, h)
    return pl.pallas_call(
        functools.partial(_sb_prompt_kernel, t=t),
        grid=(b, width // LANES, length // t),
        in_specs=[pl.BlockSpec(memory_space=pltpu.SMEM), pl.BlockSpec((1, t, LANES), q_blk),
                  pl.BlockSpec((1, length, LANES), kv_blk), pl.BlockSpec((1, length, LANES), kv_blk)],
        out_specs=pl.BlockSpec((1, t, LANES), q_blk),
        scratch_shapes=[pltpu.VMEM((2, t, LANES), F32)],
        out_shape=jax.ShapeDtypeStruct((b, length, width), F32),
        compiler_params=_cparams("parallel", "parallel", "arbitrary"),
        name="stick_break_prompt",
    )(bias, q, k, v)


def _sb_sample_kernel(pt_ref, q_ref, bias_ref, kn_ref, vn_ref, *refs, n_pages, n_new):
    k_refs, v_refs, o_ref = refs[:n_pages], refs[n_pages:2 * n_pages], refs[2 * n_pages]
    kpad_ref, vpad_ref = refs[2 * n_pages + 1:]
    rows = SB_HEADS * n_new
    width = SB_HEADS * SB_HEAD_DIM
    row_i = lax.broadcasted_iota(jnp.int32, (rows, width), 0)
    lane_i = lax.broadcasted_iota(jnp.int32, (rows, width), 1)
    own_head = (row_i // n_new) == (lane_i // SB_HEAD_DIM)
    qbd = jnp.where(own_head, q_ref[0], jnp.zeros_like(q_ref[0]))
    bias = bias_ref[...]
    tri = _later_key_matrix(PAGE)

    kpad_ref[...] = jnp.zeros(kpad_ref.shape, BF16)
    vpad_ref[...] = jnp.zeros(vpad_ref.shape, BF16)
    kpad_ref[0:kn_ref.shape[1], :] = kn_ref[0].astype(BF16)
    vpad_ref[0:vn_ref.shape[1], :] = vn_ref[0].astype(BF16)
    qpos = lax.broadcasted_iota(jnp.int32, (rows, PAGE), 0) % n_new
    kpos = lax.broadcasted_iota(jnp.int32, (rows, PAGE), 1)

    carry = jnp.zeros((rows, 1), F32)
    a, carry = _stick_break_tile(_dot_nt(qbd, kpad_ref[...]) + bias, carry, tri, kpos < qpos)
    out = _dot(a.astype(BF16), vpad_ref[...])
    def page_rows(ref):
        heads = [ref[0, pl.ds(h, PAGE, stride=SB_HEADS), :] for h in range(SB_HEADS)]
        return jnp.concatenate(heads, axis=1).astype(BF16)

    for p in reversed(range(n_pages)):
        a, carry = _stick_break_tile(_dot_nt(qbd, page_rows(k_refs[p])) + bias, carry, tri, None)
        out = out + _dot(a.astype(BF16), page_rows(v_refs[p]))
    out = jnp.where(own_head, out, 0.0)
    acc = out[0:n_new]
    for h in range(1, SB_HEADS):
        acc = acc + out[h * n_new:(h + 1) * n_new]
    o_ref[0] = acc


def _sb_sample(q, k_new, v_new, bias, cache_k, cache_v, page_table):
    b, n_new, width = q.shape
    n_pages = page_table.shape[1]
    n_phys = cache_k.shape[0]
    assert cache_k.shape[1:] == (PAGE, SB_HEADS, SB_HEAD_DIM) and n_new <= 8
    ck = cache_k.reshape(n_phys, PAGE * SB_HEADS, SB_HEAD_DIM)
    cv = cache_v.reshape(n_phys, PAGE * SB_HEADS, SB_HEAD_DIM)
    rows = SB_HEADS * n_new
    pad = ((0, 0), (0, 8 - n_new), (0, 0))
    per_b = lambda i, pt: (i, 0, 0)
    page_specs = [pl.BlockSpec((1, PAGE * SB_HEADS, SB_HEAD_DIM),
                               functools.partial(lambda i, pt, p: (pt[i, p], 0, 0), p=p)) for p in range(n_pages)]
    return pl.pallas_call(
        functools.partial(_sb_sample_kernel, n_pages=n_pages, n_new=n_new),
        grid_spec=pltpu.PrefetchScalarGridSpec(
            num_scalar_prefetch=1,
            grid=(b,),
            in_specs=[pl.BlockSpec((1, rows, width), per_b), pl.BlockSpec((rows, 1), lambda i, pt: (0, 0)),
                      pl.BlockSpec((1, 8, width), per_b), pl.BlockSpec((1, 8, width), per_b)] + page_specs + page_specs,
            out_specs=pl.BlockSpec((1, n_new, width), per_b),
            scratch_shapes=[pltpu.VMEM((PAGE, width), BF16), pltpu.VMEM((PAGE, width), BF16)]),
        out_shape=jax.ShapeDtypeStruct((b, n_new, width), F32),
        compiler_params=_cparams("parallel"),
        name="stick_break_sample",
    )(page_table, jnp.tile(q, (1, SB_HEADS, 1)), jnp.repeat(bias, n_new)[:, None],
      jnp.pad(k_new, pad), jnp.pad(v_new, pad), *([ck] * n_pages), *([cv] * n_pages))


def _even_out_kernel(h_ref, py_ref, sb_ref, gate_ref, w_ref, y_ref):
    pw = py_ref.shape[-1]
    m_pool = (py_ref[...] * _silu(gate_ref[:, 0:pw])).astype(BF16)
    m_sb = (sb_ref[...] * _silu(gate_ref[:, pw:2 * pw])).astype(BF16)
    y_ref[...] = h_ref[...] + (_dot(m_pool, w_ref[0:pw, :]) + _dot(m_sb, w_ref[pw:2 * pw, :]))


def _even_out(h2d, pool_y, sb, gate, w_out):
    m, d = h2d.shape
    pw = pool_y.shape[-1]
    tm = _row_tile(m, 512)
    row = lambda i: (i, 0)
    return pl.pallas_call(
        _even_out_kernel,
        grid=(m // tm,),
        in_specs=[pl.BlockSpec((tm, d), row), pl.BlockSpec((tm, pw), row), pl.BlockSpec((tm, pw), row),
                  pl.BlockSpec((tm, 2 * pw), row), pl.BlockSpec((2 * pw, d), lambda i: (0, 0))],
        out_specs=pl.BlockSpec((tm, d), row),
        out_shape=jax.ShapeDtypeStruct((m, d), F32),
        compiler_params=_cparams("parallel"),
        name="even_out_proj",
    )(h2d, pool_y, sb, gate, w_out.astype(BF16))


def _even_layer(h, start, pool_buf, past, norm_g, w_in, w_pool, pool_scale, q_g, k_g, sb_bias, w_out):
    b, length, d = h.shape
    h2d = h.reshape(b * length, d)
    u, q, k, v, kb, vb, gate = _even_proj(h2d, norm_g, w_in, q_g, k_g)
    seq = lambda a: a.reshape(b, length, a.shape[-1])
    pool_y, new_buf = _pool_mix(seq(u), pool_buf, start, w_pool, pool_scale)
    if past is None:
        sb = _sb_prompt(seq(q), seq(kb), seq(vb), sb_bias)
    else:
        sb = _sb_sample(seq(q), seq(k), seq(v), sb_bias, *past)
    y = _even_out(h2d, pool_y.reshape(b * length, -1), sb.reshape(b * length, -1), gate, w_out)
    heads = lambda a: a.reshape(b, length, SB_HEADS, SB_HEAD_DIM)
    return y.reshape(b, length, d), new_buf, heads(k), heads(v)


def _odd_proj_kernel(x_ref, g_ref, w_ref, wba_ref, qkv_ref, z_ref, ba_ref, *, chunk):
    xn = _rms_rows(x_ref[...], g_ref[...]).astype(BF16)
    n_qkv = qkv_ref.shape[-1]
    for c0 in range(0, n_qkv, chunk):
        qkv_ref[:, c0:c0 + chunk] = _dot(xn, w_ref[:, c0:c0 + chunk])
    for c0 in range(0, z_ref.shape[-1], chunk):
        z_ref[:, c0:c0 + chunk] = _dot(xn, w_ref[:, n_qkv + c0:n_qkv + c0 + chunk])
    ba_ref[...] = _dot(xn, wba_ref[...])


def _odd_proj(x2d, norm_g, w_in):
    m, d = x2d.shape
    n_qkv, n_z = 3 * DN_HEADS * DN_DIM, DN_HEADS * DN_DIM
    assert w_in.shape == (d, n_qkv + n_z + 2 * DN_HEADS)
    tm = _row_tile(m, 256)
    w_main = w_in[:, :n_qkv + n_z].astype(BF16)
    w_ba = jnp.pad(w_in[:, n_qkv + n_z:], ((0, 0), (0, LANES - 2 * DN_HEADS))).astype(BF16)
    row = lambda i: (i, 0)
    fixed = lambda i: (0, 0)
    return pl.pallas_call(
        functools.partial(_odd_proj_kernel, chunk=512),
        grid=(m // tm,),
        in_specs=[pl.BlockSpec((tm, d), row), pl.BlockSpec((1, d), fixed), pl.BlockSpec(w_main.shape, fixed),
                  pl.BlockSpec(w_ba.shape, fixed)],
        out_specs=[pl.BlockSpec((tm, n_qkv), row), pl.BlockSpec((tm, n_z), row), pl.BlockSpec((tm, LANES), row)],
        out_shape=[jax.ShapeDtypeStruct((m, n_qkv), F32), jax.ShapeDtypeStruct((m, n_z), F32),
                   jax.ShapeDtypeStruct((m, LANES), F32)],
        compiler_params=_cparams("parallel"),
        name="odd_in_proj",
    )(x2d, norm_g[None, :], w_main, w_ba)


def _conv_kernel(x_ref, buf_ref, cw_ref, ba_ref, ab_ref, q_ref, k_ref, v_ref, gb_ref, nb_ref, ext_ref, *, tl):
    l = pl.program_id(1)
    lo = CONV_PAD - CONV_HIST
    width = ext_ref.shape[-1]

    @pl.when(l == 0)
    def _():
        ext_ref[0:lo, :] = jnp.zeros((lo, width), F32)
        ext_ref[lo:CONV_PAD, :] = buf_ref[0]

    @pl.when(l > 0)
    def _():
        ext_ref[lo:CONV_PAD, :] = ext_ref[tl + lo:tl + CONV_PAD, :]

    ext_ref[CONV_PAD:CONV_PAD + tl, :] = x_ref[0]
    n_qk = DN_HEADS * DN_DIM
    for c in range(width // LANES):
        cols = slice(c * LANES, (c + 1) * LANES)
        conv = ext_ref[lo:lo + tl, cols] * cw_ref[0:1, cols]
        for j in range(1, CONV_WIDTH):
            conv = conv + ext_ref[lo + j:lo + j + tl, cols] * cw_ref[j:j + 1, cols]
        y = _silu(conv)
        oc = slice((c % DN_HEADS) * LANES, (c % DN_HEADS + 1) * LANES)
        if c * LANES < 2 * n_qk:
            y = y * lax.rsqrt(jnp.sum(y * y, axis=-1, keepdims=True) + NORM_EPS)
            if c * LANES < n_qk:
                q_ref[0, :, oc] = y * (DN_DIM ** -0.5)
            else:
                k_ref[0, :, oc] = y
        else:
            v_ref[0, :, oc] = y

    ba = ba_ref[0]
    x = ba + ab_ref[1:2, :]
    g = -jnp.exp(ab_ref[0:1, :]) * (jnp.maximum(x, 0.0) + _softplus_neg_abs(x))
    is_beta = lax.broadcasted_iota(jnp.int32, ba.shape, 1) < DN_HEADS
    gb_ref[0] = jnp.where(is_beta, _sigmoid(ba), g)

    @pl.when(l == pl.num_programs(1) - 1)
    def _():
        nb_ref[0] = ext_ref[tl + lo:tl + CONV_PAD, :]


def _conv_gates(qkv, conv_buf, conv_w, ba, a_log, dt_bias):
    b, length, width = qkv.shape
    n = DN_HEADS * DN_DIM
    assert width == 3 * n and conv_buf.shape == (b, CONV_HIST, width)
    tl = _row_tile(length, 256)
    assert tl == length or tl >= CONV_PAD
    park = lambda vec: jnp.pad(vec, (DN_HEADS, LANES - 2 * DN_HEADS))
    ab = jnp.stack([park(a_log), park(dt_bias)])
    seq = lambda i, l: (i, l, 0)
    per_b = lambda i, l: (i, 0, 0)
    fixed = lambda i, l: (0, 0)
    return pl.pallas_call(
        functools.partial(_conv_kernel, tl=tl),
        grid=(b, length // tl),
        in_specs=[pl.BlockSpec((1, tl, width), seq), pl.BlockSpec((1, CONV_HIST, width), per_b),
                  pl.BlockSpec((CONV_WIDTH, width), fixed), pl.BlockSpec((1, tl, LANES), seq),
                  pl.BlockSpec((2, LANES), fixed)],
        out_specs=[pl.BlockSpec((1, tl, n), seq)] * 3 + [pl.BlockSpec((1, tl, LANES), seq),
                                                          pl.BlockSpec((1, CONV_HIST, width), per_b)],
        out_shape=[jax.ShapeDtypeStruct((b, length, n), F32)] * 3
        + [jax.ShapeDtypeStruct((b, length, LANES), F32), jax.ShapeDtypeStruct((b, CONV_HIST, width), F32)],
        scratch_shapes=[pltpu.VMEM((CONV_PAD + tl, width), F32)],
        compiler_params=_cparams("parallel", "arbitrary"),
        name="conv_norm_gates",
    )(qkv, conv_buf, conv_w, ba, ab)


def _unit_lower_inverse(lmat, c):
    eye = (lax.broadcasted_iota(jnp.int32, (c, c), 0) == lax.broadcasted_iota(jnp.int32, (c, c), 1)).astype(F32)
    power = -lmat
    inv = eye + power
    span = 2
    while span < c:
        power = _dot_f32(power, power)
        inv = inv + _dot_f32(inv, power)
        span *= 2
    return inv


def _delta_kernel(q_ref, k_ref, v_ref, gb_ref, s0_ref, o_ref, s_out_ref, s_ref, *, c, n_chunks):
    l = pl.program_id(1)

    @pl.when(l == 0)
    def _():
        s_ref[...] = s0_ref[0]

    row = lax.broadcasted_iota(jnp.int32, (c, c), 0)
    col = lax.broadcasted_iota(jnp.int32, (c, c), 1)
    incl = row >= col
    strict = row > col
    prefix = incl.astype(BF16)
    eye_lanes = (lax.broadcasted_iota(jnp.int32, (LANES, LANES), 0)
                 == lax.broadcasted_iota(jnp.int32, (LANES, LANES), 1)).astype(BF16)

    def chunk(ci, _):
        r0 = pl.multiple_of(ci * c, c)
        rows = pl.ds(r0, c)
        gb = gb_ref[0, rows, :]
        gcs = _dot_exact_lhs(prefix, gb)
        h3, m3, l3 = _split3(gcs)
        gcs_t = _dot_nt(eye_lanes, h3) + _dot_nt(eye_lanes, m3) + _dot_nt(eye_lanes, l3)
        for h in range(DN_HEADS):
            cols = slice(h * DN_DIM, (h + 1) * DN_DIM)
            q, k, v = q_ref[0, rows, cols], k_ref[0, rows, cols], v_ref[0, rows, cols]
            beta = gb[:, h:h + 1]
            gc = gcs[:, DN_HEADS + h:DN_HEADS + h + 1]
            gc_row = gcs_t[DN_HEADS + h:DN_HEADS + h + 1, :]
            g_last = gcs[c - 1:c, DN_HEADS + h:DN_HEADS + h + 1]
            decay = jnp.where(incl, jnp.exp(jnp.where(incl, gc - gc_row, 0.0)), 0.0)
            kb = k * beta
            k16 = k.astype(BF16)
            lmat = jnp.where(strict, _dot_nt(kb.astype(BF16), k16) * decay, 0.0)
            tmat = _unit_lower_inverse(lmat, c).astype(BF16)
            u = _dot(tmat, (v * beta).astype(BF16))
            w = _dot(tmat, (kb * jnp.exp(gc)).astype(BF16))
            s = s_ref[h]
            s16 = s.astype(BF16)
            v_new = u - _dot(w.astype(BF16), s16)
            v16 = v_new.astype(BF16)
            att = _dot_nt(q.astype(BF16), k16) * decay
            o_ref[0, rows, cols] = _dot((q * jnp.exp(gc)).astype(BF16), s16) + _dot(att.astype(BF16), v16)
            kd_t = _dot_nt(eye_lanes, (k * jnp.exp(g_last - gc)).astype(BF16))
            s_ref[h] = s * jnp.exp(g_last) + _dot(kd_t.astype(BF16), v16)
        return 0

    lax.fori_loop(0, n_chunks, chunk, 0)

    @pl.when(l == pl.num_programs(1) - 1)
    def _():
        s_out_ref[0] = s_ref[...]


def _gated_delta(q, k, v, gb, state, c):
    b, length, n = q.shape
    assert n == DN_HEADS * DN_DIM and state.shape == (b, DN_HEADS, DN_DIM, DN_DIM) and length % c == 0
    tl = c * min(length // c, 4)
    assert length % tl == 0
    seq = lambda i, l: (i, l, 0)
    per_b = lambda i, l: (i, 0, 0, 0)
    return pl.pallas_call(
        functools.partial(_delta_kernel, c=c, n_chunks=tl // c),
        grid=(b, length // tl),
        in_specs=[pl.BlockSpec((1, tl, n), seq)] * 3 + [pl.BlockSpec((1, tl, LANES), seq),
                                                        pl.BlockSpec((1,) + state.shape[1:], per_b)],
        out_specs=[pl.BlockSpec((1, tl, n), seq), pl.BlockSpec((1,) + state.shape[1:], per_b)],
        out_shape=[jax.ShapeDtypeStruct((b, length, n), F32), jax.ShapeDtypeStruct(state.shape, F32)],
        scratch_shapes=[pltpu.VMEM(state.shape[1:], F32)],
        compiler_params=_cparams("parallel", "arbitrary"),
        name="gated_delta",
    )(q, k, v, gb, state)


def _odd_out_kernel(h_ref, o_ref, z_ref, og_ref, w_ref, y_ref):
    acc = h_ref[...]
    for hd in range(DN_HEADS):
        cols = slice(hd * DN_DIM, (hd + 1) * DN_DIM)
        o = _rms_rows(o_ref[:, cols], og_ref[...]) * _silu(z_ref[:, cols])
        acc = acc + _dot(o.astype(BF16), w_ref[cols, :])
    y_ref[...] = acc


def _odd_out(h2d, o2d, z, o_g, w_out):
    m, d = h2d.shape
    n = o2d.shape[-1]
    tm = _row_tile(m, 512)
    row = lambda i: (i, 0)
    fixed = lambda i: (0, 0)
    return pl.pallas_call(
        _odd_out_kernel,
        grid=(m // tm,),
        in_specs=[pl.BlockSpec((tm, d), row), pl.BlockSpec((tm, n), row), pl.BlockSpec((tm, n), row),
                  pl.BlockSpec((1, DN_DIM), fixed), pl.BlockSpec((n, d), fixed)],
        out_specs=pl.BlockSpec((tm, d), row),
        out_shape=jax.ShapeDtypeStruct((m, d), F32),
        compiler_params=_cparams("parallel"),
        name="odd_out_proj",
    )(h2d, o2d, z, o_g[None, :], w_out.astype(BF16))


def _odd_layer(h, conv_buf, state, norm_g, w_in, conv_w, a_log, dt_bias, o_g, w_out):
    b, length, d = h.shape
    h2d = h.reshape(b * length, d)
    qkv, z, ba = _odd_proj(h2d, norm_g, w_in)
    seq = lambda a: a.reshape(b, length, a.shape[-1])
    q, k, v, gb, new_buf = _conv_gates(seq(qkv), conv_buf, conv_w, seq(ba), a_log, dt_bias)
    c = DN_CHUNK if length % DN_CHUNK == 0 else length
    c_pad = -(-c // 8) * 8
    if c_pad != c:
        pad = lambda a: jnp.pad(a, ((0, 0), (0, c_pad - c), (0, 0)))
        o, new_state = _gated_delta(pad(q), pad(k), pad(v), pad(gb), state, c_pad)
        o = o[:, :length]
    else:
        o, new_state = _gated_delta(q, k, v, gb, state, c)
    y = _odd_out(h2d, o.reshape(b * length, -1), z, o_g, w_out)
    return y.reshape(b, length, d), new_buf, new_state


def kernel(x_prompt, x_sample, cache_pool, cache_k, cache_v, state_conv, state_delta, page_table, norm_even, w_in_even, w_pool, pool_scale, q_norm, k_norm, sb_bias, w_out_even, norm_odd, w_in_odd, conv_w, a_log, dt_bias, o_norm, w_out_odd):
    bp = x_prompt.shape[0]
    past_len = page_table.shape[1] * cache_k.shape[2]
    depth = norm_even.shape[0] + norm_odd.shape[0]
    hp, hs = x_prompt, x_sample
    outs = {name: [] for name in ("pool_p", "pool_s", "k_p", "v_p", "k_s", "v_s", "conv_p", "conv_s", "st_p", "st_s")}
    for layer in range(depth):
        i = layer // 2
        if layer % 2 == 0:
            ev = (norm_even[i], w_in_even[i], w_pool[i], pool_scale[i], q_norm[i], k_norm[i], sb_bias[i], w_out_even[i])
            zero_buf = jnp.zeros((bp, POOL_HIST, cache_pool.shape[-1]), F32)
            hp, pb, kn, vn = _even_layer(hp, 0, zero_buf, None, *ev)
            outs["pool_p"].append(pb), outs["k_p"].append(kn), outs["v_p"].append(vn)
            hs, pb, kn, vn = _even_layer(hs, past_len, cache_pool[i], (cache_k[i], cache_v[i], page_table), *ev)
            outs["pool_s"].append(pb), outs["k_s"].append(kn), outs["v_s"].append(vn)
        else:
            od = (norm_odd[i], w_in_odd[i], conv_w[i], a_log[i], dt_bias[i], o_norm[i], w_out_odd[i])
            zero_conv = jnp.zeros((bp, CONV_HIST, state_conv.shape[-1]), F32)
            zero_state = jnp.zeros((bp,) + state_delta.shape[2:], F32)
            hp, cb, st = _odd_layer(hp, zero_conv, zero_state, *od)
            outs["conv_p"].append(cb), outs["st_p"].append(st)
            hs, cb, st = _odd_layer(hs, state_conv[i], state_delta[i], *od)
            outs["conv_s"].append(cb), outs["st_s"].append(st)
    stack = lambda name: jnp.stack(outs[name])
    return (hp, hs, stack("pool_p"), stack("pool_s"), stack("k_p"), stack("v_p"), stack("k_s"), stack("v_s"),
            stack("conv_p"), stack("conv_s"), stack("st_p"), stack("st_s"))
```

```python
import functools
import math

import jax
import jax.numpy as jnp
from jax import lax
from jax.experimental import pallas as pl
from jax.experimental.pallas import tpu as pltpu

F32 = jnp.float32
BF16 = jnp.bfloat16

NORM_EPS = 1e-6
POOL_WINDOWS = (2, 4, 8, 16)
POOL_HIST = max(POOL_WINDOWS) - 1
POOL_PAD = 16
SB_HEADS = 8
SB_HEAD_DIM = 64
DN_HEADS = 8
DN_DIM = 128
CONV_WIDTH = 4
CONV_HIST = CONV_WIDTH - 1
CONV_PAD = 8
DN_CHUNK = 64
LANES = 128
PAGE = 128
VMEM_LIMIT_BYTES = 48 * 1024 * 1024


def _cparams(*sem):
    return pltpu.CompilerParams(dimension_semantics=sem, vmem_limit_bytes=VMEM_LIMIT_BYTES)


def _dot(a, b):
    return jnp.dot(a, b, preferred_element_type=F32)


def _dot_nt(a, b):
    return lax.dot_general(a, b, (((1,), (1,)), ((), ())), preferred_element_type=F32)


def _split2(x):
    hi = x.astype(BF16)
    return hi, (x - hi.astype(F32)).astype(BF16)


def _split3(x):
    hi = x.astype(BF16)
    r = x - hi.astype(F32)
    mid = r.astype(BF16)
    return hi, mid, (r - mid.astype(F32)).astype(BF16)


def _dot_exact_rhs(a, b_bf16):
    hi, mid, lo = _split3(a)
    return _dot(hi, b_bf16) + _dot(mid, b_bf16) + _dot(lo, b_bf16)


def _dot_exact_lhs(a_bf16, b):
    hi, mid, lo = _split3(b)
    return _dot(a_bf16, hi) + _dot(a_bf16, mid) + _dot(a_bf16, lo)


def _dot_f32(a, b):
    ah, al = _split2(a)
    bh, bl = _split2(b)
    return _dot(ah, bh) + (_dot(ah, bl) + _dot(al, bh))


def _sigmoid(x):
    return 1.0 / (1.0 + jnp.exp(-x))


def _silu(x):
    return x * _sigmoid(x)


def _softplus_neg_abs(x):
    return jnp.log1p(jnp.exp(-jnp.abs(x)))


def _rms_rows(x, g):
    return x * lax.rsqrt(jnp.mean(x * x, axis=-1, keepdims=True) + NORM_EPS) * g


def _row_tile(m, want):
    t = min(m, want)
    assert m % t == 0, (m, t)
    return t


def _even_proj_kernel(x_ref, g_ref, w_ref, seg_ref, qg_ref, kg_ref,
                      u_ref, q_ref, k_ref, v_ref, kb_ref, vb_ref, gate_ref):
    xn = _rms_rows(x_ref[...], g_ref[...]).astype(BF16)
    pw = u_ref.shape[-1]

    def proj(c0, c1):
        return _dot(xn, w_ref[:, c0:c1])

    def head_norm(y, gain):
        hi, lo = _split2(y * y)
        ms = (_dot(hi, seg_ref[...]) + _dot(lo, seg_ref[...])) * (1.0 / SB_HEAD_DIM)
        return y * lax.rsqrt(ms + NORM_EPS) * gain

    u_ref[...] = proj(0, pw)
    q = head_norm(proj(pw, 2 * pw), qg_ref[...])
    q_ref[...] = (q * (SB_HEAD_DIM ** -0.5)).astype(BF16)
    k = head_norm(proj(2 * pw, 3 * pw), kg_ref[...])
    k_ref[...] = k
    kb_ref[...] = k.astype(BF16)
    v = proj(3 * pw, 4 * pw)
    v_ref[...] = v
    vb_ref[...] = v.astype(BF16)
    gate_ref[...] = proj(4 * pw, 6 * pw)


def _even_proj(x2d, norm_g, w_in, q_g, k_g):
    m, d = x2d.shape
    pw = SB_HEADS * SB_HEAD_DIM
    assert w_in.shape == (d, 6 * pw)
    tm = _row_tile(m, 512)
    lane_head = jnp.arange(pw) // SB_HEAD_DIM
    seg = (lane_head[:, None] == lane_head[None, :]).astype(BF16)
    row = lambda i: (i, 0)
    fixed = lambda i: (0, 0)
    outs = [((m, pw), F32), ((m, pw), BF16), ((m, pw), F32), ((m, pw), F32), ((m, pw), BF16), ((m, pw), BF16),
            ((m, 2 * pw), F32)]
    return pl.pallas_call(
        _even_proj_kernel,
        grid=(m // tm,),
        in_specs=[pl.BlockSpec((tm, d), row), pl.BlockSpec((1, d), fixed), pl.BlockSpec((d, 6 * pw), fixed),
                  pl.BlockSpec((pw, pw), fixed), pl.BlockSpec((1, pw), fixed), pl.BlockSpec((1, pw), fixed)],
        out_specs=[pl.BlockSpec((tm, s[-1]), row) for s, _ in outs],
        out_shape=[jax.ShapeDtypeStruct(s, t) for s, t in outs],
        compiler_params=_cparams("parallel"),
        name="even_in_proj",
    )(x2d, norm_g[None, :], w_in.astype(BF16), seg,
      jnp.tile(q_g, SB_HEADS)[None, :], jnp.tile(k_g, SB_HEADS)[None, :])


def _pool_kernel(u_ref, buf_ref, wp_ref, scale_ref, y_ref, nb_ref, ext_ref, *, start, tl):
    l = pl.program_id(1)
    lo = POOL_PAD - POOL_HIST

    @pl.when(l == 0)
    def _():
        ext_ref[0:lo, :] = jnp.zeros((lo, ext_ref.shape[-1]), F32)
        ext_ref[lo:POOL_PAD, :] = buf_ref[0]

    @pl.when(l > 0)
    def _():
        ext_ref[lo:POOL_PAD, :] = ext_ref[tl + lo:tl + POOL_PAD, :]

    ext_ref[POOL_PAD:POOL_PAD + tl, :] = u_ref[0]
    pos = start + l * tl + lax.broadcasted_iota(jnp.int32, (tl, 1), 0)
    for gi, w in enumerate(POOL_WINDOWS):
        cols = slice(gi * LANES, (gi + 1) * LANES)
        cur = ext_ref[POOL_PAD:POOL_PAD + tl, cols]
        win = cur
        for i in range(1, w):
            win = win + ext_ref[POOL_PAD - i:POOL_PAD - i + tl, cols]
        cnt = jnp.minimum(pos + 1, w).astype(F32)
        d = win / cnt - cur
        y_ref[0, :, cols] = _dot(d.astype(BF16), wp_ref[gi]) * scale_ref[:, cols]

    @pl.when(l == pl.num_programs(1) - 1)
    def _():
        nb_ref[0] = ext_ref[tl + lo:tl + POOL_PAD, :]


def _pool_mix(u, buf, start, w_pool, scale):
    b, length, p = u.shape
    assert p == len(POOL_WINDOWS) * LANES and buf.shape == (b, POOL_HIST, p)
    tl = _row_tile(length, 512)
    assert tl == length or tl >= POOL_PAD
    seq = lambda i, l: (i, l, 0)
    per_b = lambda i, l: (i, 0, 0)
    return pl.pallas_call(
        functools.partial(_pool_kernel, start=start, tl=tl),
        grid=(b, length // tl),
        in_specs=[pl.BlockSpec((1, tl, p), seq), pl.BlockSpec((1, POOL_HIST, p), per_b),
                  pl.BlockSpec(w_pool.shape, lambda i, l: (0, 0, 0)), pl.BlockSpec((1, p), lambda i, l: (0, 0))],
        out_specs=[pl.BlockSpec((1, tl, p), seq), pl.BlockSpec((1, POOL_HIST, p), per_b)],
        out_shape=[jax.ShapeDtypeStruct((b, length, p), F32), jax.ShapeDtypeStruct((b, POOL_HIST, p), F32)],
        scratch_shapes=[pltpu.VMEM((POOL_PAD + tl, p), F32)],
        compiler_params=_cparams("parallel", "arbitrary"),
        name="pool_mix",
    )(u, buf, w_pool.astype(BF16), scale[None, :])


def _stick_break_tiles(zs, carries, tri, masks, chained=False):
    n = range(len(zs))
    neg = [jnp.minimum(z, 0.0) for z in zs]
    t = [jnp.log(1.0 + jnp.exp(-jnp.abs(z))) for z in zs]
    log_beta = [neg[i] - t[i] for i in n]
    log_keep = [(neg[i] - zs[i]) - t[i] for i in n]
    log_keep = [lk if m is None else jnp.where(m, lk, 0.0) for lk, m in zip(log_keep, masks)]
    totals = [jnp.sum(lk, axis=-1, keepdims=True) for lk in log_keep]
    if chained:
        carry_in = [carries[0]]
        for i in n:
            carry_in.append(carry_in[i] + totals[i])
        new_carries = carry_in[-1:]
    else:
        carry_in = carries
        new_carries = [carries[i] + totals[i] for i in n]
    parts = [_split2(lk) for lk in log_keep]
    suffix = [_dot(hi, tri) + _dot(lo, tri) + carry_in[i] for i, (hi, lo) in enumerate(parts)]
    a = [jnp.exp(log_beta[i] + suffix[i]) for i in n]
    a = [x if m is None else jnp.where(m, x, 0.0) for x, m in zip(a, masks)]
    return [x.astype(BF16) for x in a], new_carries


def _later_key_matrix(n):
    return (lax.broadcasted_iota(jnp.int32, (n, n), 0) > lax.broadcasted_iota(jnp.int32, (n, n), 1)).astype(BF16)


def _sb_prompt_kernel(bias_ref, q_ref, k_ref, v_ref, o_ref, acc_ref, carry_ref, *, t):
    qi = pl.program_id(1)
    heads = range(SB_HEADS)
    low_lanes = lax.broadcasted_iota(jnp.int32, (t, LANES), 1) < SB_HEAD_DIM
    tri = _later_key_matrix(t)
    causal = lax.broadcasted_iota(jnp.int32, (t, t), 1) < lax.broadcasted_iota(jnp.int32, (t, t), 0)
    pair_cols = lambda h: slice((h // 2) * LANES, (h // 2 + 1) * LANES)
    qh = []
    for h in heads:
        q_pair = q_ref[0, :, pair_cols(h)]
        qh.append(jnp.where(low_lanes if h % 2 == 0 else jnp.logical_not(low_lanes), q_pair, jnp.zeros_like(q_pair)))

    def tiles(kb, mask):
        rows = pl.ds(pl.multiple_of(kb * t, t), t)
        zs = [_dot_nt(qh[h], k_ref[0, rows, pair_cols(h)]) + bias_ref[h] for h in heads]
        a, carries = _stick_break_tiles(zs, [carry_ref[h] for h in heads], tri, [mask] * SB_HEADS)
        for h in heads:
            carry_ref[h] = carries[h]
            acc_ref[h] += _dot(a[h], v_ref[0, rows, pair_cols(h)])

    acc_ref[...] = jnp.zeros(acc_ref.shape, F32)
    carry_ref[...] = jnp.zeros(carry_ref.shape, F32)
    tiles(qi, causal)

    def earlier(j, _):
        tiles(qi - 1 - j, None)
        return 0

    lax.fori_loop(0, qi, earlier, 0)
    for h in range(0, SB_HEADS, 2):
        o_ref[0, :, pair_cols(h)] = jnp.where(low_lanes, acc_ref[h], acc_ref[h + 1])


def _sb_prompt(q, k, v, bias):
    b, length, width = q.shape
    assert width == SB_HEADS * SB_HEAD_DIM and 2 * SB_HEAD_DIM == LANES
    t = _row_tile(length, 256)
    q_blk = lambda i, j: (i, j, 0)
    kv_blk = lambda i, j: (i, 0, 0)
    return pl.pallas_call(
        functools.partial(_sb_prompt_kernel, t=t),
        grid=(b, length // t),
        in_specs=[pl.BlockSpec(memory_space=pltpu.SMEM), pl.BlockSpec((1, t, width), q_blk),
                  pl.BlockSpec((1, length, width), kv_blk), pl.BlockSpec((1, length, width), kv_blk)],
        out_specs=pl.BlockSpec((1, t, width), q_blk),
        scratch_shapes=[pltpu.VMEM((SB_HEADS, t, LANES), F32), pltpu.VMEM((SB_HEADS, t, 1), F32)],
        out_shape=jax.ShapeDtypeStruct((b, length, width), F32),
        compiler_params=_cparams("parallel", "arbitrary"),
        name="stick_break_prompt",
    )(bias, q, k, v)


def _sb_sample_kernel(pt_ref, q_ref, bias_ref, kn_ref, vn_ref, *refs, n_pages, n_new):
    k_refs, v_refs, o_ref = refs[:n_pages], refs[n_pages:2 * n_pages], refs[2 * n_pages]
    kpad_ref, vpad_ref = refs[2 * n_pages + 1:]
    rows = SB_HEADS * n_new
    width = SB_HEADS * SB_HEAD_DIM
    row_i = lax.broadcasted_iota(jnp.int32, (rows, width), 0)
    lane_i = lax.broadcasted_iota(jnp.int32, (rows, width), 1)
    own_head = (row_i // n_new) == (lane_i // SB_HEAD_DIM)
    qbd = jnp.where(own_head, q_ref[0], jnp.zeros_like(q_ref[0]))
    bias = bias_ref[...]
    tri = _later_key_matrix(PAGE)

    kpad_ref[...] = jnp.zeros(kpad_ref.shape, BF16)
    vpad_ref[...] = jnp.zeros(vpad_ref.shape, BF16)
    kpad_ref[0:kn_ref.shape[1], :] = kn_ref[0].astype(BF16)
    vpad_ref[0:vn_ref.shape[1], :] = vn_ref[0].astype(BF16)
    qpos = lax.broadcasted_iota(jnp.int32, (rows, PAGE), 0) % n_new
    kpos = lax.broadcasted_iota(jnp.int32, (rows, PAGE), 1)

    pages = list(reversed(range(n_pages)))
    zs = [_dot_nt(qbd, kpad_ref[...]) + bias] + [_dot(qbd, k_refs[p][0].astype(BF16)) + bias for p in pages]
    a, _ = _stick_break_tiles(zs, [jnp.zeros((rows, 1), F32)], tri, [kpos < qpos] + [None] * n_pages, chained=True)
    out = _dot(a[0], vpad_ref[...])
    for i, p in enumerate(pages):
        out = out + _dot_nt(a[i + 1], v_refs[p][0].astype(BF16))
    out = jnp.where(own_head, out, 0.0)
    acc = out[0:n_new]
    for h in range(1, SB_HEADS):
        acc = acc + out[h * n_new:(h + 1) * n_new]
    o_ref[0] = acc


def _sb_sample(q, k_new, v_new, bias, cache_k, cache_v, page_table):
    b, n_new, width = q.shape
    n_pages = page_table.shape[1]
    n_phys = cache_k.shape[0]
    assert cache_k.shape[1:] == (PAGE, SB_HEADS, SB_HEAD_DIM) and n_new <= 8
    ck = jnp.transpose(cache_k, (0, 2, 3, 1)).reshape(n_phys, width, PAGE)
    cv = jnp.transpose(cache_v, (0, 2, 3, 1)).reshape(n_phys, width, PAGE)
    rows = SB_HEADS * n_new
    pad = ((0, 0), (0, 8 - n_new), (0, 0))
    per_b = lambda i, pt: (i, 0, 0)
    page_specs = [pl.BlockSpec((1, width, PAGE), functools.partial(lambda i, pt, p: (pt[i, p], 0, 0), p=p))
                  for p in range(n_pages)]
    return pl.pallas_call(
        functools.partial(_sb_sample_kernel, n_pages=n_pages, n_new=n_new),
        grid_spec=pltpu.PrefetchScalarGridSpec(
            num_scalar_prefetch=1,
            grid=(b,),
            in_specs=[pl.BlockSpec((1, rows, width), per_b), pl.BlockSpec((rows, 1), lambda i, pt: (0, 0)),
                      pl.BlockSpec((1, 8, width), per_b), pl.BlockSpec((1, 8, width), per_b)] + page_specs + page_specs,
            out_specs=pl.BlockSpec((1, n_new, width), per_b),
            scratch_shapes=[pltpu.VMEM((PAGE, width), BF16), pltpu.VMEM((PAGE, width), BF16)]),
        out_shape=jax.ShapeDtypeStruct((b, n_new, width), F32),
        compiler_params=_cparams("parallel"),
        name="stick_break_sample",
    )(page_table, jnp.tile(q, (1, SB_HEADS, 1)), jnp.repeat(bias, n_new)[:, None],
      jnp.pad(k_new, pad), jnp.pad(v_new, pad), *([ck] * n_pages), *([cv] * n_pages))


def _even_out_kernel(h_ref, py_ref, sb_ref, gate_ref, w_ref, y_ref):
    pw = py_ref.shape[-1]
    m_pool = (py_ref[...] * _silu(gate_ref[:, 0:pw])).astype(BF16)
    m_sb = (sb_ref[...] * _silu(gate_ref[:, pw:2 * pw])).astype(BF16)
    y_ref[...] = h_ref[...] + (_dot(m_pool, w_ref[0:pw, :]) + _dot(m_sb, w_ref[pw:2 * pw, :]))


def _even_out(h2d, pool_y, sb, gate, w_out):
    m, d = h2d.shape
    pw = pool_y.shape[-1]
    tm = _row_tile(m, 512)
    row = lambda i: (i, 0)
    return pl.pallas_call(
        _even_out_kernel,
        grid=(m // tm,),
        in_specs=[pl.BlockSpec((tm, d), row), pl.BlockSpec((tm, pw), row), pl.BlockSpec((tm, pw), row),
                  pl.BlockSpec((tm, 2 * pw), row), pl.BlockSpec((2 * pw, d), lambda i: (0, 0))],
        out_specs=pl.BlockSpec((tm, d), row),
        out_shape=jax.ShapeDtypeStruct((m, d), F32),
        compiler_params=_cparams("parallel"),
        name="even_out_proj",
    )(h2d, pool_y, sb, gate, w_out.astype(BF16))


def _even_layer(h, start, pool_buf, past, norm_g, w_in, w_pool, pool_scale, q_g, k_g, sb_bias, w_out):
    b, length, d = h.shape
    h2d = h.reshape(b * length, d)
    u, q, k, v, kb, vb, gate = _even_proj(h2d, norm_g, w_in, q_g, k_g)
    seq = lambda a: a.reshape(b, length, a.shape[-1])
    pool_y, new_buf = _pool_mix(seq(u), pool_buf, start, w_pool, pool_scale)
    if past is None:
        sb = _sb_prompt(seq(q), seq(kb), seq(vb), sb_bias)
    else:
        sb = _sb_sample(seq(q), seq(k), seq(v), sb_bias, *past)
    y = _even_out(h2d, pool_y.reshape(b * length, -1), sb.reshape(b * length, -1), gate, w_out)
    heads = lambda a: a.reshape(b, length, SB_HEADS, SB_HEAD_DIM)
    return y.reshape(b, length, d), new_buf, heads(k), heads(v)


def _odd_proj_kernel(x_ref, g_ref, w_ref, wba_ref, qkv_ref, z_ref, ba_ref, *, chunk):
    xn = _rms_rows(x_ref[...], g_ref[...]).astype(BF16)
    n_qkv = qkv_ref.shape[-1]
    for c0 in range(0, n_qkv, chunk):
        qkv_ref[:, c0:c0 + chunk] = _dot(xn, w_ref[:, c0:c0 + chunk])
    for c0 in range(0, z_ref.shape[-1], chunk):
        z_ref[:, c0:c0 + chunk] = _dot(xn, w_ref[:, n_qkv + c0:n_qkv + c0 + chunk])
    ba_ref[...] = _dot(xn, wba_ref[...])


def _odd_proj(x2d, norm_g, w_in):
    m, d = x2d.shape
    n_qkv, n_z = 3 * DN_HEADS * DN_DIM, DN_HEADS * DN_DIM
    assert w_in.shape == (d, n_qkv + n_z + 2 * DN_HEADS)
    tm = _row_tile(m, 256)
    w_main = w_in[:, :n_qkv + n_z].astype(BF16)
    w_ba = jnp.pad(w_in[:, n_qkv + n_z:], ((0, 0), (0, LANES - 2 * DN_HEADS))).astype(BF16)
    row = lambda i: (i, 0)
    fixed = lambda i: (0, 0)
    return pl.pallas_call(
        functools.partial(_odd_proj_kernel, chunk=512),
        grid=(m // tm,),
        in_specs=[pl.BlockSpec((tm, d), row), pl.BlockSpec((1, d), fixed), pl.BlockSpec(w_main.shape, fixed),
                  pl.BlockSpec(w_ba.shape, fixed)],
        out_specs=[pl.BlockSpec((tm, n_qkv), row), pl.BlockSpec((tm, n_z), row), pl.BlockSpec((tm, LANES), row)],
        out_shape=[jax.ShapeDtypeStruct((m, n_qkv), F32), jax.ShapeDtypeStruct((m, n_z), F32),
                   jax.ShapeDtypeStruct((m, LANES), F32)],
        compiler_params=_cparams("parallel"),
        name="odd_in_proj",
    )(x2d, norm_g[None, :], w_main, w_ba)


def _conv_kernel(x_ref, buf_ref, cw_ref, ba_ref, ab_ref, q_ref, k_ref, v_ref, gb_ref, nb_ref, ext_ref, *, tl):
    l = pl.program_id(1)
    lo = CONV_PAD - CONV_HIST
    width = ext_ref.shape[-1]

    @pl.when(l == 0)
    def _():
        ext_ref[0:lo, :] = jnp.zeros((lo, width), F32)
        ext_ref[lo:CONV_PAD, :] = buf_ref[0]

    @pl.when(l > 0)
    def _():
        ext_ref[lo:CONV_PAD, :] = ext_ref[tl + lo:tl + CONV_PAD, :]

    ext_ref[CONV_PAD:CONV_PAD + tl, :] = x_ref[0]
    n_qk = DN_HEADS * DN_DIM
    for c in range(width // LANES):
        cols = slice(c * LANES, (c + 1) * LANES)
        conv = ext_ref[lo:lo + tl, cols] * cw_ref[0:1, cols]
        for j in range(1, CONV_WIDTH):
            conv = conv + ext_ref[lo + j:lo + j + tl, cols] * cw_ref[j:j + 1, cols]
        y = _silu(conv)
        oc = slice((c % DN_HEADS) * LANES, (c % DN_HEADS + 1) * LANES)
        if c * LANES < 2 * n_qk:
            y = y * lax.rsqrt(jnp.sum(y * y, axis=-1, keepdims=True) + NORM_EPS)
            if c * LANES < n_qk:
                q_ref[0, :, oc] = y * (DN_DIM ** -0.5)
            else:
                k_ref[0, :, oc] = y
        else:
            v_ref[0, :, oc] = y

    ba = ba_ref[0]
    x = ba + ab_ref[1:2, :]
    g = -jnp.exp(ab_ref[0:1, :]) * (jnp.maximum(x, 0.0) + _softplus_neg_abs(x))
    is_beta = lax.broadcasted_iota(jnp.int32, ba.shape, 1) < DN_HEADS
    gb_ref[0] = jnp.where(is_beta, _sigmoid(ba), g)

    @pl.when(l == pl.num_programs(1) - 1)
    def _():
        nb_ref[0] = ext_ref[tl + lo:tl + CONV_PAD, :]


def _conv_gates(qkv, conv_buf, conv_w, ba, a_log, dt_bias):
    b, length, width = qkv.shape
    n = DN_HEADS * DN_DIM
    assert width == 3 * n and conv_buf.shape == (b, CONV_HIST, width)
    tl = _row_tile(length, 256)
    assert tl == length or tl >= CONV_PAD
    park = lambda vec: jnp.pad(vec, (DN_HEADS, LANES - 2 * DN_HEADS))
    ab = jnp.stack([park(a_log), park(dt_bias)])
    seq = lambda i, l: (i, l, 0)
    per_b = lambda i, l: (i, 0, 0)
    fixed = lambda i, l: (0, 0)
    return pl.pallas_call(
        functools.partial(_conv_kernel, tl=tl),
        grid=(b, length // tl),
        in_specs=[pl.BlockSpec((1, tl, width), seq), pl.BlockSpec((1, CONV_HIST, width), per_b),
                  pl.BlockSpec((CONV_WIDTH, width), fixed), pl.BlockSpec((1, tl, LANES), seq),
                  pl.BlockSpec((2, LANES), fixed)],
        out_specs=[pl.BlockSpec((1, tl, n), seq)] * 3 + [pl.BlockSpec((1, tl, LANES), seq),
                                                          pl.BlockSpec((1, CONV_HIST, width), per_b)],
        out_shape=[jax.ShapeDtypeStruct((b, length, n), F32)] * 3
        + [jax.ShapeDtypeStruct((b, length, LANES), F32), jax.ShapeDtypeStruct((b, CONV_HIST, width), F32)],
        scratch_shapes=[pltpu.VMEM((CONV_PAD + tl, width), F32)],
        compiler_params=_cparams("parallel", "arbitrary"),
        name="conv_norm_gates",
    )(qkv, conv_buf, conv_w, ba, ab)


def _unit_lower_inverses(lmats, c):
    eye = (lax.broadcasted_iota(jnp.int32, (c, c), 0) == lax.broadcasted_iota(jnp.int32, (c, c), 1)).astype(F32)
    powers = [-lm for lm in lmats]
    invs = [eye + p for p in powers]
    span = 2
    while span < c:
        powers = [_dot_f32(p, p) for p in powers]
        invs = [i + _dot_f32(i, p) for i, p in zip(invs, powers)]
        span *= 2
    return invs


def _head_cols(h):
    return slice(h * DN_DIM, (h + 1) * DN_DIM)


def _delta_local_kernel(q_ref, k_ref, v_ref, gb_ref, u_ref, wq_ref, kd_ref, att_ref, eg_ref, *, c, n_chunks):
    row = lax.broadcasted_iota(jnp.int32, (c, c), 0)
    col = lax.broadcasted_iota(jnp.int32, (c, c), 1)
    incl = row >= col
    strict = row > col
    prefix = incl.astype(BF16)
    eye_lanes = (lax.broadcasted_iota(jnp.int32, (LANES, LANES), 0)
                 == lax.broadcasted_iota(jnp.int32, (LANES, LANES), 1)).astype(BF16)
    heads = range(DN_HEADS)

    def chunk(ci, _):
        rows = pl.ds(pl.multiple_of(ci * c, c), c)
        gb = gb_ref[rows, :]
        gcs = _dot_exact_lhs(prefix, gb)
        h3, m3, l3 = _split3(gcs)
        gcs_t = _dot_nt(eye_lanes, h3) + _dot_nt(eye_lanes, m3) + _dot_nt(eye_lanes, l3)
        q = [q_ref[rows, _head_cols(h)] for h in heads]
        k = [k_ref[rows, _head_cols(h)] for h in heads]
        v = [v_ref[rows, _head_cols(h)] for h in heads]
        beta = [gb[:, h:h + 1] for h in heads]
        gc = [gcs[:, DN_HEADS + h:DN_HEADS + h + 1] for h in heads]
        g_last = [gcs[c - 1:c, DN_HEADS + h:DN_HEADS + h + 1] for h in heads]
        decay = [jnp.where(incl, jnp.exp(jnp.where(incl, gc[h] - gcs_t[DN_HEADS + h:DN_HEADS + h + 1, :], 0.0)), 0.0)
                 for h in heads]
        kb = [k[h] * beta[h] for h in heads]
        k16 = [k[h].astype(BF16) for h in heads]
        lmat = [jnp.where(strict, _dot_nt(kb[h].astype(BF16), k16[h]) * decay[h], 0.0) for h in heads]
        tmat = [t.astype(BF16) for t in _unit_lower_inverses(lmat, c)]
        egc = [jnp.exp(gc[h]) for h in heads]
        for h in heads:
            cols = _head_cols(h)
            u_ref[rows, cols] = _dot(tmat[h], (v[h] * beta[h]).astype(BF16))
            w = _dot(tmat[h], (kb[h] * egc[h]).astype(BF16))
            wq_ref[ci, :, cols] = jnp.concatenate([w, q[h] * egc[h]], axis=0).astype(BF16)
            kd_ref[rows, cols] = (k[h] * jnp.exp(g_last[h] - gc[h])).astype(BF16)
            att = _dot_nt(q[h].astype(BF16), k16[h]) * decay[h]
            if c < LANES:
                att = jnp.concatenate([att, jnp.zeros((c, LANES - c), F32)], axis=1)
            att_ref[rows, cols] = att.astype(BF16)
            eg_ref[ci, h:h + 1, :] = jnp.broadcast_to(jnp.exp(g_last[h]), (1, LANES))
        return 0

    lax.fori_loop(0, n_chunks, chunk, 0)


def _delta_scan_kernel(u_ref, wq_ref, kd_ref, att_ref, eg_ref, s0_ref, o_ref, s_out_ref, s_ref, *, c, n_chunks, bt):
    l = pl.program_id(1)

    @pl.when(l == 0)
    def _():
        s_ref[...] = s0_ref[...]

    eye_lanes = (lax.broadcasted_iota(jnp.int32, (LANES, LANES), 0)
                 == lax.broadcasted_iota(jnp.int32, (LANES, LANES), 1)).astype(BF16)
    units = [(i, h) for i in range(bt) for h in range(DN_HEADS)]

    def chunk(ci, _):
        rows = pl.ds(pl.multiple_of(ci * c, c), c)
        s = [s_ref[i, h] for i, h in units]
        s16 = [x.astype(BF16) for x in s]
        r = [_dot(wq_ref[i, ci, :, _head_cols(h)], s16[n]) for n, (i, h) in enumerate(units)]
        v16 = [(u_ref[i, rows, _head_cols(h)] - r[n][0:c]).astype(BF16) for n, (i, h) in enumerate(units)]
        kd_t = [_dot_nt(eye_lanes, kd_ref[i, rows, _head_cols(h)]).astype(BF16) for i, h in units]
        for n, (i, h) in enumerate(units):
            cols = _head_cols(h)
            att = att_ref[i, rows, h * DN_DIM:h * DN_DIM + c]
            o_ref[i, rows, cols] = r[n][c:2 * c] + _dot(att, v16[n])
            s_ref[i, h] = s[n] * eg_ref[i, ci, h:h + 1, :] + _dot(kd_t[n], v16[n])
        return 0

    lax.fori_loop(0, n_chunks, chunk, 0)

    @pl.when(l == pl.num_programs(1) - 1)
    def _():
        s_out_ref[...] = s_ref[...]


def _gated_delta(q, k, v, gb, state, c):
    b, length, n = q.shape
    assert n == DN_HEADS * DN_DIM and state.shape == (b, DN_HEADS, DN_DIM, DN_DIM) and length % c == 0
    assert c % 8 == 0 and c <= LANES
    tokens = b * length
    flat = lambda a: a.reshape(tokens, a.shape[-1])
    tn = _row_tile(tokens, max(c, 256))
    row = lambda i: (i, 0)
    per_chunk = lambda i: (i, 0, 0)
    u, wq, kd, att, eg = pl.pallas_call(
        functools.partial(_delta_local_kernel, c=c, n_chunks=tn // c),
        grid=(tokens // tn,),
        in_specs=[pl.BlockSpec((tn, n), row)] * 3 + [pl.BlockSpec((tn, LANES), row)],
        out_specs=[pl.BlockSpec((tn, n), row), pl.BlockSpec((tn // c, 2 * c, n), per_chunk),
                   pl.BlockSpec((tn, n), row), pl.BlockSpec((tn, n), row),
                   pl.BlockSpec((tn // c, DN_HEADS, LANES), per_chunk)],
        out_shape=[jax.ShapeDtypeStruct((tokens, n), F32), jax.ShapeDtypeStruct((tokens // c, 2 * c, n), BF16),
                   jax.ShapeDtypeStruct((tokens, n), BF16), jax.ShapeDtypeStruct((tokens, n), BF16),
                   jax.ShapeDtypeStruct((tokens // c, DN_HEADS, LANES), F32)],
        compiler_params=_cparams("parallel"),
        name="delta_local",
    )(flat(q), flat(k), flat(v), flat(gb))

    n_seq_chunks = length // c
    tl = c * min(n_seq_chunks, 4)
    bt = 2 if n_seq_chunks > 1 else 4
    assert length % tl == 0 and b % bt == 0
    seq = lambda i, l: (i, l, 0)
    seq_chunks = lambda i, l: (i, l, 0, 0)
    per_b = lambda i, l: (i, 0, 0, 0)
    st_block = (bt,) + state.shape[1:]
    return pl.pallas_call(
        functools.partial(_delta_scan_kernel, c=c, n_chunks=tl // c, bt=bt),
        grid=(b // bt, length // tl),
        in_specs=[pl.BlockSpec((bt, tl, n), seq), pl.BlockSpec((bt, tl // c, 2 * c, n), seq_chunks),
                  pl.BlockSpec((bt, tl, n), seq), pl.BlockSpec((bt, tl, n), seq),
                  pl.BlockSpec((bt, tl // c, DN_HEADS, LANES), seq_chunks), pl.BlockSpec(st_block, per_b)],
        out_specs=[pl.BlockSpec((bt, tl, n), seq), pl.BlockSpec(st_block, per_b)],
        out_shape=[jax.ShapeDtypeStruct((b, length, n), F32), jax.ShapeDtypeStruct(state.shape, F32)],
        scratch_shapes=[pltpu.VMEM(st_block, F32)],
        compiler_params=_cparams("parallel", "arbitrary"),
        name="delta_scan",
    )(u.reshape(b, length, n), wq.reshape(b, n_seq_chunks, 2 * c, n), kd.reshape(b, length, n),
      att.reshape(b, length, n), eg.reshape(b, n_seq_chunks, DN_HEADS, LANES), state)


def _odd_out_kernel(h_ref, o_ref, z_ref, og_ref, w_ref, y_ref):
    acc = h_ref[...]
    for hd in range(DN_HEADS):
        cols = slice(hd * DN_DIM, (hd + 1) * DN_DIM)
        o = _rms_rows(o_ref[:, cols], og_ref[...]) * _silu(z_ref[:, cols])
        acc = acc + _dot(o.astype(BF16), w_ref[cols, :])
    y_ref[...] = acc


def _odd_out(h2d, o2d, z, o_g, w_out):
    m, d = h2d.shape
    n = o2d.shape[-1]
    tm = _row_tile(m, 512)
    row = lambda i: (i, 0)
    fixed = lambda i: (0, 0)
    return pl.pallas_call(
        _odd_out_kernel,
        grid=(m // tm,),
        in_specs=[pl.BlockSpec((tm, d), row), pl.BlockSpec((tm, n), row), pl.BlockSpec((tm, n), row),
                  pl.BlockSpec((1, DN_DIM), fixed), pl.BlockSpec((n, d), fixed)],
        out_specs=pl.BlockSpec((tm, d), row),
        out_shape=jax.ShapeDtypeStruct((m, d), F32),
        compiler_params=_cparams("parallel"),
        name="odd_out_proj",
    )(h2d, o2d, z, o_g[None, :], w_out.astype(BF16))


def _odd_layer(h, conv_buf, state, norm_g, w_in, conv_w, a_log, dt_bias, o_g, w_out):
    b, length, d = h.shape
    h2d = h.reshape(b * length, d)
    qkv, z, ba = _odd_proj(h2d, norm_g, w_in)
    seq = lambda a: a.reshape(b, length, a.shape[-1])
    q, k, v, gb, new_buf = _conv_gates(seq(qkv), conv_buf, conv_w, seq(ba), a_log, dt_bias)
    c = DN_CHUNK if length % DN_CHUNK == 0 else length
    c_pad = -(-c // 8) * 8
    if c_pad != c:
        pad = lambda a: jnp.pad(a, ((0, 0), (0, c_pad - c), (0, 0)))
        o, new_state = _gated_delta(pad(q), pad(k), pad(v), pad(gb), state, c_pad)
        o = o[:, :length]
    else:
        o, new_state = _gated_delta(q, k, v, gb, state, c)
    y = _odd_out(h2d, o.reshape(b * length, -1), z, o_g, w_out)
    return y.reshape(b, length, d), new_buf, new_state


def kernel(x_prompt, x_sample, cache_pool, cache_k, cache_v, state_conv, state_delta, page_table, norm_even, w_in_even, w_pool, pool_scale, q_norm, k_norm, sb_bias, w_out_even, norm_odd, w_in_odd, conv_w, a_log, dt_bias, o_norm, w_out_odd):
    bp = x_prompt.shape[0]
    past_len = page_table.shape[1] * cache_k.shape[2]
    depth = norm_even.shape[0] + norm_odd.shape[0]
    hp, hs = x_prompt, x_sample
    outs = {name: [] for name in ("pool_p", "pool_s", "k_p", "v_p", "k_s", "v_s", "conv_p", "conv_s", "st_p", "st_s")}
    for layer in range(depth):
        i = layer // 2
        if layer % 2 == 0:
            ev = (norm_even[i], w_in_even[i], w_pool[i], pool_scale[i], q_norm[i], k_norm[i], sb_bias[i], w_out_even[i])
            zero_buf = jnp.zeros((bp, POOL_HIST, cache_pool.shape[-1]), F32)
            hp, pb, kn, vn = _even_layer(hp, 0, zero_buf, None, *ev)
            outs["pool_p"].append(pb), outs["k_p"].append(kn), outs["v_p"].append(vn)
            hs, pb, kn, vn = _even_layer(hs, past_len, cache_pool[i], (cache_k[i], cache_v[i], page_table), *ev)
            outs["pool_s"].append(pb), outs["k_s"].append(kn), outs["v_s"].append(vn)
        else:
            od = (norm_odd[i], w_in_odd[i], conv_w[i], a_log[i], dt_bias[i], o_norm[i], w_out_odd[i])
            zero_conv = jnp.zeros((bp, CONV_HIST, state_conv.shape[-1]), F32)
            zero_state = jnp.zeros((bp,) + state_delta.shape[2:], F32)
            hp, cb, st = _odd_layer(hp, zero_conv, zero_state, *od)
            outs["conv_p"].append(cb), outs["st_p"].append(st)
            hs, cb, st = _odd_layer(hs, state_conv[i], state_delta[i], *od)
            outs["conv_s"].append(cb), outs["st_s"].append(st)
    stack = lambda name: jnp.stack(outs[name])
    return (hp, hs, stack("pool_p"), stack("pool_s"), stack("k_p"), stack("v_p"), stack("k_s"), stack("v_s"),
            stack("conv_p"), stack("conv_s"), stack("st_p"), stack("st_s"))
```

```python
import functools
import math

import jax
import jax.numpy as jnp
from jax import lax
from jax.experimental import pallas as pl
from jax.experimental.pallas import tpu as pltpu

F32 = jnp.float32
BF16 = jnp.bfloat16

NORM_EPS = 1e-6
POOL_WINDOWS = (2, 4, 8, 16)
POOL_HIST = max(POOL_WINDOWS) - 1
POOL_PAD = 16
SB_HEADS = 8
SB_HEAD_DIM = 64
DN_HEADS = 8
DN_DIM = 128
CONV_WIDTH = 4
CONV_HIST = CONV_WIDTH - 1
CONV_PAD = 8
DN_CHUNK = 64
LANES = 128
PAGE = 128
VMEM_LIMIT_BYTES = 48 * 1024 * 1024


def _cparams(*sem):
    return pltpu.CompilerParams(dimension_semantics=sem, vmem_limit_bytes=VMEM_LIMIT_BYTES)


def _dot(a, b):
    return jnp.dot(a, b, preferred_element_type=F32)


def _dot_nt(a, b):
    return lax.dot_general(a, b, (((1,), (1,)), ((), ())), preferred_element_type=F32)


def _split2(x):
    hi = x.astype(BF16)
    return hi, (x - hi.astype(F32)).astype(BF16)


def _split3(x):
    hi = x.astype(BF16)
    r = x - hi.astype(F32)
    mid = r.astype(BF16)
    return hi, mid, (r - mid.astype(F32)).astype(BF16)


def _dot_exact_rhs(a, b_bf16):
    hi, mid, lo = _split3(a)
    return _dot(hi, b_bf16) + _dot(mid, b_bf16) + _dot(lo, b_bf16)


def _dot_exact_lhs(a_bf16, b):
    hi, mid, lo = _split3(b)
    return _dot(a_bf16, hi) + _dot(a_bf16, mid) + _dot(a_bf16, lo)


def _dot_f32(a, b):
    ah, al = _split2(a)
    bh, bl = _split2(b)
    return _dot(ah, bh) + (_dot(ah, bl) + _dot(al, bh))


def _sigmoid(x):
    return 1.0 / (1.0 + jnp.exp(-x))


def _silu(x):
    return x * _sigmoid(x)


def _softplus_neg_abs(x):
    return jnp.log1p(jnp.exp(-jnp.abs(x)))


def _rms_rows(x, g):
    return x * lax.rsqrt(jnp.mean(x * x, axis=-1, keepdims=True) + NORM_EPS) * g


def _row_tile(m, want):
    t = min(m, want)
    assert m % t == 0, (m, t)
    return t


def _even_proj_kernel(x_ref, g_ref, w_ref, seg_ref, qg_ref, kg_ref,
                      u_ref, q_ref, k_ref, v_ref, kb_ref, vb_ref, gate_ref):
    xn = _rms_rows(x_ref[...], g_ref[...]).astype(BF16)
    pw = u_ref.shape[-1]

    def proj(c0, c1):
        return _dot(xn, w_ref[:, c0:c1])

    def head_norm(y, gain):
        ms = _dot((y * y).astype(BF16), seg_ref[...]) * (1.0 / SB_HEAD_DIM)
        return y * lax.rsqrt(ms + NORM_EPS) * gain

    u_ref[...] = proj(0, pw)
    q = head_norm(proj(pw, 2 * pw), qg_ref[...])
    q_ref[...] = (q * (SB_HEAD_DIM ** -0.5)).astype(BF16)
    k = head_norm(proj(2 * pw, 3 * pw), kg_ref[...])
    k_ref[...] = k
    kb_ref[...] = k.astype(BF16)
    v = proj(3 * pw, 4 * pw)
    v_ref[...] = v
    vb_ref[...] = v.astype(BF16)
    gate_ref[...] = proj(4 * pw, 6 * pw)


def _even_proj(x2d, norm_g, w_in, q_g, k_g):
    m, d = x2d.shape
    pw = SB_HEADS * SB_HEAD_DIM
    assert w_in.shape == (d, 6 * pw)
    tm = _row_tile(m, 512)
    lane_head = jnp.arange(pw) // SB_HEAD_DIM
    seg = (lane_head[:, None] == lane_head[None, :]).astype(BF16)
    row = lambda i: (i, 0)
    fixed = lambda i: (0, 0)
    outs = [((m, pw), F32), ((m, pw), BF16), ((m, pw), F32), ((m, pw), F32), ((m, pw), BF16), ((m, pw), BF16),
            ((m, 2 * pw), F32)]
    return pl.pallas_call(
        _even_proj_kernel,
        grid=(m // tm,),
        in_specs=[pl.BlockSpec((tm, d), row), pl.BlockSpec((1, d), fixed), pl.BlockSpec((d, 6 * pw), fixed),
                  pl.BlockSpec((pw, pw), fixed), pl.BlockSpec((1, pw), fixed), pl.BlockSpec((1, pw), fixed)],
        out_specs=[pl.BlockSpec((tm, s[-1]), row) for s, _ in outs],
        out_shape=[jax.ShapeDtypeStruct(s, t) for s, t in outs],
        compiler_params=_cparams("parallel"),
        name="even_in_proj",
    )(x2d, norm_g[None, :], w_in.astype(BF16), seg,
      jnp.tile(q_g, SB_HEADS)[None, :], jnp.tile(k_g, SB_HEADS)[None, :])


def _seqs_per_step(b, length):
    bt = 8 if length <= 8 else 1
    assert b % bt == 0
    return bt


def _pool_kernel(u_ref, buf_ref, wp_ref, scale_ref, y_ref, nb_ref, ext_ref, *, start, tl, bt):
    l = pl.program_id(1)
    lo = POOL_PAD - POOL_HIST
    pos = start + l * tl + lax.broadcasted_iota(jnp.int32, (tl, 1), 0)
    for s in range(bt):
        ext = ext_ref.at[s]

        @pl.when(l == 0)
        def _():
            ext[0:lo, :] = jnp.zeros((lo, ext.shape[-1]), F32)
            ext[lo:POOL_PAD, :] = buf_ref[s]

        @pl.when(l > 0)
        def _():
            ext[lo:POOL_PAD, :] = ext[tl + lo:tl + POOL_PAD, :]

        ext[POOL_PAD:POOL_PAD + tl, :] = u_ref[s]
        for gi, w in enumerate(POOL_WINDOWS):
            cols = slice(gi * LANES, (gi + 1) * LANES)
            cur = ext[POOL_PAD:POOL_PAD + tl, cols]
            win = cur
            for i in range(1, w):
                win = win + ext[POOL_PAD - i:POOL_PAD - i + tl, cols]
            cnt = jnp.minimum(pos + 1, w).astype(F32)
            d = win / cnt - cur
            y_ref[s, :, cols] = _dot(d.astype(BF16), wp_ref[gi]) * scale_ref[:, cols]

        @pl.when(l == pl.num_programs(1) - 1)
        def _():
            nb_ref[s] = ext[tl + lo:tl + POOL_PAD, :]


def _pool_mix(u, buf, start, w_pool, scale):
    b, length, p = u.shape
    assert p == len(POOL_WINDOWS) * LANES and buf.shape == (b, POOL_HIST, p)
    tl = _row_tile(length, 512)
    assert tl == length or tl >= POOL_PAD
    bt = _seqs_per_step(b, length)
    seq = lambda i, l: (i, l, 0)
    per_b = lambda i, l: (i, 0, 0)
    return pl.pallas_call(
        functools.partial(_pool_kernel, start=start, tl=tl, bt=bt),
        grid=(b // bt, length // tl),
        in_specs=[pl.BlockSpec((bt, tl, p), seq), pl.BlockSpec((bt, POOL_HIST, p), per_b),
                  pl.BlockSpec(w_pool.shape, lambda i, l: (0, 0, 0)), pl.BlockSpec((1, p), lambda i, l: (0, 0))],
        out_specs=[pl.BlockSpec((bt, tl, p), seq), pl.BlockSpec((bt, POOL_HIST, p), per_b)],
        out_shape=[jax.ShapeDtypeStruct((b, length, p), F32), jax.ShapeDtypeStruct((b, POOL_HIST, p), F32)],
        scratch_shapes=[pltpu.VMEM((bt, POOL_PAD + tl, p), F32)],
        compiler_params=_cparams("parallel", "arbitrary"),
        name="pool_mix",
    )(u, buf, w_pool.astype(BF16), scale[None, :])


def _stick_break_tiles(zs, carries, tri, masks, chained=False):
    n = range(len(zs))
    neg = [jnp.minimum(z, 0.0) for z in zs]
    neg_pos = [neg[i] - zs[i] for i in n]
    t = [jnp.log(1.0 + jnp.exp(neg[i] + neg_pos[i])) for i in n]
    log_beta = [neg[i] - t[i] for i in n]
    log_keep = [neg_pos[i] - t[i] for i in n]
    log_keep = [lk if m is None else jnp.where(m, lk, 0.0) for lk, m in zip(log_keep, masks)]
    totals = [jnp.sum(lk, axis=-1, keepdims=True) for lk in log_keep]
    if chained:
        carry_in = [carries[0]]
        for i in n:
            carry_in.append(carry_in[i] + totals[i])
        new_carries = carry_in[-1:]
    else:
        carry_in = carries
        new_carries = [carries[i] + totals[i] for i in n]
    tri2 = jnp.concatenate([tri, tri], axis=0)
    suffix = [_dot(jnp.concatenate(_split2(lk), axis=1), tri2) + carry_in[i] for i, lk in enumerate(log_keep)]
    a = [jnp.exp(log_beta[i] + suffix[i]) for i in n]
    a = [x if m is None else jnp.where(m, x, 0.0) for x, m in zip(a, masks)]
    return [x.astype(BF16) for x in a], new_carries


def _later_key_matrix(n):
    return (lax.broadcasted_iota(jnp.int32, (n, n), 0) > lax.broadcasted_iota(jnp.int32, (n, n), 1)).astype(BF16)


def _sb_prompt_kernel(bias_ref, q_ref, k_ref, v_ref, o_ref, acc_ref, carry_ref, *, t):
    qi = pl.program_id(1)
    heads = range(SB_HEADS)
    low_lanes = lax.broadcasted_iota(jnp.int32, (t, LANES), 1) < SB_HEAD_DIM
    tri = _later_key_matrix(t)
    causal = lax.broadcasted_iota(jnp.int32, (t, t), 1) < lax.broadcasted_iota(jnp.int32, (t, t), 0)
    pair_cols = lambda h: slice((h // 2) * LANES, (h // 2 + 1) * LANES)
    qh = []
    for h in heads:
        q_pair = q_ref[0, :, pair_cols(h)]
        qh.append(jnp.where(low_lanes if h % 2 == 0 else jnp.logical_not(low_lanes), q_pair, jnp.zeros_like(q_pair)))

    def tiles(kb, mask):
        rows = pl.ds(pl.multiple_of(kb * t, t), t)
        zs = [_dot_nt(qh[h], k_ref[0, rows, pair_cols(h)]) + bias_ref[h] for h in heads]
        a, carries = _stick_break_tiles(zs, [carry_ref[h] for h in heads], tri, [mask] * SB_HEADS)
        for h in heads:
            carry_ref[h] = carries[h]
            acc_ref[h] += _dot(a[h], v_ref[0, rows, pair_cols(h)])

    acc_ref[...] = jnp.zeros(acc_ref.shape, F32)
    carry_ref[...] = jnp.zeros(carry_ref.shape, F32)
    tiles(qi, causal)

    def earlier(j, _):
        tiles(qi - 1 - j, None)
        return 0

    lax.fori_loop(0, qi, earlier, 0)
    for h in range(0, SB_HEADS, 2):
        o_ref[0, :, pair_cols(h)] = jnp.where(low_lanes, acc_ref[h], acc_ref[h + 1])


def _sb_prompt(q, k, v, bias):
    b, length, width = q.shape
    assert width == SB_HEADS * SB_HEAD_DIM and 2 * SB_HEAD_DIM == LANES
    t = _row_tile(length, 256)
    q_blk = lambda i, j: (i, j, 0)
    kv_blk = lambda i, j: (i, 0, 0)
    return pl.pallas_call(
        functools.partial(_sb_prompt_kernel, t=t),
        grid=(b, length // t),
        in_specs=[pl.BlockSpec(memory_space=pltpu.SMEM), pl.BlockSpec((1, t, width), q_blk),
                  pl.BlockSpec((1, length, width), kv_blk), pl.BlockSpec((1, length, width), kv_blk)],
        out_specs=pl.BlockSpec((1, t, width), q_blk),
        scratch_shapes=[pltpu.VMEM((SB_HEADS, t, LANES), F32), pltpu.VMEM((SB_HEADS, t, 1), F32)],
        out_shape=jax.ShapeDtypeStruct((b, length, width), F32),
        compiler_params=_cparams("parallel", "arbitrary"),
        name="stick_break_prompt",
    )(bias, q, k, v)


def _sb_sample_kernel(pt_ref, q_ref, bias_ref, kn_ref, vn_ref, *refs, n_pages, n_new):
    k_refs, v_refs, o_ref = refs[:n_pages], refs[n_pages:2 * n_pages], refs[2 * n_pages]
    kpad_ref, vpad_ref = refs[2 * n_pages + 1:]
    rows = SB_HEADS * n_new
    width = SB_HEADS * SB_HEAD_DIM
    row_i = lax.broadcasted_iota(jnp.int32, (rows, width), 0)
    lane_i = lax.broadcasted_iota(jnp.int32, (rows, width), 1)
    own_head = (row_i // n_new) == (lane_i // SB_HEAD_DIM)
    qbd = jnp.where(own_head, q_ref[0], jnp.zeros_like(q_ref[0]))
    bias = bias_ref[...]
    tri = _later_key_matrix(PAGE)

    kpad_ref[...] = jnp.zeros(kpad_ref.shape, BF16)
    vpad_ref[...] = jnp.zeros(vpad_ref.shape, BF16)
    kpad_ref[0:kn_ref.shape[1], :] = kn_ref[0].astype(BF16)
    vpad_ref[0:vn_ref.shape[1], :] = vn_ref[0].astype(BF16)
    qpos = lax.broadcasted_iota(jnp.int32, (rows, PAGE), 0) % n_new
    kpos = lax.broadcasted_iota(jnp.int32, (rows, PAGE), 1)

    pages = list(reversed(range(n_pages)))
    zs = [_dot_nt(qbd, kpad_ref[...]) + bias] + [_dot(qbd, k_refs[p][0].astype(BF16)) + bias for p in pages]
    a, _ = _stick_break_tiles(zs, [jnp.zeros((rows, 1), F32)], tri, [kpos < qpos] + [None] * n_pages, chained=True)
    out = _dot(a[0], vpad_ref[...])
    for i, p in enumerate(pages):
        out = out + _dot_nt(a[i + 1], v_refs[p][0].astype(BF16))
    out = jnp.where(own_head, out, 0.0)
    acc = out[0:n_new]
    for h in range(1, SB_HEADS):
        acc = acc + out[h * n_new:(h + 1) * n_new]
    o_ref[0] = acc


def _sb_sample(q, k_new, v_new, bias, cache_k, cache_v, page_table):
    b, n_new, width = q.shape
    n_pages = page_table.shape[1]
    n_phys = cache_k.shape[0]
    assert cache_k.shape[1:] == (PAGE, SB_HEADS, SB_HEAD_DIM) and n_new <= 8
    ck = jnp.transpose(cache_k, (0, 2, 3, 1)).reshape(n_phys, width, PAGE)
    cv = jnp.transpose(cache_v, (0, 2, 3, 1)).reshape(n_phys, width, PAGE)
    rows = SB_HEADS * n_new
    pad = ((0, 0), (0, 8 - n_new), (0, 0))
    per_b = lambda i, pt: (i, 0, 0)
    page_specs = [pl.BlockSpec((1, width, PAGE), functools.partial(lambda i, pt, p: (pt[i, p], 0, 0), p=p))
                  for p in range(n_pages)]
    return pl.pallas_call(
        functools.partial(_sb_sample_kernel, n_pages=n_pages, n_new=n_new),
        grid_spec=pltpu.PrefetchScalarGridSpec(
            num_scalar_prefetch=1,
            grid=(b,),
            in_specs=[pl.BlockSpec((1, rows, width), per_b), pl.BlockSpec((rows, 1), lambda i, pt: (0, 0)),
                      pl.BlockSpec((1, 8, width), per_b), pl.BlockSpec((1, 8, width), per_b)] + page_specs + page_specs,
            out_specs=pl.BlockSpec((1, n_new, width), per_b),
            scratch_shapes=[pltpu.VMEM((PAGE, width), BF16), pltpu.VMEM((PAGE, width), BF16)]),
        out_shape=jax.ShapeDtypeStruct((b, n_new, width), F32),
        compiler_params=_cparams("parallel"),
        name="stick_break_sample",
    )(page_table, jnp.tile(q, (1, SB_HEADS, 1)), jnp.repeat(bias, n_new)[:, None],
      jnp.pad(k_new, pad), jnp.pad(v_new, pad), *([ck] * n_pages), *([cv] * n_pages))


def _even_out_kernel(h_ref, py_ref, sb_ref, gate_ref, w_ref, y_ref):
    pw = py_ref.shape[-1]
    m_pool = (py_ref[...] * _silu(gate_ref[:, 0:pw])).astype(BF16)
    m_sb = (sb_ref[...] * _silu(gate_ref[:, pw:2 * pw])).astype(BF16)
    y_ref[...] = h_ref[...] + (_dot(m_pool, w_ref[0:pw, :]) + _dot(m_sb, w_ref[pw:2 * pw, :]))


def _even_out(h2d, pool_y, sb, gate, w_out):
    m, d = h2d.shape
    pw = pool_y.shape[-1]
    tm = _row_tile(m, 512)
    row = lambda i: (i, 0)
    return pl.pallas_call(
        _even_out_kernel,
        grid=(m // tm,),
        in_specs=[pl.BlockSpec((tm, d), row), pl.BlockSpec((tm, pw), row), pl.BlockSpec((tm, pw), row),
                  pl.BlockSpec((tm, 2 * pw), row), pl.BlockSpec((2 * pw, d), lambda i: (0, 0))],
        out_specs=pl.BlockSpec((tm, d), row),
        out_shape=jax.ShapeDtypeStruct((m, d), F32),
        compiler_params=_cparams("parallel"),
        name="even_out_proj",
    )(h2d, pool_y, sb, gate, w_out.astype(BF16))


def _even_layer(h, start, pool_buf, past, norm_g, w_in, w_pool, pool_scale, q_g, k_g, sb_bias, w_out):
    b, length, d = h.shape
    h2d = h.reshape(b * length, d)
    u, q, k, v, kb, vb, gate = _even_proj(h2d, norm_g, w_in, q_g, k_g)
    seq = lambda a: a.reshape(b, length, a.shape[-1])
    pool_y, new_buf = _pool_mix(seq(u), pool_buf, start, w_pool, pool_scale)
    if past is None:
        sb = _sb_prompt(seq(q), seq(kb), seq(vb), sb_bias)
    else:
        sb = _sb_sample(seq(q), seq(k), seq(v), sb_bias, *past)
    y = _even_out(h2d, pool_y.reshape(b * length, -1), sb.reshape(b * length, -1), gate, w_out)
    heads = lambda a: a.reshape(b, length, SB_HEADS, SB_HEAD_DIM)
    return y.reshape(b, length, d), new_buf, heads(k), heads(v)


def _odd_proj_kernel(x_ref, g_ref, w_ref, wba_ref, qkv_ref, z_ref, ba_ref, *, chunk):
    xn = _rms_rows(x_ref[...], g_ref[...]).astype(BF16)
    n_qkv = qkv_ref.shape[-1]
    for c0 in range(0, n_qkv, chunk):
        qkv_ref[:, c0:c0 + chunk] = _dot(xn, w_ref[:, c0:c0 + chunk])
    for c0 in range(0, z_ref.shape[-1], chunk):
        z_ref[:, c0:c0 + chunk] = _dot(xn, w_ref[:, n_qkv + c0:n_qkv + c0 + chunk])
    ba_ref[...] = _dot(xn, wba_ref[...])


def _odd_proj(x2d, norm_g, w_in):
    m, d = x2d.shape
    n_qkv, n_z = 3 * DN_HEADS * DN_DIM, DN_HEADS * DN_DIM
    assert w_in.shape == (d, n_qkv + n_z + 2 * DN_HEADS)
    tm = _row_tile(m, 256)
    w_main = w_in[:, :n_qkv + n_z].astype(BF16)
    w_ba = jnp.pad(w_in[:, n_qkv + n_z:], ((0, 0), (0, LANES - 2 * DN_HEADS))).astype(BF16)
    row = lambda i: (i, 0)
    fixed = lambda i: (0, 0)
    return pl.pallas_call(
        functools.partial(_odd_proj_kernel, chunk=512),
        grid=(m // tm,),
        in_specs=[pl.BlockSpec((tm, d), row), pl.BlockSpec((1, d), fixed), pl.BlockSpec(w_main.shape, fixed),
                  pl.BlockSpec(w_ba.shape, fixed)],
        out_specs=[pl.BlockSpec((tm, n_qkv), row), pl.BlockSpec((tm, n_z), row), pl.BlockSpec((tm, LANES), row)],
        out_shape=[jax.ShapeDtypeStruct((m, n_qkv), F32), jax.ShapeDtypeStruct((m, n_z), F32),
                   jax.ShapeDtypeStruct((m, LANES), F32)],
        compiler_params=_cparams("parallel"),
        name="odd_in_proj",
    )(x2d, norm_g[None, :], w_main, w_ba)


def _conv_kernel(x_ref, buf_ref, cw_ref, ba_ref, ab_ref, q_ref, k_ref, v_ref, gb_ref, nb_ref, ext_ref, *, tl, bt):
    l = pl.program_id(1)
    lo = CONV_PAD - CONV_HIST
    width = ext_ref.shape[-1]
    n_qk = DN_HEADS * DN_DIM
    for s in range(bt):
        ext = ext_ref.at[s]

        @pl.when(l == 0)
        def _():
            ext[0:lo, :] = jnp.zeros((lo, width), F32)
            ext[lo:CONV_PAD, :] = buf_ref[s]

        @pl.when(l > 0)
        def _():
            ext[lo:CONV_PAD, :] = ext[tl + lo:tl + CONV_PAD, :]

        ext[CONV_PAD:CONV_PAD + tl, :] = x_ref[s]
        for c in range(width // LANES):
            cols = slice(c * LANES, (c + 1) * LANES)
            conv = ext[lo:lo + tl, cols] * cw_ref[0:1, cols]
            for j in range(1, CONV_WIDTH):
                conv = conv + ext[lo + j:lo + j + tl, cols] * cw_ref[j:j + 1, cols]
            y = _silu(conv)
            oc = slice((c % DN_HEADS) * LANES, (c % DN_HEADS + 1) * LANES)
            if c * LANES < 2 * n_qk:
                y = y * lax.rsqrt(jnp.sum(y * y, axis=-1, keepdims=True) + NORM_EPS)
                if c * LANES < n_qk:
                    q_ref[s, :, oc] = y * (DN_DIM ** -0.5)
                else:
                    k_ref[s, :, oc] = y
            else:
                v_ref[s, :, oc] = y

        ba = ba_ref[s]
        x = ba + ab_ref[1:2, :]
        g = -jnp.exp(ab_ref[0:1, :]) * (jnp.maximum(x, 0.0) + _softplus_neg_abs(x))
        is_beta = lax.broadcasted_iota(jnp.int32, ba.shape, 1) < DN_HEADS
        gb_ref[s] = jnp.where(is_beta, _sigmoid(ba), g)

        @pl.when(l == pl.num_programs(1) - 1)
        def _():
            nb_ref[s] = ext[tl + lo:tl + CONV_PAD, :]


def _conv_gates(qkv, conv_buf, conv_w, ba, a_log, dt_bias):
    b, length, width = qkv.shape
    n = DN_HEADS * DN_DIM
    assert width == 3 * n and conv_buf.shape == (b, CONV_HIST, width)
    tl = _row_tile(length, 256)
    assert tl == length or tl >= CONV_PAD
    park = lambda vec: jnp.pad(vec, (DN_HEADS, LANES - 2 * DN_HEADS))
    ab = jnp.stack([park(a_log), park(dt_bias)])
    bt = _seqs_per_step(b, length)
    seq = lambda i, l: (i, l, 0)
    per_b = lambda i, l: (i, 0, 0)
    fixed = lambda i, l: (0, 0)
    return pl.pallas_call(
        functools.partial(_conv_kernel, tl=tl, bt=bt),
        grid=(b // bt, length // tl),
        in_specs=[pl.BlockSpec((bt, tl, width), seq), pl.BlockSpec((bt, CONV_HIST, width), per_b),
                  pl.BlockSpec((CONV_WIDTH, width), fixed), pl.BlockSpec((bt, tl, LANES), seq),
                  pl.BlockSpec((2, LANES), fixed)],
        out_specs=[pl.BlockSpec((bt, tl, n), seq)] * 3 + [pl.BlockSpec((bt, tl, LANES), seq),
                                                           pl.BlockSpec((bt, CONV_HIST, width), per_b)],
        out_shape=[jax.ShapeDtypeStruct((b, length, n), F32)] * 3
        + [jax.ShapeDtypeStruct((b, length, LANES), F32), jax.ShapeDtypeStruct((b, CONV_HIST, width), F32)],
        scratch_shapes=[pltpu.VMEM((bt, CONV_PAD + tl, width), F32)],
        compiler_params=_cparams("parallel", "arbitrary"),
        name="conv_norm_gates",
    )(qkv, conv_buf, conv_w, ba, ab)


def _unit_lower_inverses(lmats, c):
    eye = (lax.broadcasted_iota(jnp.int32, (c, c), 0) == lax.broadcasted_iota(jnp.int32, (c, c), 1)).astype(F32)
    powers = [-lm for lm in lmats]
    invs = [eye + p for p in powers]
    span = 2
    while span < c:
        powers = [_dot_f32(p, p) for p in powers]
        invs = [i + _dot_f32(i, p) for i, p in zip(invs, powers)]
        span *= 2
    return invs


def _head_cols(h):
    return slice(h * DN_DIM, (h + 1) * DN_DIM)


def _delta_local_kernel(q_ref, k_ref, v_ref, gb_ref, u_ref, wq_ref, kd_ref, att_ref, eg_ref, *, c, n_chunks, group):
    row = lax.broadcasted_iota(jnp.int32, (c, c), 0)
    col = lax.broadcasted_iota(jnp.int32, (c, c), 1)
    incl = row >= col
    strict = row > col
    prefix = incl.astype(BF16)
    eye_lanes = (lax.broadcasted_iota(jnp.int32, (LANES, LANES), 0)
                 == lax.broadcasted_iota(jnp.int32, (LANES, LANES), 1)).astype(BF16)
    units = [(g, h) for g in range(group) for h in range(DN_HEADS)]
    n = range(len(units))

    def chunks(step, _):
        ci = [step * group + g for g in range(group)]
        rows = [pl.ds(pl.multiple_of(x * c, c), c) for x in ci]
        gb = [gb_ref[r, :] for r in rows]
        gcs = [_dot_exact_lhs(prefix, x) for x in gb]
        gcs_t = []
        for x in gcs:
            h3, m3, l3 = _split3(x)
            gcs_t.append(_dot_nt(eye_lanes, h3) + _dot_nt(eye_lanes, m3) + _dot_nt(eye_lanes, l3))
        q = [q_ref[rows[g], _head_cols(h)] for g, h in units]
        k = [k_ref[rows[g], _head_cols(h)] for g, h in units]
        v = [v_ref[rows[g], _head_cols(h)] for g, h in units]
        beta = [gb[g][:, h:h + 1] for g, h in units]
        gc = [gcs[g][:, DN_HEADS + h:DN_HEADS + h + 1] for g, h in units]
        g_last = [gcs[g][c - 1:c, DN_HEADS + h:DN_HEADS + h + 1] for g, h in units]
        gc_row = [gcs_t[g][DN_HEADS + h:DN_HEADS + h + 1, :] for g, h in units]
        decay = [jnp.where(incl, jnp.exp(jnp.where(incl, gc[i] - gc_row[i], 0.0)), 0.0) for i in n]
        kb = [k[i] * beta[i] for i in n]
        k16 = [k[i].astype(BF16) for i in n]
        lmat = [jnp.where(strict, _dot_nt(kb[i].astype(BF16), k16[i]) * decay[i], 0.0) for i in n]
        tmat = [t.astype(BF16) for t in _unit_lower_inverses(lmat, c)]
        egc = [jnp.exp(gc[i]) for i in n]
        for i, (g, h) in enumerate(units):
            cols = _head_cols(h)
            u_ref[rows[g], cols] = _dot(tmat[i], (v[i] * beta[i]).astype(BF16))
            w = _dot(tmat[i], (kb[i] * egc[i]).astype(BF16))
            wq_ref[ci[g], :, cols] = jnp.concatenate([w, q[i] * egc[i]], axis=0).astype(BF16)
            kd_ref[rows[g], cols] = (k[i] * jnp.exp(g_last[i] - gc[i])).astype(BF16)
            att = _dot_nt(q[i].astype(BF16), k16[i]) * decay[i]
            if c < LANES:
                att = jnp.concatenate([att, jnp.zeros((c, LANES - c), F32)], axis=1)
            att_ref[rows[g], cols] = att.astype(BF16)
            eg_ref[ci[g], h:h + 1, :] = jnp.broadcast_to(jnp.exp(g_last[i]), (1, LANES))
        return 0

    assert n_chunks % group == 0
    lax.fori_loop(0, n_chunks // group, chunks, 0)


def _delta_scan_kernel(u_ref, wq_ref, kd_ref, att_ref, eg_ref, s0_ref, o_ref, s_out_ref, s_ref, *, c, n_chunks, bt):
    l = pl.program_id(1)

    @pl.when(l == 0)
    def _():
        s_ref[...] = s0_ref[...]

    eye_lanes = (lax.broadcasted_iota(jnp.int32, (LANES, LANES), 0)
                 == lax.broadcasted_iota(jnp.int32, (LANES, LANES), 1)).astype(BF16)
    units = [(i, h) for i in range(bt) for h in range(DN_HEADS)]

    def chunk(ci, _):
        rows = pl.ds(pl.multiple_of(ci * c, c), c)
        s = [s_ref[i, h] for i, h in units]
        s16 = [x.astype(BF16) for x in s]
        r = [_dot(wq_ref[i, ci, :, _head_cols(h)], s16[n]) for n, (i, h) in enumerate(units)]
        v16 = [(u_ref[i, rows, _head_cols(h)] - r[n][0:c]).astype(BF16) for n, (i, h) in enumerate(units)]
        kd_t = [_dot_nt(eye_lanes, kd_ref[i, rows, _head_cols(h)]).astype(BF16) for i, h in units]
        for n, (i, h) in enumerate(units):
            cols = _head_cols(h)
            att = att_ref[i, rows, h * DN_DIM:h * DN_DIM + c]
            o_ref[i, rows, cols] = r[n][c:2 * c] + _dot(att, v16[n])
            s_ref[i, h] = s[n] * eg_ref[i, ci, h:h + 1, :] + _dot(kd_t[n], v16[n])
        return 0

    lax.fori_loop(0, n_chunks, chunk, 0)

    @pl.when(l == pl.num_programs(1) - 1)
    def _():
        s_out_ref[...] = s_ref[...]


def _gated_delta(q, k, v, gb, state, c):
    b, length, n = q.shape
    assert n == DN_HEADS * DN_DIM and state.shape == (b, DN_HEADS, DN_DIM, DN_DIM) and length % c == 0
    assert c % 8 == 0 and c <= LANES
    tokens = b * length
    flat = lambda a: a.reshape(tokens, a.shape[-1])
    tn = _row_tile(tokens, max(c, 256))
    row = lambda i: (i, 0)
    per_chunk = lambda i: (i, 0, 0)
    u, wq, kd, att, eg = pl.pallas_call(
        functools.partial(_delta_local_kernel, c=c, n_chunks=tn // c, group=2),
        grid=(tokens // tn,),
        in_specs=[pl.BlockSpec((tn, n), row)] * 3 + [pl.BlockSpec((tn, LANES), row)],
        out_specs=[pl.BlockSpec((tn, n), row), pl.BlockSpec((tn // c, 2 * c, n), per_chunk),
                   pl.BlockSpec((tn, n), row), pl.BlockSpec((tn, n), row),
                   pl.BlockSpec((tn // c, DN_HEADS, LANES), per_chunk)],
        out_shape=[jax.ShapeDtypeStruct((tokens, n), F32), jax.ShapeDtypeStruct((tokens // c, 2 * c, n), BF16),
                   jax.ShapeDtypeStruct((tokens, n), BF16), jax.ShapeDtypeStruct((tokens, n), BF16),
                   jax.ShapeDtypeStruct((tokens // c, DN_HEADS, LANES), F32)],
        compiler_params=_cparams("parallel"),
        name="delta_local",
    )(flat(q), flat(k), flat(v), flat(gb))

    n_seq_chunks = length // c
    tl = c * min(n_seq_chunks, 4)
    bt = 2 if n_seq_chunks > 1 else 4
    assert length % tl == 0 and b % bt == 0
    seq = lambda i, l: (i, l, 0)
    seq_chunks = lambda i, l: (i, l, 0, 0)
    per_b = lambda i, l: (i, 0, 0, 0)
    st_block = (bt,) + state.shape[1:]
    return pl.pallas_call(
        functools.partial(_delta_scan_kernel, c=c, n_chunks=tl // c, bt=bt),
        grid=(b // bt, length // tl),
        in_specs=[pl.BlockSpec((bt, tl, n), seq), pl.BlockSpec((bt, tl // c, 2 * c, n), seq_chunks),
                  pl.BlockSpec((bt, tl, n), seq), pl.BlockSpec((bt, tl, n), seq),
                  pl.BlockSpec((bt, tl // c, DN_HEADS, LANES), seq_chunks), pl.BlockSpec(st_block, per_b)],
        out_specs=[pl.BlockSpec((bt, tl, n), seq), pl.BlockSpec(st_block, per_b)],
        out_shape=[jax.ShapeDtypeStruct((b, length, n), F32), jax.ShapeDtypeStruct(state.shape, F32)],
        scratch_shapes=[pltpu.VMEM(st_block, F32)],
        compiler_params=_cparams("parallel", "arbitrary"),
        name="delta_scan",
    )(u.reshape(b, length, n), wq.reshape(b, n_seq_chunks, 2 * c, n), kd.reshape(b, length, n),
      att.reshape(b, length, n), eg.reshape(b, n_seq_chunks, DN_HEADS, LANES), state)


def _odd_out_kernel(h_ref, o_ref, z_ref, og_ref, w_ref, y_ref):
    acc = h_ref[...]
    for hd in range(DN_HEADS):
        cols = slice(hd * DN_DIM, (hd + 1) * DN_DIM)
        o = _rms_rows(o_ref[:, cols], og_ref[...]) * _silu(z_ref[:, cols])
        acc = acc + _dot(o.astype(BF16), w_ref[cols, :])
    y_ref[...] = acc


def _odd_out(h2d, o2d, z, o_g, w_out):
    m, d = h2d.shape
    n = o2d.shape[-1]
    tm = _row_tile(m, 512)
    row = lambda i: (i, 0)
    fixed = lambda i: (0, 0)
    return pl.pallas_call(
        _odd_out_kernel,
        grid=(m // tm,),
        in_specs=[pl.BlockSpec((tm, d), row), pl.BlockSpec((tm, n), row), pl.BlockSpec((tm, n), row),
                  pl.BlockSpec((1, DN_DIM), fixed), pl.BlockSpec((n, d), fixed)],
        out_specs=pl.BlockSpec((tm, d), row),
        out_shape=jax.ShapeDtypeStruct((m, d), F32),
        compiler_params=_cparams("parallel"),
        name="odd_out_proj",
    )(h2d, o2d, z, o_g[None, :], w_out.astype(BF16))


def _odd_layer(h, conv_buf, state, norm_g, w_in, conv_w, a_log, dt_bias, o_g, w_out):
    b, length, d = h.shape
    h2d = h.reshape(b * length, d)
    qkv, z, ba = _odd_proj(h2d, norm_g, w_in)
    seq = lambda a: a.reshape(b, length, a.shape[-1])
    q, k, v, gb, new_buf = _conv_gates(seq(qkv), conv_buf, conv_w, seq(ba), a_log, dt_bias)
    c = DN_CHUNK if length % DN_CHUNK == 0 else length
    c_pad = -(-c // 8) * 8
    if c_pad != c:
        pad = lambda a: jnp.pad(a, ((0, 0), (0, c_pad - c), (0, 0)))
        o, new_state = _gated_delta(pad(q), pad(k), pad(v), pad(gb), state, c_pad)
        o = o[:, :length]
    else:
        o, new_state = _gated_delta(q, k, v, gb, state, c)
    y = _odd_out(h2d, o.reshape(b * length, -1), z, o_g, w_out)
    return y.reshape(b, length, d), new_buf, new_state


def kernel(x_prompt, x_sample, cache_pool, cache_k, cache_v, state_conv, state_delta, page_table, norm_even, w_in_even, w_pool, pool_scale, q_norm, k_norm, sb_bias, w_out_even, norm_odd, w_in_odd, conv_w, a_log, dt_bias, o_norm, w_out_odd):
    bp = x_prompt.shape[0]
    past_len = page_table.shape[1] * cache_k.shape[2]
    depth = norm_even.shape[0] + norm_odd.shape[0]
    hp, hs = x_prompt, x_sample
    outs = {name: [] for name in ("pool_p", "pool_s", "k_p", "v_p", "k_s", "v_s", "conv_p", "conv_s", "st_p", "st_s")}
    for layer in range(depth):
        i = layer // 2
        if layer % 2 == 0:
            ev = (norm_even[i], w_in_even[i], w_pool[i], pool_scale[i], q_norm[i], k_norm[i], sb_bias[i], w_out_even[i])
            zero_buf = jnp.zeros((bp, POOL_HIST, cache_pool.shape[-1]), F32)
            hp, pb, kn, vn = _even_layer(hp, 0, zero_buf, None, *ev)
            outs["pool_p"].append(pb), outs["k_p"].append(kn), outs["v_p"].append(vn)
            hs, pb, kn, vn = _even_layer(hs, past_len, cache_pool[i], (cache_k[i], cache_v[i], page_table), *ev)
            outs["pool_s"].append(pb), outs["k_s"].append(kn), outs["v_s"].append(vn)
        else:
            od = (norm_odd[i], w_in_odd[i], conv_w[i], a_log[i], dt_bias[i], o_norm[i], w_out_odd[i])
            zero_conv = jnp.zeros((bp, CONV_HIST, state_conv.shape[-1]), F32)
            zero_state = jnp.zeros((bp,) + state_delta.shape[2:], F32)
            hp, cb, st = _odd_layer(hp, zero_conv, zero_state, *od)
            outs["conv_p"].append(cb), outs["st_p"].append(st)
            hs, cb, st = _odd_layer(hs, state_conv[i], state_delta[i], *od)
            outs["conv_s"].append(cb), outs["st_s"].append(st)
    stack = lambda name: jnp.stack(outs[name])
    return (hp, hs, stack("pool_p"), stack("pool_s"), stack("k_p"), stack("v_p"), stack("k_s"), stack("v_s"),
            stack("conv_p"), stack("conv_s"), stack("st_p"), stack("st_s"))
```

```python
import functools
import math

import jax
import jax.numpy as jnp
from jax import lax
from jax.experimental import pallas as pl
from jax.experimental.pallas import tpu as pltpu

F32 = jnp.float32
BF16 = jnp.bfloat16

NORM_EPS = 1e-6
POOL_WINDOWS = (2, 4, 8, 16)
POOL_HIST = max(POOL_WINDOWS) - 1
POOL_PAD = 16
SB_HEADS = 8
SB_HEAD_DIM = 64
SB_BAND = 32
DN_HEADS = 8
DN_DIM = 128
CONV_WIDTH = 4
CONV_HIST = CONV_WIDTH - 1
CONV_PAD = 8
DN_CHUNK = 64
LANES = 128
PAGE = 128
VMEM_LIMIT_BYTES = 48 * 1024 * 1024


def _cparams(*sem):
    return pltpu.CompilerParams(dimension_semantics=sem, vmem_limit_bytes=VMEM_LIMIT_BYTES)


def _dot(a, b):
    return jnp.dot(a, b, preferred_element_type=F32)


def _dot_nt(a, b):
    return lax.dot_general(a, b, (((1,), (1,)), ((), ())), preferred_element_type=F32)


def _split2(x):
    hi = x.astype(BF16)
    return hi, (x - hi.astype(F32)).astype(BF16)


def _split3(x):
    hi = x.astype(BF16)
    r = x - hi.astype(F32)
    mid = r.astype(BF16)
    return hi, mid, (r - mid.astype(F32)).astype(BF16)


def _dot_exact_rhs(a, b_bf16):
    hi, mid, lo = _split3(a)
    return _dot(hi, b_bf16) + _dot(mid, b_bf16) + _dot(lo, b_bf16)


def _dot_exact_lhs(a_bf16, b):
    hi, mid, lo = _split3(b)
    return _dot(a_bf16, hi) + _dot(a_bf16, mid) + _dot(a_bf16, lo)


def _dot_f32(a, b):
    ah, al = _split2(a)
    bh, bl = _split2(b)
    return _dot(ah, bh) + (_dot(ah, bl) + _dot(al, bh))


def _sigmoid(x):
    return 1.0 / (1.0 + jnp.exp(-x))


def _silu(x):
    return x * _sigmoid(x)


def _softplus_neg_abs(x):
    return jnp.log1p(jnp.exp(-jnp.abs(x)))


def _rms_rows(x, g):
    return x * lax.rsqrt(jnp.mean(x * x, axis=-1, keepdims=True) + NORM_EPS) * g


def _row_tile(m, want):
    t = min(m, want)
    assert m % t == 0, (m, t)
    return t


def _even_proj_kernel(x_ref, g_ref, w_ref, seg_ref, qg_ref, kg_ref,
                      u_ref, q_ref, k_ref, v_ref, kb_ref, vb_ref, gate_ref):
    xn = _rms_rows(x_ref[...], g_ref[...]).astype(BF16)
    pw = u_ref.shape[-1]

    def proj(c0, c1):
        return _dot(xn, w_ref[:, c0:c1])

    def head_norm(y, gain):
        ms = _dot((y * y).astype(BF16), seg_ref[...]) * (1.0 / SB_HEAD_DIM)
        return y * lax.rsqrt(ms + NORM_EPS) * gain

    u_ref[...] = proj(0, pw)
    q = head_norm(proj(pw, 2 * pw), qg_ref[...])
    q_ref[...] = (q * (SB_HEAD_DIM ** -0.5)).astype(BF16)
    k = head_norm(proj(2 * pw, 3 * pw), kg_ref[...])
    k_ref[...] = k
    kb_ref[...] = k.astype(BF16)
    v = proj(3 * pw, 4 * pw)
    v_ref[...] = v
    vb_ref[...] = v.astype(BF16)
    gate_ref[...] = proj(4 * pw, 6 * pw)


def _even_proj(x2d, norm_g, w_in, q_g, k_g):
    m, d = x2d.shape
    pw = SB_HEADS * SB_HEAD_DIM
    assert w_in.shape == (d, 6 * pw)
    tm = _row_tile(m, 512)
    lane_head = jnp.arange(pw) // SB_HEAD_DIM
    seg = (lane_head[:, None] == lane_head[None, :]).astype(BF16)
    row = lambda i: (i, 0)
    fixed = lambda i: (0, 0)
    outs = [((m, pw), F32), ((m, pw), BF16), ((m, pw), F32), ((m, pw), F32), ((m, pw), BF16), ((m, pw), BF16),
            ((m, 2 * pw), F32)]
    return pl.pallas_call(
        _even_proj_kernel,
        grid=(m // tm,),
        in_specs=[pl.BlockSpec((tm, d), row), pl.BlockSpec((1, d), fixed), pl.BlockSpec((d, 6 * pw), fixed),
                  pl.BlockSpec((pw, pw), fixed), pl.BlockSpec((1, pw), fixed), pl.BlockSpec((1, pw), fixed)],
        out_specs=[pl.BlockSpec((tm, s[-1]), row) for s, _ in outs],
        out_shape=[jax.ShapeDtypeStruct(s, t) for s, t in outs],
        compiler_params=_cparams("parallel"),
        name="even_in_proj",
    )(x2d, norm_g[None, :], w_in.astype(BF16), seg,
      jnp.tile(q_g, SB_HEADS)[None, :], jnp.tile(k_g, SB_HEADS)[None, :])


def _seqs_per_step(b, length):
    bt = 8 if length <= 8 else 1
    assert b % bt == 0
    return bt


def _pool_kernel(u_ref, buf_ref, wp_ref, scale_ref, y_ref, nb_ref, ext_ref, *, start, tl, bt):
    l = pl.program_id(1)
    lo = POOL_PAD - POOL_HIST
    pos = start + l * tl + lax.broadcasted_iota(jnp.int32, (tl, 1), 0)
    for s in range(bt):
        ext = ext_ref.at[s]

        @pl.when(l == 0)
        def _():
            ext[0:lo, :] = jnp.zeros((lo, ext.shape[-1]), F32)
            ext[lo:POOL_PAD, :] = buf_ref[s]

        @pl.when(l > 0)
        def _():
            ext[lo:POOL_PAD, :] = ext[tl + lo:tl + POOL_PAD, :]

        ext[POOL_PAD:POOL_PAD + tl, :] = u_ref[s]
        for gi, w in enumerate(POOL_WINDOWS):
            cols = slice(gi * LANES, (gi + 1) * LANES)
            cur = ext[POOL_PAD:POOL_PAD + tl, cols]
            win = cur
            for i in range(1, w):
                win = win + ext[POOL_PAD - i:POOL_PAD - i + tl, cols]
            cnt = jnp.minimum(pos + 1, w).astype(F32)
            d = win / cnt - cur
            y_ref[s, :, cols] = _dot(d.astype(BF16), wp_ref[gi]) * scale_ref[:, cols]

        @pl.when(l == pl.num_programs(1) - 1)
        def _():
            nb_ref[s] = ext[tl + lo:tl + POOL_PAD, :]


def _pool_mix(u, buf, start, w_pool, scale):
    b, length, p = u.shape
    assert p == len(POOL_WINDOWS) * LANES and buf.shape == (b, POOL_HIST, p)
    tl = _row_tile(length, 512)
    assert tl == length or tl >= POOL_PAD
    bt = _seqs_per_step(b, length)
    seq = lambda i, l: (i, l, 0)
    per_b = lambda i, l: (i, 0, 0)
    return pl.pallas_call(
        functools.partial(_pool_kernel, start=start, tl=tl, bt=bt),
        grid=(b // bt, length // tl),
        in_specs=[pl.BlockSpec((bt, tl, p), seq), pl.BlockSpec((bt, POOL_HIST, p), per_b),
                  pl.BlockSpec(w_pool.shape, lambda i, l: (0, 0, 0)), pl.BlockSpec((1, p), lambda i, l: (0, 0))],
        out_specs=[pl.BlockSpec((bt, tl, p), seq), pl.BlockSpec((bt, POOL_HIST, p), per_b)],
        out_shape=[jax.ShapeDtypeStruct((b, length, p), F32), jax.ShapeDtypeStruct((b, POOL_HIST, p), F32)],
        scratch_shapes=[pltpu.VMEM((bt, POOL_PAD + tl, p), F32)],
        compiler_params=_cparams("parallel", "arbitrary"),
        name="pool_mix",
    )(u, buf, w_pool.astype(BF16), scale[None, :])


def _log_beta_keep(z):
    m = jnp.minimum(z, 0.0)
    d = m - z
    s = jnp.log(1.0 + jnp.exp(m + d))
    return m - s, d - s


def _stick_break_tiles(zs, carries, tri, masks, chained=False):
    n = range(len(zs))
    log_beta, log_keep = zip(*[_log_beta_keep(z) for z in zs])
    log_keep = [lk if m is None else jnp.where(m, lk, 0.0) for lk, m in zip(log_keep, masks)]
    totals = [jnp.sum(lk, axis=-1, keepdims=True) for lk in log_keep]
    if chained:
        carry_in = [carries[0]]
        for i in n:
            carry_in.append(carry_in[i] + totals[i])
        new_carries = carry_in[-1:]
    else:
        carry_in = carries
        new_carries = [carries[i] + totals[i] for i in n]
    suffix = [_dot(lk.astype(BF16), tri) + carry_in[i] for i, lk in enumerate(log_keep)]
    a = [jnp.exp(log_beta[i] + suffix[i]) for i in n]
    a = [x if m is None else jnp.where(m, x, 0.0) for x, m in zip(a, masks)]
    return [x.astype(BF16) for x in a], new_carries


def _later_key_matrix(n):
    return (lax.broadcasted_iota(jnp.int32, (n, n), 0) > lax.broadcasted_iota(jnp.int32, (n, n), 1)).astype(BF16)


def _sb_prompt_kernel(bias_ref, q_ref, k_ref, v_ref, o_ref, acc_ref, carry_ref, tot_ref, lb_ref, lk_ref, a_ref, *, t):
    qi = pl.program_id(1)
    heads = range(SB_HEADS)
    low_lanes = lax.broadcasted_iota(jnp.int32, (t, LANES), 1) < SB_HEAD_DIM
    tri = _later_key_matrix(t)
    causal = lax.broadcasted_iota(jnp.int32, (t, t), 1) < lax.broadcasted_iota(jnp.int32, (t, t), 0)
    pair_cols = lambda h: slice((h // 2) * LANES, (h // 2 + 1) * LANES)
    qh = []
    for h in heads:
        q_pair = q_ref[0, :, pair_cols(h)]
        qh.append(jnp.where(low_lanes if h % 2 == 0 else jnp.logical_not(low_lanes), q_pair, jnp.zeros_like(q_pair)))

    bands = [slice(r, r + SB_BAND) for r in range(0, t, SB_BAND)]

    def tiles(kb, mask):
        rows = pl.ds(pl.multiple_of(kb * t, t), t)
        for h in heads:
            z = _dot_nt(qh[h], k_ref[0, rows, pair_cols(h)]) + bias_ref[h]
            for band in bands:
                log_beta, log_keep = _log_beta_keep(z[band])
                if mask is not None:
                    log_keep = jnp.where(mask[band], log_keep, 0.0)
                lb_ref[h, band, :] = log_beta
                lk_ref[h, band, :] = log_keep.astype(BF16)
                tot_ref[h, band, :] = jnp.sum(log_keep, axis=-1, keepdims=True)
        for h in heads:
            suffix = _dot(lk_ref[h], tri)
            for band in bands:
                carry = carry_ref[h, band, :]
                a = jnp.exp(lb_ref[h, band, :] + (suffix[band] + carry))
                if mask is not None:
                    a = jnp.where(mask[band], a, 0.0)
                a_ref[h, band, :] = a.astype(BF16)
                carry_ref[h, band, :] = carry + tot_ref[h, band, :]
        for h in heads:
            acc_ref[h] += _dot(a_ref[h], v_ref[0, rows, pair_cols(h)])

    acc_ref[...] = jnp.zeros(acc_ref.shape, F32)
    carry_ref[...] = jnp.zeros(carry_ref.shape, F32)
    tiles(qi, causal)

    def earlier(j, _):
        tiles(qi - 1 - j, None)
        return 0

    lax.fori_loop(0, qi, earlier, 0)
    for h in range(0, SB_HEADS, 2):
        o_ref[0, :, pair_cols(h)] = jnp.where(low_lanes, acc_ref[h], acc_ref[h + 1])


def _sb_prompt(q, k, v, bias):
    b, length, width = q.shape
    assert width == SB_HEADS * SB_HEAD_DIM and 2 * SB_HEAD_DIM == LANES
    t = _row_tile(length, 256)
    q_blk = lambda i, j: (i, j, 0)
    kv_blk = lambda i, j: (i, 0, 0)
    return pl.pallas_call(
        functools.partial(_sb_prompt_kernel, t=t),
        grid=(b, length // t),
        in_specs=[pl.BlockSpec(memory_space=pltpu.SMEM), pl.BlockSpec((1, t, width), q_blk),
                  pl.BlockSpec((1, length, width), kv_blk), pl.BlockSpec((1, length, width), kv_blk)],
        out_specs=pl.BlockSpec((1, t, width), q_blk),
        scratch_shapes=[pltpu.VMEM((SB_HEADS, t, LANES), F32), pltpu.VMEM((SB_HEADS, t, 1), F32),
                        pltpu.VMEM((SB_HEADS, t, 1), F32), pltpu.VMEM((SB_HEADS, t, t), F32),
                        pltpu.VMEM((SB_HEADS, t, t), BF16), pltpu.VMEM((SB_HEADS, t, t), BF16)],
        out_shape=jax.ShapeDtypeStruct((b, length, width), F32),
        compiler_params=_cparams("parallel", "arbitrary"),
        name="stick_break_prompt",
    )(bias, q, k, v)


def _sb_sample_kernel(pt_ref, q_ref, bias_ref, kn_ref, vn_ref, *refs, n_pages, n_new):
    k_refs, v_refs, o_ref = refs[:n_pages], refs[n_pages:2 * n_pages], refs[2 * n_pages]
    kpad_ref, vpad_ref = refs[2 * n_pages + 1:]
    rows = SB_HEADS * n_new
    width = SB_HEADS * SB_HEAD_DIM
    row_i = lax.broadcasted_iota(jnp.int32, (rows, width), 0)
    lane_i = lax.broadcasted_iota(jnp.int32, (rows, width), 1)
    own_head = (row_i // n_new) == (lane_i // SB_HEAD_DIM)
    qbd = jnp.where(own_head, q_ref[0], jnp.zeros_like(q_ref[0]))
    bias = bias_ref[...]
    tri = _later_key_matrix(PAGE)

    kpad_ref[...] = jnp.zeros(kpad_ref.shape, BF16)
    vpad_ref[...] = jnp.zeros(vpad_ref.shape, BF16)
    kpad_ref[0:kn_ref.shape[1], :] = kn_ref[0].astype(BF16)
    vpad_ref[0:vn_ref.shape[1], :] = vn_ref[0].astype(BF16)
    qpos = lax.broadcasted_iota(jnp.int32, (rows, PAGE), 0) % n_new
    kpos = lax.broadcasted_iota(jnp.int32, (rows, PAGE), 1)

    pages = list(reversed(range(n_pages)))
    zs = [_dot_nt(qbd, kpad_ref[...]) + bias] + [_dot(qbd, k_refs[p][0].astype(BF16)) + bias for p in pages]
    a, _ = _stick_break_tiles(zs, [jnp.zeros((rows, 1), F32)], tri, [kpos < qpos] + [None] * n_pages, chained=True)
    out = _dot(a[0], vpad_ref[...])
    for i, p in enumerate(pages):
        out = out + _dot_nt(a[i + 1], v_refs[p][0].astype(BF16))
    out = jnp.where(own_head, out, 0.0)
    acc = out[0:n_new]
    for h in range(1, SB_HEADS):
        acc = acc + out[h * n_new:(h + 1) * n_new]
    o_ref[0] = acc


def _sb_sample(q, k_new, v_new, bias, cache_k, cache_v, page_table):
    b, n_new, width = q.shape
    n_pages = page_table.shape[1]
    n_phys = cache_k.shape[0]
    assert cache_k.shape[1:] == (PAGE, SB_HEADS, SB_HEAD_DIM) and n_new <= 8
    ck = jnp.transpose(cache_k, (0, 2, 3, 1)).reshape(n_phys, width, PAGE)
    cv = jnp.transpose(cache_v, (0, 2, 3, 1)).reshape(n_phys, width, PAGE)
    rows = SB_HEADS * n_new
    pad = ((0, 0), (0, 8 - n_new), (0, 0))
    per_b = lambda i, pt: (i, 0, 0)
    page_specs = [pl.BlockSpec((1, width, PAGE), functools.partial(lambda i, pt, p: (pt[i, p], 0, 0), p=p))
                  for p in range(n_pages)]
    return pl.pallas_call(
        functools.partial(_sb_sample_kernel, n_pages=n_pages, n_new=n_new),
        grid_spec=pltpu.PrefetchScalarGridSpec(
            num_scalar_prefetch=1,
            grid=(b,),
            in_specs=[pl.BlockSpec((1, rows, width), per_b), pl.BlockSpec((rows, 1), lambda i, pt: (0, 0)),
                      pl.BlockSpec((1, 8, width), per_b), pl.BlockSpec((1, 8, width), per_b)] + page_specs + page_specs,
            out_specs=pl.BlockSpec((1, n_new, width), per_b),
            scratch_shapes=[pltpu.VMEM((PAGE, width), BF16), pltpu.VMEM((PAGE, width), BF16)]),
        out_shape=jax.ShapeDtypeStruct((b, n_new, width), F32),
        compiler_params=_cparams("parallel"),
        name="stick_break_sample",
    )(page_table, jnp.tile(q, (1, SB_HEADS, 1)), jnp.repeat(bias, n_new)[:, None],
      jnp.pad(k_new, pad), jnp.pad(v_new, pad), *([ck] * n_pages), *([cv] * n_pages))


def _even_out_kernel(h_ref, py_ref, sb_ref, gate_ref, w_ref, y_ref):
    pw = py_ref.shape[-1]
    m_pool = (py_ref[...] * _silu(gate_ref[:, 0:pw])).astype(BF16)
    m_sb = (sb_ref[...] * _silu(gate_ref[:, pw:2 * pw])).astype(BF16)
    y_ref[...] = h_ref[...] + (_dot(m_pool, w_ref[0:pw, :]) + _dot(m_sb, w_ref[pw:2 * pw, :]))


def _even_out(h2d, pool_y, sb, gate, w_out):
    m, d = h2d.shape
    pw = pool_y.shape[-1]
    tm = _row_tile(m, 512)
    row = lambda i: (i, 0)
    return pl.pallas_call(
        _even_out_kernel,
        grid=(m // tm,),
        in_specs=[pl.BlockSpec((tm, d), row), pl.BlockSpec((tm, pw), row), pl.BlockSpec((tm, pw), row),
                  pl.BlockSpec((tm, 2 * pw), row), pl.BlockSpec((2 * pw, d), lambda i: (0, 0))],
        out_specs=pl.BlockSpec((tm, d), row),
        out_shape=jax.ShapeDtypeStruct((m, d), F32),
        compiler_params=_cparams("parallel"),
        name="even_out_proj",
    )(h2d, pool_y, sb, gate, w_out.astype(BF16))


def _even_layer(h, start, pool_buf, past, norm_g, w_in, w_pool, pool_scale, q_g, k_g, sb_bias, w_out):
    b, length, d = h.shape
    h2d = h.reshape(b * length, d)
    u, q, k, v, kb, vb, gate = _even_proj(h2d, norm_g, w_in, q_g, k_g)
    seq = lambda a: a.reshape(b, length, a.shape[-1])
    pool_y, new_buf = _pool_mix(seq(u), pool_buf, start, w_pool, pool_scale)
    if past is None:
        sb = _sb_prompt(seq(q), seq(kb), seq(vb), sb_bias)
    else:
        sb = _sb_sample(seq(q), seq(k), seq(v), sb_bias, *past)
    y = _even_out(h2d, pool_y.reshape(b * length, -1), sb.reshape(b * length, -1), gate, w_out)
    heads = lambda a: a.reshape(b, length, SB_HEADS, SB_HEAD_DIM)
    return y.reshape(b, length, d), new_buf, heads(k), heads(v)


def _odd_proj_kernel(x_ref, g_ref, w_ref, wba_ref, qkv_ref, z_ref, ba_ref, *, chunk):
    xn = _rms_rows(x_ref[...], g_ref[...]).astype(BF16)
    n_qkv = qkv_ref.shape[-1]
    for c0 in range(0, n_qkv, chunk):
        qkv_ref[:, c0:c0 + chunk] = _dot(xn, w_ref[:, c0:c0 + chunk])
    for c0 in range(0, z_ref.shape[-1], chunk):
        z_ref[:, c0:c0 + chunk] = _dot(xn, w_ref[:, n_qkv + c0:n_qkv + c0 + chunk])
    ba_ref[...] = _dot(xn, wba_ref[...])


def _odd_proj(x2d, norm_g, w_in):
    m, d = x2d.shape
    n_qkv, n_z = 3 * DN_HEADS * DN_DIM, DN_HEADS * DN_DIM
    assert w_in.shape == (d, n_qkv + n_z + 2 * DN_HEADS)
    tm = _row_tile(m, 256)
    w_main = w_in[:, :n_qkv + n_z].astype(BF16)
    w_ba = jnp.pad(w_in[:, n_qkv + n_z:], ((0, 0), (0, LANES - 2 * DN_HEADS))).astype(BF16)
    row = lambda i: (i, 0)
    fixed = lambda i: (0, 0)
    return pl.pallas_call(
        functools.partial(_odd_proj_kernel, chunk=512),
        grid=(m // tm,),
        in_specs=[pl.BlockSpec((tm, d), row), pl.BlockSpec((1, d), fixed), pl.BlockSpec(w_main.shape, fixed),
                  pl.BlockSpec(w_ba.shape, fixed)],
        out_specs=[pl.BlockSpec((tm, n_qkv), row), pl.BlockSpec((tm, n_z), row), pl.BlockSpec((tm, LANES), row)],
        out_shape=[jax.ShapeDtypeStruct((m, n_qkv), F32), jax.ShapeDtypeStruct((m, n_z), F32),
                   jax.ShapeDtypeStruct((m, LANES), F32)],
        compiler_params=_cparams("parallel"),
        name="odd_in_proj",
    )(x2d, norm_g[None, :], w_main, w_ba)


def _conv_kernel(x_ref, buf_ref, cw_ref, ba_ref, ab_ref, q_ref, k_ref, v_ref, gb_ref, nb_ref, ext_ref, *, tl, bt):
    l = pl.program_id(1)
    lo = CONV_PAD - CONV_HIST
    width = ext_ref.shape[-1]
    n_qk = DN_HEADS * DN_DIM
    for s in range(bt):
        ext = ext_ref.at[s]

        @pl.when(l == 0)
        def _():
            ext[0:lo, :] = jnp.zeros((lo, width), F32)
            ext[lo:CONV_PAD, :] = buf_ref[s]

        @pl.when(l > 0)
        def _():
            ext[lo:CONV_PAD, :] = ext[tl + lo:tl + CONV_PAD, :]

        ext[CONV_PAD:CONV_PAD + tl, :] = x_ref[s]
        for c in range(width // LANES):
            cols = slice(c * LANES, (c + 1) * LANES)
            conv = ext[lo:lo + tl, cols] * cw_ref[0:1, cols]
            for j in range(1, CONV_WIDTH):
                conv = conv + ext[lo + j:lo + j + tl, cols] * cw_ref[j:j + 1, cols]
            y = _silu(conv)
            oc = slice((c % DN_HEADS) * LANES, (c % DN_HEADS + 1) * LANES)
            if c * LANES < 2 * n_qk:
                y = y * lax.rsqrt(jnp.sum(y * y, axis=-1, keepdims=True) + NORM_EPS)
                if c * LANES < n_qk:
                    q_ref[s, :, oc] = y * (DN_DIM ** -0.5)
                else:
                    k_ref[s, :, oc] = y
            else:
                v_ref[s, :, oc] = y

        ba = ba_ref[s]
        x = ba + ab_ref[1:2, :]
        g = -jnp.exp(ab_ref[0:1, :]) * (jnp.maximum(x, 0.0) + _softplus_neg_abs(x))
        is_beta = lax.broadcasted_iota(jnp.int32, ba.shape, 1) < DN_HEADS
        gb_ref[s] = jnp.where(is_beta, _sigmoid(ba), g)

        @pl.when(l == pl.num_programs(1) - 1)
        def _():
            nb_ref[s] = ext[tl + lo:tl + CONV_PAD, :]


def _conv_gates(qkv, conv_buf, conv_w, ba, a_log, dt_bias):
    b, length, width = qkv.shape
    n = DN_HEADS * DN_DIM
    assert width == 3 * n and conv_buf.shape == (b, CONV_HIST, width)
    tl = _row_tile(length, 256)
    assert tl == length or tl >= CONV_PAD
    park = lambda vec: jnp.pad(vec, (DN_HEADS, LANES - 2 * DN_HEADS))
    ab = jnp.stack([park(a_log), park(dt_bias)])
    bt = _seqs_per_step(b, length)
    seq = lambda i, l: (i, l, 0)
    per_b = lambda i, l: (i, 0, 0)
    fixed = lambda i, l: (0, 0)
    return pl.pallas_call(
        functools.partial(_conv_kernel, tl=tl, bt=bt),
        grid=(b // bt, length // tl),
        in_specs=[pl.BlockSpec((bt, tl, width), seq), pl.BlockSpec((bt, CONV_HIST, width), per_b),
                  pl.BlockSpec((CONV_WIDTH, width), fixed), pl.BlockSpec((bt, tl, LANES), seq),
                  pl.BlockSpec((2, LANES), fixed)],
        out_specs=[pl.BlockSpec((bt, tl, n), seq)] * 3 + [pl.BlockSpec((bt, tl, LANES), seq),
                                                           pl.BlockSpec((bt, CONV_HIST, width), per_b)],
        out_shape=[jax.ShapeDtypeStruct((b, length, n), F32)] * 3
        + [jax.ShapeDtypeStruct((b, length, LANES), F32), jax.ShapeDtypeStruct((b, CONV_HIST, width), F32)],
        scratch_shapes=[pltpu.VMEM((bt, CONV_PAD + tl, width), F32)],
        compiler_params=_cparams("parallel", "arbitrary"),
        name="conv_norm_gates",
    )(qkv, conv_buf, conv_w, ba, ab)


def _unit_lower_inverses(lmats, c):
    eye = (lax.broadcasted_iota(jnp.int32, (c, c), 0) == lax.broadcasted_iota(jnp.int32, (c, c), 1)).astype(F32)
    powers = [-lm for lm in lmats]
    invs = [eye + p for p in powers]
    span = 2
    while span < c:
        powers = [_dot_f32(p, p) for p in powers]
        invs = [i + _dot_f32(i, p) for i, p in zip(invs, powers)]
        span *= 2
    return invs


def _head_cols(h):
    return slice(h * DN_DIM, (h + 1) * DN_DIM)


def _delta_local_kernel(q_ref, k_ref, v_ref, gb_ref, u_ref, wq_ref, kd_ref, att_ref, eg_ref, *, c, n_chunks, group):
    row = lax.broadcasted_iota(jnp.int32, (c, c), 0)
    col = lax.broadcasted_iota(jnp.int32, (c, c), 1)
    incl = row >= col
    strict = row > col
    prefix = incl.astype(BF16)
    eye_lanes = (lax.broadcasted_iota(jnp.int32, (LANES, LANES), 0)
                 == lax.broadcasted_iota(jnp.int32, (LANES, LANES), 1)).astype(BF16)
    units = [(g, h) for g in range(group) for h in range(DN_HEADS)]
    n = range(len(units))

    def chunks(step, _):
        ci = [step * group + g for g in range(group)]
        rows = [pl.ds(pl.multiple_of(x * c, c), c) for x in ci]
        gb = [gb_ref[r, :] for r in rows]
        gcs = [_dot_exact_lhs(prefix, x) for x in gb]
        gcs_t = []
        for x in gcs:
            h3, m3, l3 = _split3(x)
            gcs_t.append(_dot_nt(eye_lanes, h3) + _dot_nt(eye_lanes, m3) + _dot_nt(eye_lanes, l3))
        q = [q_ref[rows[g], _head_cols(h)] for g, h in units]
        k = [k_ref[rows[g], _head_cols(h)] for g, h in units]
        v = [v_ref[rows[g], _head_cols(h)] for g, h in units]
        beta = [gb[g][:, h:h + 1] for g, h in units]
        gc = [gcs[g][:, DN_HEADS + h:DN_HEADS + h + 1] for g, h in units]
        g_last = [gcs[g][c - 1:c, DN_HEADS + h:DN_HEADS + h + 1] for g, h in units]
        gc_row = [gcs_t[g][DN_HEADS + h:DN_HEADS + h + 1, :] for g, h in units]
        decay = [jnp.where(incl, jnp.exp(jnp.where(incl, gc[i] - gc_row[i], 0.0)), 0.0) for i in n]
        kb = [k[i] * beta[i] for i in n]
        k16 = [k[i].astype(BF16) for i in n]
        lmat = [jnp.where(strict, _dot_nt(kb[i].astype(BF16), k16[i]) * decay[i], 0.0) for i in n]
        tmat = [t.astype(BF16) for t in _unit_lower_inverses(lmat, c)]
        egc = [jnp.exp(gc[i]) for i in n]
        for i, (g, h) in enumerate(units):
            cols = _head_cols(h)
            u_ref[rows[g], cols] = _dot(tmat[i], (v[i] * beta[i]).astype(BF16))
            w = _dot(tmat[i], (kb[i] * egc[i]).astype(BF16))
            wq_ref[ci[g], :, cols] = jnp.concatenate([w, q[i] * egc[i]], axis=0).astype(BF16)
            kd_ref[rows[g], cols] = (k[i] * jnp.exp(g_last[i] - gc[i])).astype(BF16)
            att = _dot_nt(q[i].astype(BF16), k16[i]) * decay[i]
            if c < LANES:
                att = jnp.concatenate([att, jnp.zeros((c, LANES - c), F32)], axis=1)
            att_ref[rows[g], cols] = att.astype(BF16)
            eg_ref[ci[g], h:h + 1, :] = jnp.broadcast_to(jnp.exp(g_last[i]), (1, LANES))
        return 0

    assert n_chunks % group == 0
    lax.fori_loop(0, n_chunks // group, chunks, 0)


def _delta_scan_kernel(u_ref, wq_ref, kd_ref, att_ref, eg_ref, s0_ref, o_ref, s_out_ref, s_ref, *, c, n_chunks, bt):
    l = pl.program_id(1)

    @pl.when(l == 0)
    def _():
        s_ref[...] = s0_ref[...]

    eye_lanes = (lax.broadcasted_iota(jnp.int32, (LANES, LANES), 0)
                 == lax.broadcasted_iota(jnp.int32, (LANES, LANES), 1)).astype(BF16)
    units = [(i, h) for i in range(bt) for h in range(DN_HEADS)]

    def chunk(ci, _):
        rows = pl.ds(pl.multiple_of(ci * c, c), c)
        s = [s_ref[i, h] for i, h in units]
        s16 = [x.astype(BF16) for x in s]
        r = [_dot(wq_ref[i, ci, :, _head_cols(h)], s16[n]) for n, (i, h) in enumerate(units)]
        v16 = [(u_ref[i, rows, _head_cols(h)] - r[n][0:c]).astype(BF16) for n, (i, h) in enumerate(units)]
        kd_t = [_dot_nt(eye_lanes, kd_ref[i, rows, _head_cols(h)]).astype(BF16) for i, h in units]
        for n, (i, h) in enumerate(units):
            cols = _head_cols(h)
            att = att_ref[i, rows, h * DN_DIM:h * DN_DIM + c]
            o_ref[i, rows, cols] = r[n][c:2 * c] + _dot(att, v16[n])
            s_ref[i, h] = s[n] * eg_ref[i, ci, h:h + 1, :] + _dot(kd_t[n], v16[n])
        return 0

    lax.fori_loop(0, n_chunks, chunk, 0)

    @pl.when(l == pl.num_programs(1) - 1)
    def _():
        s_out_ref[...] = s_ref[...]


def _gated_delta(q, k, v, gb, state, c):
    b, length, n = q.shape
    assert n == DN_HEADS * DN_DIM and state.shape == (b, DN_HEADS, DN_DIM, DN_DIM) and length % c == 0
    assert c % 8 == 0 and c <= LANES
    tokens = b * length
    flat = lambda a: a.reshape(tokens, a.shape[-1])
    tn = _row_tile(tokens, max(c, 256))
    row = lambda i: (i, 0)
    per_chunk = lambda i: (i, 0, 0)
    u, wq, kd, att, eg = pl.pallas_call(
        functools.partial(_delta_local_kernel, c=c, n_chunks=tn // c, group=2),
        grid=(tokens // tn,),
        in_specs=[pl.BlockSpec((tn, n), row)] * 3 + [pl.BlockSpec((tn, LANES), row)],
        out_specs=[pl.BlockSpec((tn, n), row), pl.BlockSpec((tn // c, 2 * c, n), per_chunk),
                   pl.BlockSpec((tn, n), row), pl.BlockSpec((tn, n), row),
                   pl.BlockSpec((tn // c, DN_HEADS, LANES), per_chunk)],
        out_shape=[jax.ShapeDtypeStruct((tokens, n), F32), jax.ShapeDtypeStruct((tokens // c, 2 * c, n), BF16),
                   jax.ShapeDtypeStruct((tokens, n), BF16), jax.ShapeDtypeStruct((tokens, n), BF16),
                   jax.ShapeDtypeStruct((tokens // c, DN_HEADS, LANES), F32)],
        compiler_params=_cparams("parallel"),
        name="delta_local",
    )(flat(q), flat(k), flat(v), flat(gb))

    n_seq_chunks = length // c
    tl = c * min(n_seq_chunks, 4)
    bt = 2 if n_seq_chunks > 1 else 4
    assert length % tl == 0 and b % bt == 0
    seq = lambda i, l: (i, l, 0)
    seq_chunks = lambda i, l: (i, l, 0, 0)
    per_b = lambda i, l: (i, 0, 0, 0)
    st_block = (bt,) + state.shape[1:]
    return pl.pallas_call(
        functools.partial(_delta_scan_kernel, c=c, n_chunks=tl // c, bt=bt),
        grid=(b // bt, length // tl),
        in_specs=[pl.BlockSpec((bt, tl, n), seq), pl.BlockSpec((bt, tl // c, 2 * c, n), seq_chunks),
                  pl.BlockSpec((bt, tl, n), seq), pl.BlockSpec((bt, tl, n), seq),
                  pl.BlockSpec((bt, tl // c, DN_HEADS, LANES), seq_chunks), pl.BlockSpec(st_block, per_b)],
        out_specs=[pl.BlockSpec((bt, tl, n), seq), pl.BlockSpec(st_block, per_b)],
        out_shape=[jax.ShapeDtypeStruct((b, length, n), F32), jax.ShapeDtypeStruct(state.shape, F32)],
        scratch_shapes=[pltpu.VMEM(st_block, F32)],
        compiler_params=_cparams("parallel", "arbitrary"),
        name="delta_scan",
    )(u.reshape(b, length, n), wq.reshape(b, n_seq_chunks, 2 * c, n), kd.reshape(b, length, n),
      att.reshape(b, length, n), eg.reshape(b, n_seq_chunks, DN_HEADS, LANES), state)


def _odd_out_kernel(h_ref, o_ref, z_ref, og_ref, w_ref, y_ref, gated_ref):
    for hd in range(DN_HEADS):
        cols = _head_cols(hd)
        gated_ref[:, cols] = (_rms_rows(o_ref[:, cols], og_ref[...]) * _silu(z_ref[:, cols])).astype(BF16)
    y_ref[...] = h_ref[...] + _dot(gated_ref[...], w_ref[...])


def _odd_out(h2d, o2d, z, o_g, w_out):
    m, d = h2d.shape
    n = o2d.shape[-1]
    tm = _row_tile(m, 512)
    row = lambda i: (i, 0)
    fixed = lambda i: (0, 0)
    return pl.pallas_call(
        _odd_out_kernel,
        grid=(m // tm,),
        in_specs=[pl.BlockSpec((tm, d), row), pl.BlockSpec((tm, n), row), pl.BlockSpec((tm, n), row),
                  pl.BlockSpec((1, DN_DIM), fixed), pl.BlockSpec((n, d), fixed)],
        out_specs=pl.BlockSpec((tm, d), row),
        out_shape=jax.ShapeDtypeStruct((m, d), F32),
        scratch_shapes=[pltpu.VMEM((tm, n), BF16)],
        compiler_params=_cparams("parallel"),
        name="odd_out_proj",
    )(h2d, o2d, z, o_g[None, :], w_out.astype(BF16))


def _odd_layer(h, conv_buf, state, norm_g, w_in, conv_w, a_log, dt_bias, o_g, w_out):
    b, length, d = h.shape
    h2d = h.reshape(b * length, d)
    qkv, z, ba = _odd_proj(h2d, norm_g, w_in)
    seq = lambda a: a.reshape(b, length, a.shape[-1])
    q, k, v, gb, new_buf = _conv_gates(seq(qkv), conv_buf, conv_w, seq(ba), a_log, dt_bias)
    c = DN_CHUNK if length % DN_CHUNK == 0 else length
    c_pad = -(-c // 8) * 8
    if c_pad != c:
        pad = lambda a: jnp.pad(a, ((0, 0), (0, c_pad - c), (0, 0)))
        o, new_state = _gated_delta(pad(q), pad(k), pad(v), pad(gb), state, c_pad)
        o = o[:, :length]
    else:
        o, new_state = _gated_delta(q, k, v, gb, state, c)
    y = _odd_out(h2d, o.reshape(b * length, -1), z, o_g, w_out)
    return y.reshape(b, length, d), new_buf, new_state


def kernel(x_prompt, x_sample, cache_pool, cache_k, cache_v, state_conv, state_delta, page_table, norm_even, w_in_even, w_pool, pool_scale, q_norm, k_norm, sb_bias, w_out_even, norm_odd, w_in_odd, conv_w, a_log, dt_bias, o_norm, w_out_odd):
    bp = x_prompt.shape[0]
    past_len = page_table.shape[1] * cache_k.shape[2]
    depth = norm_even.shape[0] + norm_odd.shape[0]
    hp, hs = x_prompt, x_sample
    outs = {name: [] for name in ("pool_p", "pool_s", "k_p", "v_p", "k_s", "v_s", "conv_p", "conv_s", "st_p", "st_s")}
    for layer in range(depth):
        i = layer // 2
        if layer % 2 == 0:
            ev = (norm_even[i], w_in_even[i], w_pool[i], pool_scale[i], q_norm[i], k_norm[i], sb_bias[i], w_out_even[i])
            zero_buf = jnp.zeros((bp, POOL_HIST, cache_pool.shape[-1]), F32)
            hp, pb, kn, vn = _even_layer(hp, 0, zero_buf, None, *ev)
            outs["pool_p"].append(pb), outs["k_p"].append(kn), outs["v_p"].append(vn)
            hs, pb, kn, vn = _even_layer(hs, past_len, cache_pool[i], (cache_k[i], cache_v[i], page_table), *ev)
            outs["pool_s"].append(pb), outs["k_s"].append(kn), outs["v_s"].append(vn)
        else:
            od = (norm_odd[i], w_in_odd[i], conv_w[i], a_log[i], dt_bias[i], o_norm[i], w_out_odd[i])
            zero_conv = jnp.zeros((bp, CONV_HIST, state_conv.shape[-1]), F32)
            zero_state = jnp.zeros((bp,) + state_delta.shape[2:], F32)
            hp, cb, st = _odd_layer(hp, zero_conv, zero_state, *od)
            outs["conv_p"].append(cb), outs["st_p"].append(st)
            hs, cb, st = _odd_layer(hs, state_conv[i], state_delta[i], *od)
            outs["conv_s"].append(cb), outs["st_s"].append(st)
    stack = lambda name: jnp.stack(outs[name])
    return (hp, hs, stack("pool_p"), stack("pool_s"), stack("k_p"), stack("v_p"), stack("k_s"), stack("v_s"),
            stack("conv_p"), stack("conv_s"), stack("st_p"), stack("st_s"))
```

```python
import functools
import math

import jax
import jax.numpy as jnp
from jax import lax
from jax.experimental import pallas as pl
from jax.experimental.pallas import tpu as pltpu

F32 = jnp.float32
BF16 = jnp.bfloat16

NORM_EPS = 1e-6
POOL_WINDOWS = (2, 4, 8, 16)
POOL_HIST = max(POOL_WINDOWS) - 1
POOL_PAD = 16
SB_HEADS = 8
SB_HEAD_DIM = 64
SB_BAND = 32
DN_HEADS = 8
DN_DIM = 128
CONV_WIDTH = 4
CONV_HIST = CONV_WIDTH - 1
CONV_PAD = 8
DN_CHUNK = 64
LANES = 128
MXU_DIM = 256
PAGE = 128
VMEM_LIMIT_BYTES = 48 * 1024 * 1024


def _cparams(*sem):
    return pltpu.CompilerParams(dimension_semantics=sem, vmem_limit_bytes=VMEM_LIMIT_BYTES)


def _dot(a, b):
    return jnp.dot(a, b, preferred_element_type=F32)


def _dot_nt(a, b):
    return lax.dot_general(a, b, (((1,), (1,)), ((), ())), preferred_element_type=F32)


def _dot_tn(a, b):
    return lax.dot_general(a, b, (((0,), (0,)), ((), ())), preferred_element_type=F32)


def _split2(x):
    hi = x.astype(BF16)
    return hi, (x - hi.astype(F32)).astype(BF16)


def _split3(x):
    hi = x.astype(BF16)
    r = x - hi.astype(F32)
    mid = r.astype(BF16)
    return hi, mid, (r - mid.astype(F32)).astype(BF16)


def _dot_exact_rhs(a, b_bf16):
    hi, mid, lo = _split3(a)
    return _dot(hi, b_bf16) + _dot(mid, b_bf16) + _dot(lo, b_bf16)


def _dot_exact_lhs(a_bf16, b):
    hi, mid, lo = _split3(b)
    return _dot(a_bf16, hi) + _dot(a_bf16, mid) + _dot(a_bf16, lo)


def _dot_f32(a, b):
    ah, al = _split2(a)
    bh, bl = _split2(b)
    return _dot(ah, bh) + (_dot(ah, bl) + _dot(al, bh))


def _sigmoid(x):
    return 1.0 / (1.0 + jnp.exp(-x))


def _silu(x):
    return x * _sigmoid(x)


def _softplus_neg_abs(x):
    return jnp.log1p(jnp.exp(-jnp.abs(x)))


def _rms_rows(x, g):
    return x * lax.rsqrt(jnp.mean(x * x, axis=-1, keepdims=True) + NORM_EPS) * g


def _row_tile(m, want):
    t = min(m, want)
    assert m % t == 0, (m, t)
    return t


def _even_proj_kernel(x_ref, g_ref, w_ref, seg_ref, qg_ref, kg_ref,
                      u_ref, q_ref, k_ref, v_ref, kb_ref, vb_ref, gate_ref, *, feature_major):
    xn = _rms_rows(x_ref[...], g_ref[...]).astype(BF16)
    pw = u_ref.shape[-1]

    def proj(c0, c1):
        return _dot(xn, w_ref[:, c0:c1])

    def head_norm(y, gain):
        ms = _dot((y * y).astype(BF16), seg_ref[...]) * (1.0 / SB_HEAD_DIM)
        return y * lax.rsqrt(ms + NORM_EPS) * gain

    u_ref[...] = proj(0, pw)
    q = head_norm(proj(pw, 2 * pw), qg_ref[...])
    q_ref[...] = (q * (SB_HEAD_DIM ** -0.5)).astype(BF16)
    k = head_norm(proj(2 * pw, 3 * pw), kg_ref[...])
    kb_ref[...] = k.astype(BF16)
    v = proj(3 * pw, 4 * pw)
    vb_ref[...] = v.astype(BF16)
    if feature_major:
        k_ref[0] = k.T
        v_ref[0] = v.T
    else:
        k_ref[...] = k
        v_ref[...] = v
    gate_ref[...] = proj(4 * pw, 6 * pw)


def _even_proj(x2d, norm_g, w_in, q_g, k_g, seq_len):
    m, d = x2d.shape
    pw = SB_HEADS * SB_HEAD_DIM
    assert w_in.shape == (d, 6 * pw)
    tm = _row_tile(m, 512)
    feature_major = seq_len % tm == 0
    lane_head = jnp.arange(pw) // SB_HEAD_DIM
    seg = (lane_head[:, None] == lane_head[None, :]).astype(BF16)
    row = lambda i: (i, 0)
    fixed = lambda i: (0, 0)
    outs = [((m, pw), F32), ((m, pw), BF16), ((m, pw), F32), ((m, pw), F32), ((m, pw), BF16), ((m, pw), BF16),
            ((m, 2 * pw), F32)]
    out_specs = [pl.BlockSpec((tm, s[-1]), row) for s, _ in outs]
    out_shape = [jax.ShapeDtypeStruct(s, t) for s, t in outs]
    if feature_major:
        tiles = seq_len // tm
        for i in (2, 3):
            out_specs[i] = pl.BlockSpec((1, pw, tm), lambda j: (j // tiles, 0, j % tiles))
            out_shape[i] = jax.ShapeDtypeStruct((m // seq_len, pw, seq_len), F32)
    return pl.pallas_call(
        functools.partial(_even_proj_kernel, feature_major=feature_major),
        grid=(m // tm,),
        in_specs=[pl.BlockSpec((tm, d), row), pl.BlockSpec((1, d), fixed), pl.BlockSpec((d, 6 * pw), fixed),
                  pl.BlockSpec((pw, pw), fixed), pl.BlockSpec((1, pw), fixed), pl.BlockSpec((1, pw), fixed)],
        out_specs=out_specs,
        out_shape=out_shape,
        compiler_params=_cparams("parallel"),
        name="even_in_proj",
    )(x2d, norm_g[None, :], w_in.astype(BF16), seg,
      jnp.tile(q_g, SB_HEADS)[None, :], jnp.tile(k_g, SB_HEADS)[None, :])


def _seqs_per_step(b, length):
    bt = 8 if length <= 8 else 1
    assert b % bt == 0
    return bt


def _pool_kernel(u_ref, buf_ref, wp_ref, scale_ref, y_ref, nb_ref, ext_ref, *, start, tl, bt):
    l = pl.program_id(1)
    lo = POOL_PAD - POOL_HIST
    pos = start + l * tl + lax.broadcasted_iota(jnp.int32, (tl, 1), 0)
    for s in range(bt):
        ext = ext_ref.at[s]

        @pl.when(l == 0)
        def _():
            ext[0:lo, :] = jnp.zeros((lo, ext.shape[-1]), F32)
            ext[lo:POOL_PAD, :] = buf_ref[s]

        @pl.when(l > 0)
        def _():
            ext[lo:POOL_PAD, :] = ext[tl + lo:tl + POOL_PAD, :]

        ext[POOL_PAD:POOL_PAD + tl, :] = u_ref[s]
        for gi, w in enumerate(POOL_WINDOWS):
            cols = slice(gi * LANES, (gi + 1) * LANES)
            cur = ext[POOL_PAD:POOL_PAD + tl, cols]
            win = cur
            for i in range(1, w):
                win = win + ext[POOL_PAD - i:POOL_PAD - i + tl, cols]
            cnt = jnp.minimum(pos + 1, w).astype(F32)
            d = win / cnt - cur
            y_ref[s, :, cols] = _dot(d.astype(BF16), wp_ref[gi]) * scale_ref[:, cols]

        @pl.when(l == pl.num_programs(1) - 1)
        def _():
            nb_ref[s] = ext[tl + lo:tl + POOL_PAD, :]


def _pool_mix(u, buf, start, w_pool, scale):
    b, length, p = u.shape
    assert p == len(POOL_WINDOWS) * LANES and buf.shape == (b, POOL_HIST, p)
    tl = _row_tile(length, 512)
    assert tl == length or tl >= POOL_PAD
    bt = _seqs_per_step(b, length)
    seq = lambda i, l: (i, l, 0)
    per_b = lambda i, l: (i, 0, 0)
    return pl.pallas_call(
        functools.partial(_pool_kernel, start=start, tl=tl, bt=bt),
        grid=(b // bt, length // tl),
        in_specs=[pl.BlockSpec((bt, tl, p), seq), pl.BlockSpec((bt, POOL_HIST, p), per_b),
                  pl.BlockSpec(w_pool.shape, lambda i, l: (0, 0, 0)), pl.BlockSpec((1, p), lambda i, l: (0, 0))],
        out_specs=[pl.BlockSpec((bt, tl, p), seq), pl.BlockSpec((bt, POOL_HIST, p), per_b)],
        out_shape=[jax.ShapeDtypeStruct((b, length, p), F32), jax.ShapeDtypeStruct((b, POOL_HIST, p), F32)],
        scratch_shapes=[pltpu.VMEM((bt, POOL_PAD + tl, p), F32)],
        compiler_params=_cparams("parallel", "arbitrary"),
        name="pool_mix",
    )(u, buf, w_pool.astype(BF16), scale[None, :])


def _log_beta_keep(z):
    m = jnp.minimum(z, 0.0)
    d = m - z
    s = jnp.log(1.0 + jnp.exp(m + d))
    return m - s, d - s


def _stick_break_tiles(zs, carries, tri, masks, chained=False):
    n = range(len(zs))
    log_beta, log_keep = zip(*[_log_beta_keep(z) for z in zs])
    log_keep = [lk if m is None else jnp.where(m, lk, 0.0) for lk, m in zip(log_keep, masks)]
    totals = [jnp.sum(lk, axis=-1, keepdims=True) for lk in log_keep]
    if chained:
        carry_in = [carries[0]]
        for i in n:
            carry_in.append(carry_in[i] + totals[i])
        new_carries = carry_in[-1:]
    else:
        carry_in = carries
        new_carries = [carries[i] + totals[i] for i in n]
    suffix = [_dot(lk.astype(BF16), tri) + carry_in[i] for i, lk in enumerate(log_keep)]
    a = [jnp.exp(log_beta[i] + suffix[i]) for i in n]
    a = [x if m is None else jnp.where(m, x, 0.0) for x, m in zip(a, masks)]
    return [x.astype(BF16) for x in a], new_carries


def _later_key_matrix(n):
    return (lax.broadcasted_iota(jnp.int32, (n, n), 0) > lax.broadcasted_iota(jnp.int32, (n, n), 1)).astype(BF16)


def _sb_prompt_kernel(bias_ref, q_ref, k_ref, v_ref, o_ref, acc_ref, carry_ref, tot_ref, lb_ref, lk_ref, a_ref, *, t):
    qi = pl.program_id(1)
    heads = range(SB_HEADS)
    low_lanes = lax.broadcasted_iota(jnp.int32, (t, LANES), 1) < SB_HEAD_DIM
    tri = _later_key_matrix(t)
    causal = lax.broadcasted_iota(jnp.int32, (t, t), 1) < lax.broadcasted_iota(jnp.int32, (t, t), 0)
    pair_cols = lambda h: slice((h // 2) * LANES, (h // 2 + 1) * LANES)
    qh = []
    for h in heads:
        q_pair = q_ref[0, :, pair_cols(h)]
        qh.append(jnp.where(low_lanes if h % 2 == 0 else jnp.logical_not(low_lanes), q_pair, jnp.zeros_like(q_pair)))

    bands = [slice(r, r + SB_BAND) for r in range(0, t, SB_BAND)]

    def tiles(kb, mask):
        rows = pl.ds(pl.multiple_of(kb * t, t), t)
        for h in heads:
            z = _dot_nt(qh[h], k_ref[0, rows, pair_cols(h)]) + bias_ref[h]
            for band in bands:
                log_beta, log_keep = _log_beta_keep(z[band])
                if mask is not None:
                    log_keep = jnp.where(mask[band], log_keep, 0.0)
                lb_ref[h, band, :] = log_beta
                lk_ref[h, band, :] = log_keep.astype(BF16)
                tot_ref[h, band, :] = jnp.sum(log_keep, axis=-1, keepdims=True)
        for h in heads:
            suffix = _dot(lk_ref[h], tri)
            for band in bands:
                carry = carry_ref[h, band, :]
                a = jnp.exp(lb_ref[h, band, :] + (suffix[band] + carry))
                if mask is not None:
                    a = jnp.where(mask[band], a, 0.0)
                a_ref[h, band, :] = a.astype(BF16)
                carry_ref[h, band, :] = carry + tot_ref[h, band, :]
        for h in heads:
            acc_ref[h] += _dot(a_ref[h], v_ref[0, rows, pair_cols(h)])

    acc_ref[...] = jnp.zeros(acc_ref.shape, F32)
    carry_ref[...] = jnp.zeros(carry_ref.shape, F32)
    tiles(qi, causal)

    def earlier(j, _):
        tiles(qi - 1 - j, None)
        return 0

    lax.fori_loop(0, qi, earlier, 0)
    for h in range(0, SB_HEADS, 2):
        o_ref[0, :, pair_cols(h)] = jnp.where(low_lanes, acc_ref[h], acc_ref[h + 1])


def _sb_prompt(q, k, v, bias):
    b, length, width = q.shape
    assert width == SB_HEADS * SB_HEAD_DIM and 2 * SB_HEAD_DIM == LANES
    t = _row_tile(length, 256)
    q_blk = lambda i, j: (i, j, 0)
    kv_blk = lambda i, j: (i, 0, 0)
    return pl.pallas_call(
        functools.partial(_sb_prompt_kernel, t=t),
        grid=(b, length // t),
        in_specs=[pl.BlockSpec(memory_space=pltpu.SMEM), pl.BlockSpec((1, t, width), q_blk),
                  pl.BlockSpec((1, length, width), kv_blk), pl.BlockSpec((1, length, width), kv_blk)],
        out_specs=pl.BlockSpec((1, t, width), q_blk),
        scratch_shapes=[pltpu.VMEM((SB_HEADS, t, LANES), F32), pltpu.VMEM((SB_HEADS, t, 1), F32),
                        pltpu.VMEM((SB_HEADS, t, 1), F32), pltpu.VMEM((SB_HEADS, t, t), F32),
                        pltpu.VMEM((SB_HEADS, t, t), BF16), pltpu.VMEM((SB_HEADS, t, t), BF16)],
        out_shape=jax.ShapeDtypeStruct((b, length, width), F32),
        compiler_params=_cparams("parallel", "arbitrary"),
        name="stick_break_prompt",
    )(bias, q, k, v)


def _sb_sample_kernel(pt_ref, q_ref, bias_ref, kn_ref, vn_ref, *refs, n_pages, n_new):
    k_refs, v_refs, o_ref = refs[:n_pages], refs[n_pages:2 * n_pages], refs[2 * n_pages]
    kpad_ref, vpad_ref = refs[2 * n_pages + 1:]
    rows = SB_HEADS * n_new
    width = SB_HEADS * SB_HEAD_DIM
    row_i = lax.broadcasted_iota(jnp.int32, (rows, width), 0)
    lane_i = lax.broadcasted_iota(jnp.int32, (rows, width), 1)
    own_head = (row_i // n_new) == (lane_i // SB_HEAD_DIM)
    qbd = jnp.where(own_head, q_ref[0], jnp.zeros_like(q_ref[0]))
    bias = bias_ref[...]
    tri = _later_key_matrix(PAGE)

    kpad_ref[...] = jnp.zeros(kpad_ref.shape, BF16)
    vpad_ref[...] = jnp.zeros(vpad_ref.shape, BF16)
    kpad_ref[0:kn_ref.shape[1], :] = kn_ref[0].astype(BF16)
    vpad_ref[0:vn_ref.shape[1], :] = vn_ref[0].astype(BF16)
    qpos = lax.broadcasted_iota(jnp.int32, (rows, PAGE), 0) % n_new
    kpos = lax.broadcasted_iota(jnp.int32, (rows, PAGE), 1)

    pages = list(reversed(range(n_pages)))
    zs = [_dot_nt(qbd, kpad_ref[...]) + bias] + [_dot(qbd, k_refs[p][0].astype(BF16)) + bias for p in pages]
    a, _ = _stick_break_tiles(zs, [jnp.zeros((rows, 1), F32)], tri, [kpos < qpos] + [None] * n_pages, chained=True)
    out = _dot(a[0], vpad_ref[...])
    for i, p in enumerate(pages):
        out = out + _dot_nt(a[i + 1], v_refs[p][0].astype(BF16))
    out = jnp.where(own_head, out, 0.0)
    acc = out[0:n_new]
    for h in range(1, SB_HEADS):
        acc = acc + out[h * n_new:(h + 1) * n_new]
    o_ref[0] = acc


def _sb_sample(q, k_new, v_new, bias, cache_k, cache_v, page_table):
    b, n_new, width = q.shape
    n_pages = page_table.shape[1]
    n_phys = cache_k.shape[0]
    assert cache_k.shape[1:] == (PAGE, SB_HEADS, SB_HEAD_DIM) and n_new <= 8
    ck = jnp.transpose(cache_k, (0, 2, 3, 1)).reshape(n_phys, width, PAGE)
    cv = jnp.transpose(cache_v, (0, 2, 3, 1)).reshape(n_phys, width, PAGE)
    rows = SB_HEADS * n_new
    pad = ((0, 0), (0, 8 - n_new), (0, 0))
    per_b = lambda i, pt: (i, 0, 0)
    page_specs = [pl.BlockSpec((1, width, PAGE), functools.partial(lambda i, pt, p: (pt[i, p], 0, 0), p=p))
                  for p in range(n_pages)]
    return pl.pallas_call(
        functools.partial(_sb_sample_kernel, n_pages=n_pages, n_new=n_new),
        grid_spec=pltpu.PrefetchScalarGridSpec(
            num_scalar_prefetch=1,
            grid=(b,),
            in_specs=[pl.BlockSpec((1, rows, width), per_b), pl.BlockSpec((rows, 1), lambda i, pt: (0, 0)),
                      pl.BlockSpec((1, 8, width), per_b), pl.BlockSpec((1, 8, width), per_b)] + page_specs + page_specs,
            out_specs=pl.BlockSpec((1, n_new, width), per_b),
            scratch_shapes=[pltpu.VMEM((PAGE, width), BF16), pltpu.VMEM((PAGE, width), BF16)]),
        out_shape=jax.ShapeDtypeStruct((b, n_new, width), F32),
        compiler_params=_cparams("parallel"),
        name="stick_break_sample",
    )(page_table, jnp.tile(q, (1, SB_HEADS, 1)), jnp.repeat(bias, n_new)[:, None],
      jnp.pad(k_new, pad), jnp.pad(v_new, pad), *([ck] * n_pages), *([cv] * n_pages))


def _even_out_kernel(h_ref, py_ref, sb_ref, gate_ref, w_ref, y_ref):
    pw = py_ref.shape[-1]
    m_pool = (py_ref[...] * _silu(gate_ref[:, 0:pw])).astype(BF16)
    m_sb = (sb_ref[...] * _silu(gate_ref[:, pw:2 * pw])).astype(BF16)
    y_ref[...] = h_ref[...] + (_dot(m_pool, w_ref[0:pw, :]) + _dot(m_sb, w_ref[pw:2 * pw, :]))


def _even_out(h2d, pool_y, sb, gate, w_out):
    m, d = h2d.shape
    pw = pool_y.shape[-1]
    tm = _row_tile(m, 512)
    row = lambda i: (i, 0)
    return pl.pallas_call(
        _even_out_kernel,
        grid=(m // tm,),
        in_specs=[pl.BlockSpec((tm, d), row), pl.BlockSpec((tm, pw), row), pl.BlockSpec((tm, pw), row),
                  pl.BlockSpec((tm, 2 * pw), row), pl.BlockSpec((2 * pw, d), lambda i: (0, 0))],
        out_specs=pl.BlockSpec((tm, d), row),
        out_shape=jax.ShapeDtypeStruct((m, d), F32),
        compiler_params=_cparams("parallel"),
        name="even_out_proj",
    )(h2d, pool_y, sb, gate, w_out.astype(BF16))


def _even_layer(h, start, pool_buf, past, norm_g, w_in, w_pool, pool_scale, q_g, k_g, sb_bias, w_out):
    b, length, d = h.shape
    h2d = h.reshape(b * length, d)
    u, q, k, v, kb, vb, gate = _even_proj(h2d, norm_g, w_in, q_g, k_g, length)
    seq = lambda a: a.reshape(b, length, a.shape[-1])
    pool_y, new_buf = _pool_mix(seq(u), pool_buf, start, w_pool, pool_scale)
    if past is None:
        sb = _sb_prompt(seq(q), seq(kb), seq(vb), sb_bias)
    else:
        sb = _sb_sample(seq(q), seq(k), seq(v), sb_bias, *past)
    y = _even_out(h2d, pool_y.reshape(b * length, -1), sb.reshape(b * length, -1), gate, w_out)
    if k.ndim == 3:
        heads = lambda a: jnp.transpose(a.reshape(b, SB_HEADS, SB_HEAD_DIM, length), (0, 3, 1, 2))
    else:
        heads = lambda a: a.reshape(b, length, SB_HEADS, SB_HEAD_DIM)
    return y.reshape(b, length, d), new_buf, heads(k), heads(v)


def _odd_proj_kernel(x_ref, g_ref, w_ref, wba_ref, qkv_ref, z_ref, ba_ref, *, chunk):
    xn = _rms_rows(x_ref[...], g_ref[...]).astype(BF16)
    n_qkv = qkv_ref.shape[-1]
    for c0 in range(0, n_qkv, chunk):
        qkv_ref[:, c0:c0 + chunk] = _dot(xn, w_ref[:, c0:c0 + chunk])
    for c0 in range(0, z_ref.shape[-1], chunk):
        z_ref[:, c0:c0 + chunk] = _dot(xn, w_ref[:, n_qkv + c0:n_qkv + c0 + chunk])
    ba_ref[...] = _dot(xn, wba_ref[...])


def _odd_proj(x2d, norm_g, w_in):
    m, d = x2d.shape
    n_qkv, n_z = 3 * DN_HEADS * DN_DIM, DN_HEADS * DN_DIM
    assert w_in.shape == (d, n_qkv + n_z + 2 * DN_HEADS)
    tm = _row_tile(m, 256)
    w_t = w_in.T
    w_main = w_t[:n_qkv + n_z].astype(BF16).T
    w_ba = jnp.pad(w_t[n_qkv + n_z:], ((0, LANES - 2 * DN_HEADS), (0, 0))).astype(BF16).T
    row = lambda i: (i, 0)
    fixed = lambda i: (0, 0)
    return pl.pallas_call(
        functools.partial(_odd_proj_kernel, chunk=512),
        grid=(m // tm,),
        in_specs=[pl.BlockSpec((tm, d), row), pl.BlockSpec((1, d), fixed), pl.BlockSpec(w_main.shape, fixed),
                  pl.BlockSpec(w_ba.shape, fixed)],
        out_specs=[pl.BlockSpec((tm, n_qkv), row), pl.BlockSpec((tm, n_z), row), pl.BlockSpec((tm, LANES), row)],
        out_shape=[jax.ShapeDtypeStruct((m, n_qkv), F32), jax.ShapeDtypeStruct((m, n_z), F32),
                   jax.ShapeDtypeStruct((m, LANES), F32)],
        compiler_params=_cparams("parallel"),
        name="odd_in_proj",
    )(x2d, norm_g[None, :], w_main, w_ba)


def _conv_kernel(x_ref, buf_ref, cw_ref, ba_ref, ab_ref, q_ref, k_ref, v_ref, gb_ref, nb_ref, ext_ref, *, tl, bt):
    l = pl.program_id(1)
    lo = CONV_PAD - CONV_HIST
    width = ext_ref.shape[-1]
    n_qk = DN_HEADS * DN_DIM
    for s in range(bt):
        ext = ext_ref.at[s]

        @pl.when(l == 0)
        def _():
            ext[0:lo, :] = jnp.zeros((lo, width), F32)
            ext[lo:CONV_PAD, :] = buf_ref[s]

        @pl.when(l > 0)
        def _():
            ext[lo:CONV_PAD, :] = ext[tl + lo:tl + CONV_PAD, :]

        ext[CONV_PAD:CONV_PAD + tl, :] = x_ref[s]
        for c in range(width // LANES):
            cols = slice(c * LANES, (c + 1) * LANES)
            conv = ext[lo:lo + tl, cols] * cw_ref[0:1, cols]
            for j in range(1, CONV_WIDTH):
                conv = conv + ext[lo + j:lo + j + tl, cols] * cw_ref[j:j + 1, cols]
            y = _silu(conv)
            oc = slice((c % DN_HEADS) * LANES, (c % DN_HEADS + 1) * LANES)
            if c * LANES < 2 * n_qk:
                y = y * lax.rsqrt(jnp.sum(y * y, axis=-1, keepdims=True) + NORM_EPS)
                if c * LANES < n_qk:
                    q_ref[s, :, oc] = y * (DN_DIM ** -0.5)
                else:
                    k_ref[s, :, oc] = y
            else:
                v_ref[s, :, oc] = y

        ba = ba_ref[s]
        x = ba + ab_ref[1:2, :]
        g = -jnp.exp(ab_ref[0:1, :]) * (jnp.maximum(x, 0.0) + _softplus_neg_abs(x))
        is_beta = lax.broadcasted_iota(jnp.int32, ba.shape, 1) < DN_HEADS
        gb_ref[s] = jnp.where(is_beta, _sigmoid(ba), g)

        @pl.when(l == pl.num_programs(1) - 1)
        def _():
            nb_ref[s] = ext[tl + lo:tl + CONV_PAD, :]


def _conv_gates(qkv, conv_buf, conv_w, ba, a_log, dt_bias):
    b, length, width = qkv.shape
    n = DN_HEADS * DN_DIM
    assert width == 3 * n and conv_buf.shape == (b, CONV_HIST, width)
    tl = _row_tile(length, 256)
    assert tl == length or tl >= CONV_PAD
    park = lambda vec: jnp.pad(vec, (DN_HEADS, LANES - 2 * DN_HEADS))
    ab = jnp.stack([park(a_log), park(dt_bias)])
    bt = _seqs_per_step(b, length)
    seq = lambda i, l: (i, l, 0)
    per_b = lambda i, l: (i, 0, 0)
    fixed = lambda i, l: (0, 0)
    return pl.pallas_call(
        functools.partial(_conv_kernel, tl=tl, bt=bt),
        grid=(b // bt, length // tl),
        in_specs=[pl.BlockSpec((bt, tl, width), seq), pl.BlockSpec((bt, CONV_HIST, width), per_b),
                  pl.BlockSpec((CONV_WIDTH, width), fixed), pl.BlockSpec((bt, tl, LANES), seq),
                  pl.BlockSpec((2, LANES), fixed)],
        out_specs=[pl.BlockSpec((bt, tl, n), seq)] * 3 + [pl.BlockSpec((bt, tl, LANES), seq),
                                                           pl.BlockSpec((bt, CONV_HIST, width), per_b)],
        out_shape=[jax.ShapeDtypeStruct((b, length, n), F32)] * 3
        + [jax.ShapeDtypeStruct((b, length, LANES), F32), jax.ShapeDtypeStruct((b, CONV_HIST, width), F32)],
        scratch_shapes=[pltpu.VMEM((bt, CONV_PAD + tl, width), F32)],
        compiler_params=_cparams("parallel", "arbitrary"),
        name="conv_norm_gates",
    )(qkv, conv_buf, conv_w, ba, ab)


def _unit_lower_inverses(lmats, c):
    eye = (lax.broadcasted_iota(jnp.int32, (c, c), 0) == lax.broadcasted_iota(jnp.int32, (c, c), 1)).astype(F32)
    powers = [-lm for lm in lmats]
    invs = [eye + p for p in powers]
    span = 2
    while span < c:
        powers = [_dot_f32(p, p) for p in powers]
        invs = [i + _dot_f32(i, p) for i, p in zip(invs, powers)]
        span *= 2
    return invs


def _head_cols(h):
    return slice(h * DN_DIM, (h + 1) * DN_DIM)


def _delta_local_kernel(q_ref, k_ref, v_ref, gb_ref, u_ref, wq_ref, kd_ref, att_ref, eg_ref, *, c, n_chunks, group):
    row = lax.broadcasted_iota(jnp.int32, (c, c), 0)
    col = lax.broadcasted_iota(jnp.int32, (c, c), 1)
    incl = row >= col
    strict = row > col
    prefix = incl.astype(BF16)
    eye_lanes = (lax.broadcasted_iota(jnp.int32, (LANES, LANES), 0)
                 == lax.broadcasted_iota(jnp.int32, (LANES, LANES), 1)).astype(BF16)
    units = [(g, h) for g in range(group) for h in range(DN_HEADS)]
    n = range(len(units))

    def chunks(step, _):
        ci = [step * group + g for g in range(group)]
        rows = [pl.ds(pl.multiple_of(x * c, c), c) for x in ci]
        gb = [gb_ref[r, :] for r in rows]
        gcs = [_dot_exact_lhs(prefix, x) for x in gb]
        gcs_t = []
        for x in gcs:
            h3, m3, l3 = _split3(x)
            gcs_t.append(_dot_nt(eye_lanes, h3) + _dot_nt(eye_lanes, m3) + _dot_nt(eye_lanes, l3))
        q = [q_ref[rows[g], _head_cols(h)] for g, h in units]
        k = [k_ref[rows[g], _head_cols(h)] for g, h in units]
        v = [v_ref[rows[g], _head_cols(h)] for g, h in units]
        beta = [gb[g][:, h:h + 1] for g, h in units]
        gc = [gcs[g][:, DN_HEADS + h:DN_HEADS + h + 1] for g, h in units]
        g_last = [gcs[g][c - 1:c, DN_HEADS + h:DN_HEADS + h + 1] for g, h in units]
        gc_row = [gcs_t[g][DN_HEADS + h:DN_HEADS + h + 1, :] for g, h in units]
        decay = [jnp.where(incl, jnp.exp(jnp.where(incl, gc[i] - gc_row[i], 0.0)), 0.0) for i in n]
        kb = [k[i] * beta[i] for i in n]
        k16 = [k[i].astype(BF16) for i in n]
        lmat = [jnp.where(strict, _dot_nt(kb[i].astype(BF16), k16[i]) * decay[i], 0.0) for i in n]
        tmat = [t.astype(BF16) for t in _unit_lower_inverses(lmat, c)]
        egc = [jnp.exp(gc[i]) for i in n]
        for i, (g, h) in enumerate(units):
            cols = _head_cols(h)
            u_ref[rows[g], cols] = _dot(tmat[i], (v[i] * beta[i]).astype(BF16))
            w = _dot(tmat[i], (kb[i] * egc[i]).astype(BF16))
            wq_ref[ci[g], :, cols] = jnp.concatenate([w, q[i] * egc[i]], axis=0).astype(BF16)
            kd_ref[rows[g], cols] = (k[i] * jnp.exp(g_last[i] - gc[i])).astype(BF16)
            att = _dot_nt(q[i].astype(BF16), k16[i]) * decay[i]
            if c < LANES:
                att = jnp.concatenate([att, jnp.zeros((c, LANES - c), F32)], axis=1)
            att_ref[rows[g], cols] = att.astype(BF16)
            eg_ref[ci[g], h:h + 1, :] = jnp.broadcast_to(jnp.exp(g_last[i]), (1, LANES))
        return 0

    assert n_chunks % group == 0
    lax.fori_loop(0, n_chunks // group, chunks, 0)


def _delta_scan_kernel(u_ref, wq_ref, kd_ref, att_ref, eg_ref, s0_ref, o_ref, s_out_ref, s_ref, *, c, n_chunks, bt):
    l = pl.program_id(1)

    @pl.when(l == 0)
    def _():
        s_ref[...] = s0_ref[...]

    eye_lanes = (lax.broadcasted_iota(jnp.int32, (LANES, LANES), 0)
                 == lax.broadcasted_iota(jnp.int32, (LANES, LANES), 1)).astype(BF16)
    units = [(i, h) for i in range(bt) for h in range(DN_HEADS)]

    def chunk(ci, _):
        rows = pl.ds(pl.multiple_of(ci * c, c), c)
        s = [s_ref[i, h] for i, h in units]
        s16 = [x.astype(BF16) for x in s]
        r = [_dot(wq_ref[i, ci, :, _head_cols(h)], s16[n]) for n, (i, h) in enumerate(units)]
        v16 = [(u_ref[i, rows, _head_cols(h)] - r[n][0:c]).astype(BF16) for n, (i, h) in enumerate(units)]
        for n, (i, h) in enumerate(units):
            cols = _head_cols(h)
            att = att_ref[i, rows, h * DN_DIM:h * DN_DIM + c]
            o_ref[i, rows, cols] = r[n][c:2 * c] + _dot(att, v16[n])
            s_ref[i, h] = s[n] * eg_ref[i, ci, h:h + 1, :] + _dot_tn(kd_ref[i, rows, cols], v16[n])
        return 0

    lax.fori_loop(0, n_chunks, chunk, 0)

    @pl.when(l == pl.num_programs(1) - 1)
    def _():
        s_out_ref[...] = s_ref[...]


def _gated_delta(q, k, v, gb, state, c):
    b, length, n = q.shape
    assert n == DN_HEADS * DN_DIM and state.shape == (b, DN_HEADS, DN_DIM, DN_DIM) and length % c == 0
    assert c % 8 == 0 and c <= LANES
    tokens = b * length
    flat = lambda a: a.reshape(tokens, a.shape[-1])
    tn = _row_tile(tokens, max(c, 256))
    row = lambda i: (i, 0)
    per_chunk = lambda i: (i, 0, 0)
    u, wq, kd, att, eg = pl.pallas_call(
        functools.partial(_delta_local_kernel, c=c, n_chunks=tn // c, group=2),
        grid=(tokens // tn,),
        in_specs=[pl.BlockSpec((tn, n), row)] * 3 + [pl.BlockSpec((tn, LANES), row)],
        out_specs=[pl.BlockSpec((tn, n), row), pl.BlockSpec((tn // c, 2 * c, n), per_chunk),
                   pl.BlockSpec((tn, n), row), pl.BlockSpec((tn, n), row),
                   pl.BlockSpec((tn // c, DN_HEADS, LANES), per_chunk)],
        out_shape=[jax.ShapeDtypeStruct((tokens, n), F32), jax.ShapeDtypeStruct((tokens // c, 2 * c, n), BF16),
                   jax.ShapeDtypeStruct((tokens, n), BF16), jax.ShapeDtypeStruct((tokens, n), BF16),
                   jax.ShapeDtypeStruct((tokens // c, DN_HEADS, LANES), F32)],
        compiler_params=_cparams("parallel"),
        name="delta_local",
    )(flat(q), flat(k), flat(v), flat(gb))

    n_seq_chunks = length // c
    tl = c * min(n_seq_chunks, 4)
    bt = 2 if n_seq_chunks > 1 else 4
    assert length % tl == 0 and b % bt == 0
    seq = lambda i, l: (i, l, 0)
    seq_chunks = lambda i, l: (i, l, 0, 0)
    per_b = lambda i, l: (i, 0, 0, 0)
    st_block = (bt,) + state.shape[1:]
    return pl.pallas_call(
        functools.partial(_delta_scan_kernel, c=c, n_chunks=tl // c, bt=bt),
        grid=(b // bt, length // tl),
        in_specs=[pl.BlockSpec((bt, tl, n), seq), pl.BlockSpec((bt, tl // c, 2 * c, n), seq_chunks),
                  pl.BlockSpec((bt, tl, n), seq), pl.BlockSpec((bt, tl, n), seq),
                  pl.BlockSpec((bt, tl // c, DN_HEADS, LANES), seq_chunks), pl.BlockSpec(st_block, per_b)],
        out_specs=[pl.BlockSpec((bt, tl, n), seq), pl.BlockSpec(st_block, per_b)],
        out_shape=[jax.ShapeDtypeStruct((b, length, n), F32), jax.ShapeDtypeStruct(state.shape, F32)],
        scratch_shapes=[pltpu.VMEM(st_block, F32)],
        compiler_params=_cparams("parallel", "arbitrary"),
        name="delta_scan",
    )(u.reshape(b, length, n), wq.reshape(b, n_seq_chunks, 2 * c, n), kd.reshape(b, length, n),
      att.reshape(b, length, n), eg.reshape(b, n_seq_chunks, DN_HEADS, LANES), state)


def _odd_out_kernel(h_ref, o_ref, z_ref, og_ref, w_ref, y_ref, gated_ref):
    for hd in range(DN_HEADS):
        cols = _head_cols(hd)
        gated_ref[:, cols] = (_rms_rows(o_ref[:, cols], og_ref[...]) * _silu(z_ref[:, cols])).astype(BF16)
    y_ref[...] = h_ref[...] + _dot(gated_ref[...], w_ref[...])


def _odd_out(h2d, o2d, z, o_g, w_out):
    m, d = h2d.shape
    n = o2d.shape[-1]
    tm = _row_tile(m, 512)
    row = lambda i: (i, 0)
    fixed = lambda i: (0, 0)
    return pl.pallas_call(
        _odd_out_kernel,
        grid=(m // tm,),
        in_specs=[pl.BlockSpec((tm, d), row), pl.BlockSpec((tm, n), row), pl.BlockSpec((tm, n), row),
                  pl.BlockSpec((1, DN_DIM), fixed), pl.BlockSpec((n, d), fixed)],
        out_specs=pl.BlockSpec((tm, d), row),
        out_shape=jax.ShapeDtypeStruct((m, d), F32),
        scratch_shapes=[pltpu.VMEM((tm, n), BF16)],
        compiler_params=_cparams("parallel"),
        name="odd_out_proj",
    )(h2d, o2d, z, o_g[None, :], w_out.astype(BF16))


def _odd_layer(h, conv_buf, state, norm_g, w_in, conv_w, a_log, dt_bias, o_g, w_out):
    b, length, d = h.shape
    h2d = h.reshape(b * length, d)
    qkv, z, ba = _odd_proj(h2d, norm_g, w_in)
    seq = lambda a: a.reshape(b, length, a.shape[-1])
    q, k, v, gb, new_buf = _conv_gates(seq(qkv), conv_buf, conv_w, seq(ba), a_log, dt_bias)
    c = DN_CHUNK if length % DN_CHUNK == 0 else length
    c_pad = -(-c // 8) * 8
    if c_pad != c:
        pad = lambda a: jnp.pad(a, ((0, 0), (0, c_pad - c), (0, 0)))
        o, new_state = _gated_delta(pad(q), pad(k), pad(v), pad(gb), state, c_pad)
        o = o[:, :length]
    else:
        o, new_state = _gated_delta(q, k, v, gb, state, c)
    y = _odd_out(h2d, o.reshape(b * length, -1), z, o_g, w_out)
    return y.reshape(b, length, d), new_buf, new_state


def kernel(x_prompt, x_sample, cache_pool, cache_k, cache_v, state_conv, state_delta, page_table, norm_even, w_in_even, w_pool, pool_scale, q_norm, k_norm, sb_bias, w_out_even, norm_odd, w_in_odd, conv_w, a_log, dt_bias, o_norm, w_out_odd):
    bp = x_prompt.shape[0]
    past_len = page_table.shape[1] * cache_k.shape[2]
    depth = norm_even.shape[0] + norm_odd.shape[0]
    hp, hs = x_prompt, x_sample
    outs = {name: [] for name in ("pool_p", "pool_s", "k_p", "v_p", "k_s", "v_s", "conv_p", "conv_s", "st_p", "st_s")}
    for layer in range(depth):
        i = layer // 2
        if layer % 2 == 0:
            ev = (norm_even[i], w_in_even[i], w_pool[i], pool_scale[i], q_norm[i], k_norm[i], sb_bias[i], w_out_even[i])
            zero_buf = jnp.zeros((bp, POOL_HIST, cache_pool.shape[-1]), F32)
            hp, pb, kn, vn = _even_layer(hp, 0, zero_buf, None, *ev)
            outs["pool_p"].append(pb), outs["k_p"].append(kn), outs["v_p"].append(vn)
            hs, pb, kn, vn = _even_layer(hs, past_len, cache_pool[i], (cache_k[i], cache_v[i], page_table), *ev)
            outs["pool_s"].append(pb), outs["k_s"].append(kn), outs["v_s"].append(vn)
        else:
            od = (norm_odd[i], w_in_odd[i], conv_w[i], a_log[i], dt_bias[i], o_norm[i], w_out_odd[i])
            zero_conv = jnp.zeros((bp, CONV_HIST, state_conv.shape[-1]), F32)
            zero_state = jnp.zeros((bp,) + state_delta.shape[2:], F32)
            hp, cb, st = _odd_layer(hp, zero_conv, zero_state, *od)
            outs["conv_p"].append(cb), outs["st_p"].append(st)
            hs, cb, st = _odd_layer(hs, state_conv[i], state_delta[i], *od)
            outs["conv_s"].append(cb), outs["st_s"].append(st)
    stack = lambda name: jnp.stack(outs[name])
    return (hp, hs, stack("pool_p"), stack("pool_s"), stack("k_p"), stack("v_p"), stack("k_s"), stack("v_s"),
            stack("conv_p"), stack("conv_s"), stack("st_p"), stack("st_s"))
```

```python
import functools
import math

import jax
import jax.numpy as jnp
from jax import lax
from jax.experimental import pallas as pl
from jax.experimental.pallas import tpu as pltpu

F32 = jnp.float32
BF16 = jnp.bfloat16

NORM_EPS = 1e-6
POOL_WINDOWS = (2, 4, 8, 16)
POOL_HIST = max(POOL_WINDOWS) - 1
POOL_PAD = 16
SB_HEADS = 8
SB_HEAD_DIM = 64
SB_BAND = 32
DN_HEADS = 8
DN_DIM = 128
CONV_WIDTH = 4
CONV_HIST = CONV_WIDTH - 1
CONV_PAD = 8
DN_CHUNK = 64
LANES = 128
MXU_DIM = 256
PAGE = 128
VMEM_LIMIT_BYTES = 48 * 1024 * 1024


def _cparams(*sem):
    return pltpu.CompilerParams(dimension_semantics=sem, vmem_limit_bytes=VMEM_LIMIT_BYTES)


def _dot(a, b):
    return jnp.dot(a, b, preferred_element_type=F32)


def _dot_nt(a, b):
    return lax.dot_general(a, b, (((1,), (1,)), ((), ())), preferred_element_type=F32)


def _dot_tn(a, b):
    return lax.dot_general(a, b, (((0,), (0,)), ((), ())), preferred_element_type=F32)


def _split2(x):
    hi = x.astype(BF16)
    return hi, (x - hi.astype(F32)).astype(BF16)


def _split3(x):
    hi = x.astype(BF16)
    r = x - hi.astype(F32)
    mid = r.astype(BF16)
    return hi, mid, (r - mid.astype(F32)).astype(BF16)


def _dot_exact_rhs(a, b_bf16):
    hi, mid, lo = _split3(a)
    return _dot(hi, b_bf16) + _dot(mid, b_bf16) + _dot(lo, b_bf16)


def _dot_exact_lhs(a_bf16, b):
    hi, mid, lo = _split3(b)
    return _dot(a_bf16, hi) + _dot(a_bf16, mid) + _dot(a_bf16, lo)


def _dot_f32(a, b):
    ah, al = _split2(a)
    bh, bl = _split2(b)
    return _dot(ah, bh) + (_dot(ah, bl) + _dot(al, bh))


def _sigmoid(x):
    return 1.0 / (1.0 + jnp.exp2(x * (-math.log2(math.e))))


def _silu(x):
    return x * _sigmoid(x)


def _softplus_neg_abs(x):
    return jnp.log1p(jnp.exp(-jnp.abs(x)))


def _rms_rows(x, g):
    return x * lax.rsqrt(jnp.mean(x * x, axis=-1, keepdims=True) + NORM_EPS) * g


def _row_tile(m, want):
    t = min(m, want)
    assert m % t == 0, (m, t)
    return t


def _even_proj_kernel(x_ref, g_ref, w_ref, seg_ref, qg_ref, kg_ref,
                      u_ref, q_ref, k_ref, v_ref, kb_ref, vb_ref, gate_ref, *, feature_major):
    xn = _rms_rows(x_ref[...], g_ref[...]).astype(BF16)
    pw = u_ref.shape[-1]

    def proj(c0, c1):
        return _dot(xn, w_ref[:, c0:c1])

    def head_norm(y, gain):
        ms = _dot((y * y).astype(BF16), seg_ref[...]) * (1.0 / SB_HEAD_DIM)
        return y * lax.rsqrt(ms + NORM_EPS) * gain

    u_ref[...] = proj(0, pw)
    q = head_norm(proj(pw, 2 * pw), qg_ref[...])
    q_ref[...] = (q * (SB_HEAD_DIM ** -0.5)).astype(BF16)
    k = head_norm(proj(2 * pw, 3 * pw), kg_ref[...])
    kb_ref[...] = k.astype(BF16)
    v = proj(3 * pw, 4 * pw)
    vb_ref[...] = v.astype(BF16)
    if feature_major:
        k_ref[0] = k.T
        v_ref[0] = v.T
    else:
        k_ref[...] = k
        v_ref[...] = v
    gate_ref[...] = proj(4 * pw, 6 * pw)


def _even_proj(x2d, norm_g, w_in, q_g, k_g, seq_len):
    m, d = x2d.shape
    pw = SB_HEADS * SB_HEAD_DIM
    assert w_in.shape == (d, 6 * pw)
    tm = _row_tile(m, 512)
    feature_major = seq_len % tm == 0
    lane_head = jnp.arange(pw) // SB_HEAD_DIM
    seg = (lane_head[:, None] == lane_head[None, :]).astype(BF16)
    row = lambda i: (i, 0)
    fixed = lambda i: (0, 0)
    outs = [((m, pw), F32), ((m, pw), BF16), ((m, pw), F32), ((m, pw), F32), ((m, pw), BF16), ((m, pw), BF16),
            ((m, 2 * pw), F32)]
    out_specs = [pl.BlockSpec((tm, s[-1]), row) for s, _ in outs]
    out_shape = [jax.ShapeDtypeStruct(s, t) for s, t in outs]
    if feature_major:
        tiles = seq_len // tm
        for i in (2, 3):
            out_specs[i] = pl.BlockSpec((1, pw, tm), lambda j: (j // tiles, 0, j % tiles))
            out_shape[i] = jax.ShapeDtypeStruct((m // seq_len, pw, seq_len), F32)
    return pl.pallas_call(
        functools.partial(_even_proj_kernel, feature_major=feature_major),
        grid=(m // tm,),
        in_specs=[pl.BlockSpec((tm, d), row), pl.BlockSpec((1, d), fixed), pl.BlockSpec((d, 6 * pw), fixed),
                  pl.BlockSpec((pw, pw), fixed), pl.BlockSpec((1, pw), fixed), pl.BlockSpec((1, pw), fixed)],
        out_specs=out_specs,
        out_shape=out_shape,
        compiler_params=_cparams("parallel"),
        name="even_in_proj",
    )(x2d, norm_g[None, :], w_in.astype(BF16), seg,
      jnp.tile(q_g, SB_HEADS)[None, :], jnp.tile(k_g, SB_HEADS)[None, :])


def _seqs_per_step(b, length):
    bt = 8 if length <= 8 else 1
    assert b % bt == 0
    return bt


def _pool_kernel(u_ref, buf_ref, wp_ref, scale_ref, y_ref, nb_ref, ext_ref, *, start, tl, bt):
    l = pl.program_id(1)
    lo = POOL_PAD - POOL_HIST
    pos = start + l * tl + lax.broadcasted_iota(jnp.int32, (tl, 1), 0)
    for s in range(bt):
        ext = ext_ref.at[s]

        @pl.when(l == 0)
        def _():
            ext[0:lo, :] = jnp.zeros((lo, ext.shape[-1]), F32)
            ext[lo:POOL_PAD, :] = buf_ref[s]

        @pl.when(l > 0)
        def _():
            ext[lo:POOL_PAD, :] = ext[tl + lo:tl + POOL_PAD, :]

        ext[POOL_PAD:POOL_PAD + tl, :] = u_ref[s]
        for gi, w in enumerate(POOL_WINDOWS):
            cols = slice(gi * LANES, (gi + 1) * LANES)
            cur = ext[POOL_PAD:POOL_PAD + tl, cols]
            win = cur
            for i in range(1, w):
                win = win + ext[POOL_PAD - i:POOL_PAD - i + tl, cols]
            cnt = jnp.minimum(pos + 1, w).astype(F32)
            d = win / cnt - cur
            y_ref[s, :, cols] = _dot(d.astype(BF16), wp_ref[gi]) * scale_ref[:, cols]

        @pl.when(l == pl.num_programs(1) - 1)
        def _():
            nb_ref[s] = ext[tl + lo:tl + POOL_PAD, :]


def _pool_mix(u, buf, start, w_pool, scale):
    b, length, p = u.shape
    assert p == len(POOL_WINDOWS) * LANES and buf.shape == (b, POOL_HIST, p)
    tl = _row_tile(length, 512)
    assert tl == length or tl >= POOL_PAD
    bt = _seqs_per_step(b, length)
    seq = lambda i, l: (i, l, 0)
    per_b = lambda i, l: (i, 0, 0)
    return pl.pallas_call(
        functools.partial(_pool_kernel, start=start, tl=tl, bt=bt),
        grid=(b // bt, length // tl),
        in_specs=[pl.BlockSpec((bt, tl, p), seq), pl.BlockSpec((bt, POOL_HIST, p), per_b),
                  pl.BlockSpec(w_pool.shape, lambda i, l: (0, 0, 0)), pl.BlockSpec((1, p), lambda i, l: (0, 0))],
        out_specs=[pl.BlockSpec((bt, tl, p), seq), pl.BlockSpec((bt, POOL_HIST, p), per_b)],
        out_shape=[jax.ShapeDtypeStruct((b, length, p), F32), jax.ShapeDtypeStruct((b, POOL_HIST, p), F32)],
        scratch_shapes=[pltpu.VMEM((bt, POOL_PAD + tl, p), F32)],
        compiler_params=_cparams("parallel", "arbitrary"),
        name="pool_mix",
    )(u, buf, w_pool.astype(BF16), scale[None, :])


def _log_beta_keep(z):
    m = jnp.minimum(z, 0.0)
    d = m - z
    s = jnp.log(1.0 + jnp.exp(m + d))
    return m - s, d - s


def _stick_break_tiles(zs, carries, tri, masks, chained=False):
    n = range(len(zs))
    log_beta, log_keep = zip(*[_log_beta_keep(z) for z in zs])
    log_keep = [lk if m is None else jnp.where(m, lk, 0.0) for lk, m in zip(log_keep, masks)]
    totals = [jnp.sum(lk, axis=-1, keepdims=True) for lk in log_keep]
    if chained:
        carry_in = [carries[0]]
        for i in n:
            carry_in.append(carry_in[i] + totals[i])
        new_carries = carry_in[-1:]
    else:
        carry_in = carries
        new_carries = [carries[i] + totals[i] for i in n]
    suffix = [_dot(lk.astype(BF16), tri) + carry_in[i] for i, lk in enumerate(log_keep)]
    a = [jnp.exp(log_beta[i] + suffix[i]) for i in n]
    a = [x if m is None else jnp.where(m, x, 0.0) for x, m in zip(a, masks)]
    return [x.astype(BF16) for x in a], new_carries


def _later_key_matrix(n):
    return (lax.broadcasted_iota(jnp.int32, (n, n), 0) > lax.broadcasted_iota(jnp.int32, (n, n), 1)).astype(BF16)


def _sb_prompt_kernel(bias_ref, q_ref, k_ref, v_ref, o_ref, acc_ref, carry_ref, tot_ref, lb_ref, lk_ref, a_ref, *, t):
    qi = pl.program_id(1)
    heads = range(SB_HEADS)
    low_lanes = lax.broadcasted_iota(jnp.int32, (t, LANES), 1) < SB_HEAD_DIM
    tri = _later_key_matrix(t)
    causal = lax.broadcasted_iota(jnp.int32, (t, t), 1) < lax.broadcasted_iota(jnp.int32, (t, t), 0)
    pair_cols = lambda h: slice((h // 2) * LANES, (h // 2 + 1) * LANES)
    qh = []
    for h in heads:
        q_pair = q_ref[0, :, pair_cols(h)]
        qh.append(jnp.where(low_lanes if h % 2 == 0 else jnp.logical_not(low_lanes), q_pair, jnp.zeros_like(q_pair)))

    bands = [slice(r, r + SB_BAND) for r in range(0, t, SB_BAND)]

    def tiles(kb, mask):
        rows = pl.ds(pl.multiple_of(kb * t, t), t)
        for h in heads:
            z = _dot_nt(qh[h], k_ref[0, rows, pair_cols(h)]) + bias_ref[h]
            for band in bands:
                log_beta, log_keep = _log_beta_keep(z[band])
                if mask is not None:
                    log_keep = jnp.where(mask[band], log_keep, 0.0)
                lb_ref[h, band, :] = log_beta
                lk_ref[h, band, :] = log_keep.astype(BF16)
                tot_ref[h, band, :] = jnp.sum(log_keep, axis=-1, keepdims=True)
        for h in heads:
            suffix = _dot(lk_ref[h], tri)
            for band in bands:
                carry = carry_ref[h, band, :]
                a = jnp.exp(lb_ref[h, band, :] + (suffix[band] + carry))
                if mask is not None:
                    a = jnp.where(mask[band], a, 0.0)
                a_ref[h, band, :] = a.astype(BF16)
                carry_ref[h, band, :] = carry + tot_ref[h, band, :]
        for h in heads:
            acc_ref[h] += _dot(a_ref[h], v_ref[0, rows, pair_cols(h)])

    acc_ref[...] = jnp.zeros(acc_ref.shape, F32)
    carry_ref[...] = jnp.zeros(carry_ref.shape, F32)
    tiles(qi, causal)

    def earlier(j, _):
        tiles(qi - 1 - j, None)
        return 0

    lax.fori_loop(0, qi, earlier, 0)
    for h in range(0, SB_HEADS, 2):
        o_ref[0, :, pair_cols(h)] = jnp.where(low_lanes, acc_ref[h], acc_ref[h + 1])


def _sb_prompt(q, k, v, bias):
    b, length, width = q.shape
    assert width == SB_HEADS * SB_HEAD_DIM and 2 * SB_HEAD_DIM == LANES
    t = _row_tile(length, 256)
    q_blk = lambda i, j: (i, j, 0)
    kv_blk = lambda i, j: (i, 0, 0)
    return pl.pallas_call(
        functools.partial(_sb_prompt_kernel, t=t),
        grid=(b, length // t),
        in_specs=[pl.BlockSpec(memory_space=pltpu.SMEM), pl.BlockSpec((1, t, width), q_blk),
                  pl.BlockSpec((1, length, width), kv_blk), pl.BlockSpec((1, length, width), kv_blk)],
        out_specs=pl.BlockSpec((1, t, width), q_blk),
        scratch_shapes=[pltpu.VMEM((SB_HEADS, t, LANES), F32), pltpu.VMEM((SB_HEADS, t, 1), F32),
                        pltpu.VMEM((SB_HEADS, t, 1), F32), pltpu.VMEM((SB_HEADS, t, t), F32),
                        pltpu.VMEM((SB_HEADS, t, t), BF16), pltpu.VMEM((SB_HEADS, t, t), BF16)],
        out_shape=jax.ShapeDtypeStruct((b, length, width), F32),
        compiler_params=_cparams("parallel", "arbitrary"),
        name="stick_break_prompt",
    )(bias, q, k, v)


def _sb_sample_kernel(pt_ref, q_ref, bias_ref, kn_ref, vn_ref, *refs, n_pages, n_new):
    k_refs, v_refs, o_ref = refs[:n_pages], refs[n_pages:2 * n_pages], refs[2 * n_pages]
    kpad_ref, vpad_ref = refs[2 * n_pages + 1:]
    rows = SB_HEADS * n_new
    width = SB_HEADS * SB_HEAD_DIM
    row_i = lax.broadcasted_iota(jnp.int32, (rows, width), 0)
    lane_i = lax.broadcasted_iota(jnp.int32, (rows, width), 1)
    own_head = (row_i // n_new) == (lane_i // SB_HEAD_DIM)
    qbd = jnp.where(own_head, q_ref[0], jnp.zeros_like(q_ref[0]))
    bias = bias_ref[...]
    tri = _later_key_matrix(PAGE)

    kpad_ref[...] = jnp.zeros(kpad_ref.shape, BF16)
    vpad_ref[...] = jnp.zeros(vpad_ref.shape, BF16)
    kpad_ref[0:kn_ref.shape[1], :] = kn_ref[0].astype(BF16)
    vpad_ref[0:vn_ref.shape[1], :] = vn_ref[0].astype(BF16)
    qpos = lax.broadcasted_iota(jnp.int32, (rows, PAGE), 0) % n_new
    kpos = lax.broadcasted_iota(jnp.int32, (rows, PAGE), 1)

    pages = list(reversed(range(n_pages)))
    zs = [_dot_nt(qbd, kpad_ref[...]) + bias] + [_dot(qbd, k_refs[p][0].astype(BF16)) + bias for p in pages]
    a, _ = _stick_break_tiles(zs, [jnp.zeros((rows, 1), F32)], tri, [kpos < qpos] + [None] * n_pages, chained=True)
    out = _dot(a[0], vpad_ref[...])
    for i, p in enumerate(pages):
        out = out + _dot_nt(a[i + 1], v_refs[p][0].astype(BF16))
    out = jnp.where(own_head, out, 0.0)
    acc = out[0:n_new]
    for h in range(1, SB_HEADS):
        acc = acc + out[h * n_new:(h + 1) * n_new]
    o_ref[0] = acc


def _sb_sample(q, k_new, v_new, bias, cache_k, cache_v, page_table):
    b, n_new, width = q.shape
    n_pages = page_table.shape[1]
    n_phys = cache_k.shape[0]
    assert cache_k.shape[1:] == (PAGE, SB_HEADS, SB_HEAD_DIM) and n_new <= 8
    ck = jnp.transpose(cache_k, (0, 2, 3, 1)).reshape(n_phys, width, PAGE)
    cv = jnp.transpose(cache_v, (0, 2, 3, 1)).reshape(n_phys, width, PAGE)
    rows = SB_HEADS * n_new
    pad = ((0, 0), (0, 8 - n_new), (0, 0))
    per_b = lambda i, pt: (i, 0, 0)
    page_specs = [pl.BlockSpec((1, width, PAGE), functools.partial(lambda i, pt, p: (pt[i, p], 0, 0), p=p))
                  for p in range(n_pages)]
    return pl.pallas_call(
        functools.partial(_sb_sample_kernel, n_pages=n_pages, n_new=n_new),
        grid_spec=pltpu.PrefetchScalarGridSpec(
            num_scalar_prefetch=1,
            grid=(b,),
            in_specs=[pl.BlockSpec((1, rows, width), per_b), pl.BlockSpec((rows, 1), lambda i, pt: (0, 0)),
                      pl.BlockSpec((1, 8, width), per_b), pl.BlockSpec((1, 8, width), per_b)] + page_specs + page_specs,
            out_specs=pl.BlockSpec((1, n_new, width), per_b),
            scratch_shapes=[pltpu.VMEM((PAGE, width), BF16), pltpu.VMEM((PAGE, width), BF16)]),
        out_shape=jax.ShapeDtypeStruct((b, n_new, width), F32),
        compiler_params=_cparams("parallel"),
        name="stick_break_sample",
    )(page_table, jnp.tile(q, (1, SB_HEADS, 1)), jnp.repeat(bias, n_new)[:, None],
      jnp.pad(k_new, pad), jnp.pad(v_new, pad), *([ck] * n_pages), *([cv] * n_pages))


def _even_out_kernel(h_ref, py_ref, sb_ref, gate_ref, w_ref, y_ref):
    pw = py_ref.shape[-1]
    m_pool = (py_ref[...] * _silu(gate_ref[:, 0:pw])).astype(BF16)
    m_sb = (sb_ref[...] * _silu(gate_ref[:, pw:2 * pw])).astype(BF16)
    y_ref[...] = h_ref[...] + (_dot(m_pool, w_ref[0:pw, :]) + _dot(m_sb, w_ref[pw:2 * pw, :]))


def _even_out(h2d, pool_y, sb, gate, w_out):
    m, d = h2d.shape
    pw = pool_y.shape[-1]
    tm = _row_tile(m, 512)
    row = lambda i: (i, 0)
    return pl.pallas_call(
        _even_out_kernel,
        grid=(m // tm,),
        in_specs=[pl.BlockSpec((tm, d), row), pl.BlockSpec((tm, pw), row), pl.BlockSpec((tm, pw), row),
                  pl.BlockSpec((tm, 2 * pw), row), pl.BlockSpec((2 * pw, d), lambda i: (0, 0))],
        out_specs=pl.BlockSpec((tm, d), row),
        out_shape=jax.ShapeDtypeStruct((m, d), F32),
        compiler_params=_cparams("parallel"),
        name="even_out_proj",
    )(h2d, pool_y, sb, gate, w_out.astype(BF16))


def _even_layer(h, start, pool_buf, past, norm_g, w_in, w_pool, pool_scale, q_g, k_g, sb_bias, w_out):
    b, length, d = h.shape
    h2d = h.reshape(b * length, d)
    u, q, k, v, kb, vb, gate = _even_proj(h2d, norm_g, w_in, q_g, k_g, length)
    seq = lambda a: a.reshape(b, length, a.shape[-1])
    pool_y, new_buf = _pool_mix(seq(u), pool_buf, start, w_pool, pool_scale)
    if past is None:
        sb = _sb_prompt(seq(q), seq(kb), seq(vb), sb_bias)
    else:
        sb = _sb_sample(seq(q), seq(k), seq(v), sb_bias, *past)
    y = _even_out(h2d, pool_y.reshape(b * length, -1), sb.reshape(b * length, -1), gate, w_out)
    if k.ndim == 3:
        heads = lambda a: jnp.transpose(a.reshape(b, SB_HEADS, SB_HEAD_DIM, length), (0, 3, 1, 2))
    else:
        heads = lambda a: a.reshape(b, length, SB_HEADS, SB_HEAD_DIM)
    return y.reshape(b, length, d), new_buf, heads(k), heads(v)


def _odd_proj_kernel(x_ref, g_ref, w_ref, wba_ref, qkv_ref, z_ref, ba_ref, *, chunk):
    xn = _rms_rows(x_ref[...], g_ref[...]).astype(BF16)
    n_qkv = qkv_ref.shape[-1]
    for c0 in range(0, n_qkv, chunk):
        qkv_ref[:, c0:c0 + chunk] = _dot(xn, w_ref[:, c0:c0 + chunk])
    for c0 in range(0, z_ref.shape[-1], chunk):
        z_ref[:, c0:c0 + chunk] = _dot(xn, w_ref[:, n_qkv + c0:n_qkv + c0 + chunk])
    ba_ref[...] = _dot(xn, wba_ref[...])


def _odd_weights(w_in):
    n_main = 4 * DN_HEADS * DN_DIM
    assert w_in.shape[1] == n_main + 2 * DN_HEADS
    w_main = w_in[:, :n_main].astype(BF16)
    w_ba = jnp.pad(w_in[:, n_main:], ((0, 0), (0, LANES - 2 * DN_HEADS))).astype(BF16)
    return w_main, w_ba


def _odd_proj(x2d, norm_g, w_in):
    m, d = x2d.shape
    n_qkv, n_z = 3 * DN_HEADS * DN_DIM, DN_HEADS * DN_DIM
    tm = _row_tile(m, 256)
    w_main, w_ba = _odd_weights(w_in)
    row = lambda i: (i, 0)
    fixed = lambda i: (0, 0)
    return pl.pallas_call(
        functools.partial(_odd_proj_kernel, chunk=512),
        grid=(m // tm,),
        in_specs=[pl.BlockSpec((tm, d), row), pl.BlockSpec((1, d), fixed), pl.BlockSpec(w_main.shape, fixed),
                  pl.BlockSpec(w_ba.shape, fixed)],
        out_specs=[pl.BlockSpec((tm, n_qkv), row), pl.BlockSpec((tm, n_z), row), pl.BlockSpec((tm, LANES), row)],
        out_shape=[jax.ShapeDtypeStruct((m, n_qkv), F32), jax.ShapeDtypeStruct((m, n_z), F32),
                   jax.ShapeDtypeStruct((m, LANES), F32)],
        compiler_params=_cparams("parallel"),
        name="odd_in_proj",
    )(x2d, norm_g[None, :], w_main, w_ba)


def _conv_head_tile(ext, cw_ref, c, tl):
    cols = slice(c * LANES, (c + 1) * LANES)
    if tl % 8 == 0:
        block = ext[0:CONV_PAD + tl, cols]
        conv = block[CONV_PAD:] * cw_ref[CONV_HIST:CONV_WIDTH, cols]
        for j in range(CONV_HIST):
            conv = conv + pltpu.roll(block, CONV_HIST - j, 0)[CONV_PAD:] * cw_ref[j:j + 1, cols]
    else:
        lo = CONV_PAD - CONV_HIST
        conv = ext[lo:lo + tl, cols] * cw_ref[0:1, cols]
        for j in range(1, CONV_WIDTH):
            conv = conv + ext[lo + j:lo + j + tl, cols] * cw_ref[j:j + 1, cols]
    y = _silu(conv)
    if c < 2 * DN_HEADS:
        inv_norm = lax.rsqrt(jnp.sum(y * y, axis=-1, keepdims=True) + NORM_EPS)
        if c < DN_HEADS:
            inv_norm = inv_norm * (DN_DIM ** -0.5)
        y = y * inv_norm
    return y


def _decay_gates(ba, ab_ref):
    x = ba + ab_ref[1:2, :]
    g = -jnp.exp(ab_ref[0:1, :]) * (jnp.maximum(x, 0.0) + _softplus_neg_abs(x))
    is_beta = lax.broadcasted_iota(jnp.int32, ba.shape, 1) < DN_HEADS
    return jnp.where(is_beta, _sigmoid(ba), g)


def _conv_kernel(x_ref, buf_ref, cw_ref, ba_ref, ab_ref, q_ref, k_ref, v_ref, gb_ref, nb_ref, ext_ref, *, tl, bt):
    l = pl.program_id(1)
    lo = CONV_PAD - CONV_HIST
    width = ext_ref.shape[-1]
    outs = (q_ref, k_ref, v_ref)
    for s in range(bt):
        ext = ext_ref.at[s]

        @pl.when(l == 0)
        def _():
            ext[0:lo, :] = jnp.zeros((lo, width), F32)
            ext[lo:CONV_PAD, :] = buf_ref[s]

        @pl.when(l > 0)
        def _():
            ext[lo:CONV_PAD, :] = ext[tl + lo:tl + CONV_PAD, :]

        ext[CONV_PAD:CONV_PAD + tl, :] = x_ref[s]
        for c in range(width // LANES):
            outs[c // DN_HEADS][s, :, _head_cols(c % DN_HEADS)] = _conv_head_tile(ext, cw_ref, c, tl)
        gb_ref[s] = _decay_gates(ba_ref[s], ab_ref)

        @pl.when(l == pl.num_programs(1) - 1)
        def _():
            nb_ref[s] = ext[tl + lo:tl + CONV_PAD, :]


def _odd_proj_conv_kernel(x_ref, g_ref, w_ref, wba_ref, buf_ref, cw_ref, ab_ref,
                          q_ref, k_ref, v_ref, z_ref, gb_ref, nb_ref, ext_ref, *, chunk, tiles_per_seq):
    i = pl.program_id(0)
    tm = x_ref.shape[0]
    lo = CONV_PAD - CONV_HIST
    n_qkv = ext_ref.shape[-1]
    seq_start = i % tiles_per_seq == 0

    @pl.when(seq_start)
    def _():
        ext_ref[0:lo, :] = jnp.zeros((lo, n_qkv), F32)
        ext_ref[lo:CONV_PAD, :] = buf_ref[0]

    @pl.when(jnp.logical_not(seq_start))
    def _():
        ext_ref[lo:CONV_PAD, :] = ext_ref[tm + lo:tm + CONV_PAD, :]

    xn = _rms_rows(x_ref[...], g_ref[...]).astype(BF16)
    outs = (q_ref, k_ref, v_ref)
    for c0 in range(0, n_qkv, chunk):
        ext_ref[CONV_PAD:CONV_PAD + tm, c0:c0 + chunk] = _dot(xn, w_ref[:, c0:c0 + chunk])
        for c in range(c0 // LANES, (c0 + chunk) // LANES):
            outs[c // DN_HEADS][:, _head_cols(c % DN_HEADS)] = _conv_head_tile(ext_ref, cw_ref, c, tm)
    for c0 in range(0, z_ref.shape[-1], chunk):
        z_ref[:, c0:c0 + chunk] = _dot(xn, w_ref[:, n_qkv + c0:n_qkv + c0 + chunk])
    gb_ref[...] = _decay_gates(_dot(xn, wba_ref[...]), ab_ref)

    @pl.when(i % tiles_per_seq == tiles_per_seq - 1)
    def _():
        nb_ref[0] = ext_ref[tm + lo:tm + CONV_PAD, :]


def _decay_params(a_log, dt_bias):
    park = lambda vec: jnp.pad(vec, (DN_HEADS, LANES - 2 * DN_HEADS))
    return jnp.stack([park(a_log), park(dt_bias)])


def _odd_proj_conv(x2d, seq_len, conv_buf, norm_g, w_in, conv_w, a_log, dt_bias):
    m, d = x2d.shape
    n = DN_HEADS * DN_DIM
    n_qkv = 3 * n
    tm = _row_tile(m, 256)
    assert seq_len % tm == 0 and tm >= CONV_PAD and conv_buf.shape == (m // seq_len, CONV_HIST, n_qkv)
    tiles_per_seq = seq_len // tm
    w_main, w_ba = _odd_weights(w_in)
    row = lambda i: (i, 0)
    fixed = lambda i: (0, 0)
    per_seq = lambda i: (i // tiles_per_seq, 0, 0)
    return pl.pallas_call(
        functools.partial(_odd_proj_conv_kernel, chunk=512, tiles_per_seq=tiles_per_seq),
        grid=(m // tm,),
        in_specs=[pl.BlockSpec((tm, d), row), pl.BlockSpec((1, d), fixed), pl.BlockSpec(w_main.shape, fixed),
                  pl.BlockSpec(w_ba.shape, fixed), pl.BlockSpec((1, CONV_HIST, n_qkv), per_seq),
                  pl.BlockSpec((CONV_WIDTH, n_qkv), fixed), pl.BlockSpec((2, LANES), fixed)],
        out_specs=[pl.BlockSpec((tm, n), row)] * 4 + [pl.BlockSpec((tm, LANES), row),
                                                      pl.BlockSpec((1, CONV_HIST, n_qkv), per_seq)],
        out_shape=[jax.ShapeDtypeStruct((m, n), F32)] * 4
        + [jax.ShapeDtypeStruct((m, LANES), F32), jax.ShapeDtypeStruct(conv_buf.shape, F32)],
        scratch_shapes=[pltpu.VMEM((CONV_PAD + tm, n_qkv), F32)],
        compiler_params=_cparams("arbitrary"),
        name="odd_in_proj_conv",
    )(x2d, norm_g[None, :], w_main, w_ba, conv_buf, conv_w, _decay_params(a_log, dt_bias))


def _conv_gates(qkv, conv_buf, conv_w, ba, a_log, dt_bias):
    b, length, width = qkv.shape
    n = DN_HEADS * DN_DIM
    assert width == 3 * n and conv_buf.shape == (b, CONV_HIST, width)
    tl = _row_tile(length, 256)
    assert tl == length or tl >= CONV_PAD
    ab = _decay_params(a_log, dt_bias)
    bt = _seqs_per_step(b, length)
    seq = lambda i, l: (i, l, 0)
    per_b = lambda i, l: (i, 0, 0)
    fixed = lambda i, l: (0, 0)
    return pl.pallas_call(
        functools.partial(_conv_kernel, tl=tl, bt=bt),
        grid=(b // bt, length // tl),
        in_specs=[pl.BlockSpec((bt, tl, width), seq), pl.BlockSpec((bt, CONV_HIST, width), per_b),
                  pl.BlockSpec((CONV_WIDTH, width), fixed), pl.BlockSpec((bt, tl, LANES), seq),
                  pl.BlockSpec((2, LANES), fixed)],
        out_specs=[pl.BlockSpec((bt, tl, n), seq)] * 3 + [pl.BlockSpec((bt, tl, LANES), seq),
                                                           pl.BlockSpec((bt, CONV_HIST, width), per_b)],
        out_shape=[jax.ShapeDtypeStruct((b, length, n), F32)] * 3
        + [jax.ShapeDtypeStruct((b, length, LANES), F32), jax.ShapeDtypeStruct((b, CONV_HIST, width), F32)],
        scratch_shapes=[pltpu.VMEM((bt, CONV_PAD + tl, width), F32)],
        compiler_params=_cparams("parallel", "arbitrary"),
        name="conv_norm_gates",
    )(qkv, conv_buf, conv_w, ba, ab)


def _unit_lower_inverses(lmats, c):
    eye = (lax.broadcasted_iota(jnp.int32, (c, c), 0) == lax.broadcasted_iota(jnp.int32, (c, c), 1)).astype(F32)
    powers = [-lm for lm in lmats]
    invs = [eye + p for p in powers]
    span = 2
    while span < c:
        powers = [_dot_f32(p, p) for p in powers]
        invs = [i + _dot_f32(i, p) for i, p in zip(invs, powers)]
        span *= 2
    return invs


def _head_cols(h):
    return slice(h * DN_DIM, (h + 1) * DN_DIM)


def _delta_local_kernel(q_ref, k_ref, v_ref, gb_ref, u_ref, wq_ref, kd_ref, att_ref, eg_ref, *, c, n_chunks, group):
    row = lax.broadcasted_iota(jnp.int32, (c, c), 0)
    col = lax.broadcasted_iota(jnp.int32, (c, c), 1)
    incl = row >= col
    strict = row > col
    prefix = incl.astype(BF16)
    eye_lanes = (lax.broadcasted_iota(jnp.int32, (LANES, LANES), 0)
                 == lax.broadcasted_iota(jnp.int32, (LANES, LANES), 1)).astype(BF16)
    units = [(g, h) for g in range(group) for h in range(DN_HEADS)]
    n = range(len(units))

    def chunks(step, _):
        ci = [step * group + g for g in range(group)]
        rows = [pl.ds(pl.multiple_of(x * c, c), c) for x in ci]
        gb = [gb_ref[r, :] for r in rows]
        gcs = [_dot_exact_lhs(prefix, x) for x in gb]
        gcs_t = []
        for x in gcs:
            h3, m3, l3 = _split3(x)
            gcs_t.append(_dot_nt(eye_lanes, h3) + _dot_nt(eye_lanes, m3) + _dot_nt(eye_lanes, l3))
        q = [q_ref[rows[g], _head_cols(h)] for g, h in units]
        k = [k_ref[rows[g], _head_cols(h)] for g, h in units]
        v = [v_ref[rows[g], _head_cols(h)] for g, h in units]
        beta = [gb[g][:, h:h + 1] for g, h in units]
        gc = [gcs[g][:, DN_HEADS + h:DN_HEADS + h + 1] for g, h in units]
        g_last = [gcs[g][c - 1:c, DN_HEADS + h:DN_HEADS + h + 1] for g, h in units]
        gc_row = [gcs_t[g][DN_HEADS + h:DN_HEADS + h + 1, :] for g, h in units]
        decay = [jnp.where(incl, jnp.exp(jnp.where(incl, gc[i] - gc_row[i], 0.0)), 0.0) for i in n]
        kb = [k[i] * beta[i] for i in n]
        k16 = [k[i].astype(BF16) for i in n]
        lmat = [jnp.where(strict, _dot_nt(kb[i].astype(BF16), k16[i]) * decay[i], 0.0) for i in n]
        tmat = [t.astype(BF16) for t in _unit_lower_inverses(lmat, c)]
        egc = [jnp.exp(gc[i]) for i in n]
        for i, (g, h) in enumerate(units):
            cols = _head_cols(h)
            u_ref[rows[g], cols] = _dot(tmat[i], (v[i] * beta[i]).astype(BF16))
            w = _dot(tmat[i], (kb[i] * egc[i]).astype(BF16))
            wq_ref[ci[g], :, cols] = jnp.concatenate([w, q[i] * egc[i]], axis=0).astype(BF16)
            kd_ref[rows[g], cols] = (k[i] * jnp.exp(g_last[i] - gc[i])).astype(BF16)
            att = _dot_nt(q[i].astype(BF16), k16[i]) * decay[i]
            if c < LANES:
                att = jnp.concatenate([att, jnp.zeros((c, LANES - c), F32)], axis=1)
            att_ref[rows[g], cols] = att.astype(BF16)
            eg_ref[ci[g], h:h + 1, :] = jnp.broadcast_to(jnp.exp(g_last[i]), (1, LANES))
        return 0

    assert n_chunks % group == 0
    lax.fori_loop(0, n_chunks // group, chunks, 0)


def _delta_scan_kernel(u_ref, wq_ref, kd_ref, att_ref, eg_ref, s0_ref, o_ref, s_out_ref, s_ref, *, c, n_chunks, bt):
    l = pl.program_id(1)

    @pl.when(l == 0)
    def _():
        s_ref[...] = s0_ref[...]

    eye_lanes = (lax.broadcasted_iota(jnp.int32, (LANES, LANES), 0)
                 == lax.broadcasted_iota(jnp.int32, (LANES, LANES), 1)).astype(BF16)
    units = [(i, h) for i in range(bt) for h in range(DN_HEADS)]

    def chunk(ci, _):
        rows = pl.ds(pl.multiple_of(ci * c, c), c)
        s = [s_ref[i, h] for i, h in units]
        s16 = [x.astype(BF16) for x in s]
        r = [_dot(wq_ref[i, ci, :, _head_cols(h)], s16[n]) for n, (i, h) in enumerate(units)]
        v16 = [(u_ref[i, rows, _head_cols(h)] - r[n][0:c]).astype(BF16) for n, (i, h) in enumerate(units)]
        for n, (i, h) in enumerate(units):
            cols = _head_cols(h)
            att = att_ref[i, rows, h * DN_DIM:h * DN_DIM + c]
            o_ref[i, rows, cols] = r[n][c:2 * c] + _dot(att, v16[n])
            s_ref[i, h] = s[n] * eg_ref[i, ci, h:h + 1, :] + _dot_tn(kd_ref[i, rows, cols], v16[n])
        return 0

    lax.fori_loop(0, n_chunks, chunk, 0)

    @pl.when(l == pl.num_programs(1) - 1)
    def _():
        s_out_ref[...] = s_ref[...]


def _gated_delta(q, k, v, gb, state, c):
    b, length, n = q.shape
    assert n == DN_HEADS * DN_DIM and state.shape == (b, DN_HEADS, DN_DIM, DN_DIM) and length % c == 0
    assert c % 8 == 0 and c <= LANES
    tokens = b * length
    flat = lambda a: a.reshape(tokens, a.shape[-1])
    tn = _row_tile(tokens, max(c, 256))
    row = lambda i: (i, 0)
    per_chunk = lambda i: (i, 0, 0)
    u, wq, kd, att, eg = pl.pallas_call(
        functools.partial(_delta_local_kernel, c=c, n_chunks=tn // c, group=2 if c >= DN_CHUNK else 4),
        grid=(tokens // tn,),
        in_specs=[pl.BlockSpec((tn, n), row)] * 3 + [pl.BlockSpec((tn, LANES), row)],
        out_specs=[pl.BlockSpec((tn, n), row), pl.BlockSpec((tn // c, 2 * c, n), per_chunk),
                   pl.BlockSpec((tn, n), row), pl.BlockSpec((tn, n), row),
                   pl.BlockSpec((tn // c, DN_HEADS, LANES), per_chunk)],
        out_shape=[jax.ShapeDtypeStruct((tokens, n), F32), jax.ShapeDtypeStruct((tokens // c, 2 * c, n), BF16),
                   jax.ShapeDtypeStruct((tokens, n), BF16), jax.ShapeDtypeStruct((tokens, n), BF16),
                   jax.ShapeDtypeStruct((tokens // c, DN_HEADS, LANES), F32)],
        compiler_params=_cparams("parallel"),
        name="delta_local",
    )(flat(q), flat(k), flat(v), flat(gb))

    n_seq_chunks = length // c
    tl = c * min(n_seq_chunks, 4)
    bt = 2 if n_seq_chunks > 1 else 4
    assert length % tl == 0 and b % bt == 0
    seq = lambda i, l: (i, l, 0)
    seq_chunks = lambda i, l: (i, l, 0, 0)
    per_b = lambda i, l: (i, 0, 0, 0)
    st_block = (bt,) + state.shape[1:]
    return pl.pallas_call(
        functools.partial(_delta_scan_kernel, c=c, n_chunks=tl // c, bt=bt),
        grid=(b // bt, length // tl),
        in_specs=[pl.BlockSpec((bt, tl, n), seq), pl.BlockSpec((bt, tl // c, 2 * c, n), seq_chunks),
                  pl.BlockSpec((bt, tl, n), seq), pl.BlockSpec((bt, tl, n), seq),
                  pl.BlockSpec((bt, tl // c, DN_HEADS, LANES), seq_chunks), pl.BlockSpec(st_block, per_b)],
        out_specs=[pl.BlockSpec((bt, tl, n), seq), pl.BlockSpec(st_block, per_b)],
        out_shape=[jax.ShapeDtypeStruct((b, length, n), F32), jax.ShapeDtypeStruct(state.shape, F32)],
        scratch_shapes=[pltpu.VMEM(st_block, F32)],
        compiler_params=_cparams("parallel", "arbitrary"),
        name="delta_scan",
    )(u.reshape(b, length, n), wq.reshape(b, n_seq_chunks, 2 * c, n), kd.reshape(b, length, n),
      att.reshape(b, length, n), eg.reshape(b, n_seq_chunks, DN_HEADS, LANES), state)


def _odd_out_kernel(h_ref, o_ref, z_ref, og_ref, w_ref, y_ref, gated_ref):
    for hd in range(DN_HEADS):
        cols = _head_cols(hd)
        gated_ref[:, cols] = (_rms_rows(o_ref[:, cols], og_ref[...]) * _silu(z_ref[:, cols])).astype(BF16)
    y_ref[...] = h_ref[...] + _dot(gated_ref[...], w_ref[...])


def _odd_out(h2d, o2d, z, o_g, w_out):
    m, d = h2d.shape
    n = o2d.shape[-1]
    tm = _row_tile(m, 512)
    row = lambda i: (i, 0)
    fixed = lambda i: (0, 0)
    return pl.pallas_call(
        _odd_out_kernel,
        grid=(m // tm,),
        in_specs=[pl.BlockSpec((tm, d), row), pl.BlockSpec((tm, n), row), pl.BlockSpec((tm, n), row),
                  pl.BlockSpec((1, DN_DIM), fixed), pl.BlockSpec((n, d), fixed)],
        out_specs=pl.BlockSpec((tm, d), row),
        out_shape=jax.ShapeDtypeStruct((m, d), F32),
        scratch_shapes=[pltpu.VMEM((tm, n), BF16)],
        compiler_params=_cparams("parallel"),
        name="odd_out_proj",
    )(h2d, o2d, z, o_g[None, :], w_out.astype(BF16))


def _odd_layer(h, conv_buf, state, norm_g, w_in, conv_w, a_log, dt_bias, o_g, w_out):
    b, length, d = h.shape
    h2d = h.reshape(b * length, d)
    seq = lambda a: a.reshape(b, length, a.shape[-1])
    if length % min(b * length, 256) == 0:
        q, k, v, z, gb, new_buf = _odd_proj_conv(h2d, length, conv_buf, norm_g, w_in, conv_w, a_log, dt_bias)
        q, k, v, gb = seq(q), seq(k), seq(v), seq(gb)
    else:
        qkv, z, ba = _odd_proj(h2d, norm_g, w_in)
        q, k, v, gb, new_buf = _conv_gates(seq(qkv), conv_buf, conv_w, seq(ba), a_log, dt_bias)
    c = DN_CHUNK if length % DN_CHUNK == 0 else length
    c_pad = -(-c // 8) * 8
    if c_pad != c:
        pad = lambda a: jnp.pad(a, ((0, 0), (0, c_pad - c), (0, 0)))
        o, new_state = _gated_delta(pad(q), pad(k), pad(v), pad(gb), state, c_pad)
        o = o[:, :length]
    else:
        o, new_state = _gated_delta(q, k, v, gb, state, c)
    y = _odd_out(h2d, o.reshape(b * length, -1), z, o_g, w_out)
    return y.reshape(b, length, d), new_buf, new_state


def kernel(x_prompt, x_sample, cache_pool, cache_k, cache_v, state_conv, state_delta, page_table, norm_even, w_in_even, w_pool, pool_scale, q_norm, k_norm, sb_bias, w_out_even, norm_odd, w_in_odd, conv_w, a_log, dt_bias, o_norm, w_out_odd):
    bp = x_prompt.shape[0]
    past_len = page_table.shape[1] * cache_k.shape[2]
    depth = norm_even.shape[0] + norm_odd.shape[0]
    hp, hs = x_prompt, x_sample
    outs = {name: [] for name in ("pool_p", "pool_s", "k_p", "v_p", "k_s", "v_s", "conv_p", "conv_s", "st_p", "st_s")}
    for layer in range(depth):
        i = layer // 2
        if layer % 2 == 0:
            ev = (norm_even[i], w_in_even[i], w_pool[i], pool_scale[i], q_norm[i], k_norm[i], sb_bias[i], w_out_even[i])
            zero_buf = jnp.zeros((bp, POOL_HIST, cache_pool.shape[-1]), F32)
            hp, pb, kn, vn = _even_layer(hp, 0, zero_buf, None, *ev)
            outs["pool_p"].append(pb), outs["k_p"].append(kn), outs["v_p"].append(vn)
            hs, pb, kn, vn = _even_layer(hs, past_len, cache_pool[i], (cache_k[i], cache_v[i], page_table), *ev)
            outs["pool_s"].append(pb), outs["k_s"].append(kn), outs["v_s"].append(vn)
        else:
            od = (norm_odd[i], w_in_odd[i], conv_w[i], a_log[i], dt_bias[i], o_norm[i], w_out_odd[i])
            zero_conv = jnp.zeros((bp, CONV_HIST, state_conv.shape[-1]), F32)
            zero_state = jnp.zeros((bp,) + state_delta.shape[2:], F32)
            hp, cb, st = _odd_layer(hp, zero_conv, zero_state, *od)
            outs["conv_p"].append(cb), outs["st_p"].append(st)
            hs, cb, st = _odd_layer(hs, state_conv[i], state_delta[i], *od)
            outs["conv_s"].append(cb), outs["st_s"].append(st)
    stack = lambda name: jnp.stack(outs[name])
    return (hp, hs, stack("pool_p"), stack("pool_s"), stack("k_p"), stack("v_p"), stack("k_s"), stack("v_s"),
            stack("conv_p"), stack("conv_s"), stack("st_p"), stack("st_s"))
```

```python
import functools
import math

import jax
import jax.numpy as jnp
from jax import lax
from jax.experimental import pallas as pl
from jax.experimental.pallas import tpu as pltpu

F32 = jnp.float32
BF16 = jnp.bfloat16

NORM_EPS = 1e-6
POOL_WINDOWS = (2, 4, 8, 16)
POOL_HIST = max(POOL_WINDOWS) - 1
POOL_PAD = 16
SB_HEADS = 8
SB_HEAD_DIM = 64
SB_BAND = 32
DN_HEADS = 8
DN_DIM = 128
CONV_WIDTH = 4
CONV_HIST = CONV_WIDTH - 1
CONV_PAD = 8
DN_CHUNK = 64
LANES = 128
MXU_DIM = 256
PAGE = 128
VMEM_LIMIT_BYTES = 48 * 1024 * 1024


def _cparams(*sem):
    return pltpu.CompilerParams(dimension_semantics=sem, vmem_limit_bytes=VMEM_LIMIT_BYTES)


def _dot(a, b):
    return jnp.dot(a, b, preferred_element_type=F32)


def _dot_nt(a, b):
    return lax.dot_general(a, b, (((1,), (1,)), ((), ())), preferred_element_type=F32)


def _dot_tn(a, b):
    return lax.dot_general(a, b, (((0,), (0,)), ((), ())), preferred_element_type=F32)


def _split2(x):
    hi = x.astype(BF16)
    return hi, (x - hi.astype(F32)).astype(BF16)


def _split3(x):
    hi = x.astype(BF16)
    r = x - hi.astype(F32)
    mid = r.astype(BF16)
    return hi, mid, (r - mid.astype(F32)).astype(BF16)


def _dot_exact_rhs(a, b_bf16):
    hi, mid, lo = _split3(a)
    return _dot(hi, b_bf16) + _dot(mid, b_bf16) + _dot(lo, b_bf16)


def _dot_exact_lhs(a_bf16, b):
    hi, mid, lo = _split3(b)
    return _dot(a_bf16, hi) + _dot(a_bf16, mid) + _dot(a_bf16, lo)


def _dot_f32(a, b):
    ah, al = _split2(a)
    bh, bl = _split2(b)
    return _dot(ah, bh) + (_dot(ah, bl) + _dot(al, bh))


def _sigmoid(x):
    return 1.0 / (1.0 + jnp.exp2(x * (-math.log2(math.e))))


def _silu(x):
    return x * _sigmoid(x)


def _softplus_neg_abs(x):
    return jnp.log1p(jnp.exp(-jnp.abs(x)))


def _rms_rows(x, g):
    return x * lax.rsqrt(jnp.mean(x * x, axis=-1, keepdims=True) + NORM_EPS) * g


def _row_tile(m, want):
    t = min(m, want)
    assert m % t == 0, (m, t)
    return t


def _even_proj_kernel(x_ref, g_ref, w_ref, seg_ref, qg_ref, kg_ref,
                      u_ref, q_ref, k_ref, v_ref, kb_ref, vb_ref, gate_ref, *, feature_major):
    xn = _rms_rows(x_ref[...], g_ref[...]).astype(BF16)
    pw = u_ref.shape[-1]

    def proj(c0, c1):
        return _dot(xn, w_ref[:, c0:c1])

    def head_norm(y, gain):
        ms = _dot((y * y).astype(BF16), seg_ref[...]) * (1.0 / SB_HEAD_DIM)
        return y * lax.rsqrt(ms + NORM_EPS) * gain

    u_ref[...] = proj(0, pw)
    q = head_norm(proj(pw, 2 * pw), qg_ref[...])
    q_ref[...] = (q * (SB_HEAD_DIM ** -0.5)).astype(BF16)
    k = head_norm(proj(2 * pw, 3 * pw), kg_ref[...])
    kb_ref[...] = k.astype(BF16)
    v = proj(3 * pw, 4 * pw)
    vb_ref[...] = v.astype(BF16)
    if feature_major:
        k_ref[0] = k.T
        v_ref[0] = v.T
    else:
        k_ref[...] = k
        v_ref[...] = v
    gate_ref[...] = proj(4 * pw, 6 * pw)


def _even_proj(x2d, norm_g, w_in, q_g, k_g, seq_len):
    m, d = x2d.shape
    pw = SB_HEADS * SB_HEAD_DIM
    assert w_in.shape == (d, 6 * pw)
    tm = _row_tile(m, 512)
    feature_major = seq_len % tm == 0
    lane_head = jnp.arange(pw) // SB_HEAD_DIM
    seg = (lane_head[:, None] == lane_head[None, :]).astype(BF16)
    row = lambda i: (i, 0)
    fixed = lambda i: (0, 0)
    outs = [((m, pw), F32), ((m, pw), BF16), ((m, pw), F32), ((m, pw), F32), ((m, pw), BF16), ((m, pw), BF16),
            ((m, 2 * pw), F32)]
    out_specs = [pl.BlockSpec((tm, s[-1]), row) for s, _ in outs]
    out_shape = [jax.ShapeDtypeStruct(s, t) for s, t in outs]
    if feature_major:
        tiles = seq_len // tm
        for i in (2, 3):
            out_specs[i] = pl.BlockSpec((1, pw, tm), lambda j: (j // tiles, 0, j % tiles))
            out_shape[i] = jax.ShapeDtypeStruct((m // seq_len, pw, seq_len), F32)
    return pl.pallas_call(
        functools.partial(_even_proj_kernel, feature_major=feature_major),
        grid=(m // tm,),
        in_specs=[pl.BlockSpec((tm, d), row), pl.BlockSpec((1, d), fixed), pl.BlockSpec((d, 6 * pw), fixed),
                  pl.BlockSpec((pw, pw), fixed), pl.BlockSpec((1, pw), fixed), pl.BlockSpec((1, pw), fixed)],
        out_specs=out_specs,
        out_shape=out_shape,
        compiler_params=_cparams("parallel"),
        name="even_in_proj",
    )(x2d, norm_g[None, :], w_in.astype(BF16), seg,
      jnp.tile(q_g, SB_HEADS)[None, :], jnp.tile(k_g, SB_HEADS)[None, :])


def _seqs_per_step(b, length):
    bt = 8 if length <= 8 else 1
    assert b % bt == 0
    return bt


def _pool_group(ext, gi, pos, wp_ref, scale_ref, tl):
    w = POOL_WINDOWS[gi]
    cols = slice(gi * LANES, (gi + 1) * LANES)
    if tl % 8 == 0:
        block = ext[0:POOL_PAD + tl, cols]
        win = block
        span = 1
        while span < w:
            win = win + pltpu.roll(win, span, 0)
            span *= 2
        win, cur = win[POOL_PAD:], block[POOL_PAD:]
    else:
        cur = ext[POOL_PAD:POOL_PAD + tl, cols]
        win = cur
        for i in range(1, w):
            win = win + ext[POOL_PAD - i:POOL_PAD - i + tl, cols]
    cnt = jnp.minimum(pos + 1, w).astype(F32)
    d = win / cnt - cur
    return _dot(d.astype(BF16), wp_ref[gi]) * scale_ref[:, cols]


def _pool_kernel(u_ref, buf_ref, wp_ref, scale_ref, y_ref, nb_ref, ext_ref, *, start, tl, bt):
    l = pl.program_id(1)
    lo = POOL_PAD - POOL_HIST
    pos = start + l * tl + lax.broadcasted_iota(jnp.int32, (tl, 1), 0)
    for s in range(bt):
        ext = ext_ref.at[s]

        @pl.when(l == 0)
        def _():
            ext[0:lo, :] = jnp.zeros((lo, ext.shape[-1]), F32)
            ext[lo:POOL_PAD, :] = buf_ref[s]

        @pl.when(l > 0)
        def _():
            ext[lo:POOL_PAD, :] = ext[tl + lo:tl + POOL_PAD, :]

        ext[POOL_PAD:POOL_PAD + tl, :] = u_ref[s]
        for gi in range(len(POOL_WINDOWS)):
            y_ref[s, :, gi * LANES:(gi + 1) * LANES] = _pool_group(ext, gi, pos, wp_ref, scale_ref, tl)

        @pl.when(l == pl.num_programs(1) - 1)
        def _():
            nb_ref[s] = ext[tl + lo:tl + POOL_PAD, :]


def _pool_mix(u, buf, start, w_pool, scale):
    b, length, p = u.shape
    assert p == len(POOL_WINDOWS) * LANES and buf.shape == (b, POOL_HIST, p)
    tl = _row_tile(length, 512)
    assert tl == length or tl >= POOL_PAD
    bt = _seqs_per_step(b, length)
    seq = lambda i, l: (i, l, 0)
    per_b = lambda i, l: (i, 0, 0)
    return pl.pallas_call(
        functools.partial(_pool_kernel, start=start, tl=tl, bt=bt),
        grid=(b // bt, length // tl),
        in_specs=[pl.BlockSpec((bt, tl, p), seq), pl.BlockSpec((bt, POOL_HIST, p), per_b),
                  pl.BlockSpec(w_pool.shape, lambda i, l: (0, 0, 0)), pl.BlockSpec((1, p), lambda i, l: (0, 0))],
        out_specs=[pl.BlockSpec((bt, tl, p), seq), pl.BlockSpec((bt, POOL_HIST, p), per_b)],
        out_shape=[jax.ShapeDtypeStruct((b, length, p), F32), jax.ShapeDtypeStruct((b, POOL_HIST, p), F32)],
        scratch_shapes=[pltpu.VMEM((bt, POOL_PAD + tl, p), F32)],
        compiler_params=_cparams("parallel", "arbitrary"),
        name="pool_mix",
    )(u, buf, w_pool.astype(BF16), scale[None, :])


def _log_beta_keep(z):
    m = jnp.minimum(z, 0.0)
    d = m - z
    s = jnp.log(1.0 + jnp.exp(m + d))
    return m - s, d - s


def _stick_break_tiles(zs, carries, tri, masks, chained=False):
    n = range(len(zs))
    log_beta, log_keep = zip(*[_log_beta_keep(z) for z in zs])
    log_keep = [lk if m is None else jnp.where(m, lk, 0.0) for lk, m in zip(log_keep, masks)]
    totals = [jnp.sum(lk, axis=-1, keepdims=True) for lk in log_keep]
    if chained:
        carry_in = [carries[0]]
        for i in n:
            carry_in.append(carry_in[i] + totals[i])
        new_carries = carry_in[-1:]
    else:
        carry_in = carries
        new_carries = [carries[i] + totals[i] for i in n]
    suffix = [_dot(lk.astype(BF16), tri) + carry_in[i] for i, lk in enumerate(log_keep)]
    a = [jnp.exp(log_beta[i] + suffix[i]) for i in n]
    a = [x if m is None else jnp.where(m, x, 0.0) for x, m in zip(a, masks)]
    return [x.astype(BF16) for x in a], new_carries


def _later_key_matrix(n):
    return (lax.broadcasted_iota(jnp.int32, (n, n), 0) > lax.broadcasted_iota(jnp.int32, (n, n), 1)).astype(BF16)


def _sb_prompt_kernel(bias_ref, q_ref, k_ref, v_ref, o_ref, acc_ref, carry_ref, tot_ref, lb_ref, lk_ref, a_ref, *, t):
    qi = pl.program_id(1)
    heads = range(SB_HEADS)
    low_lanes = lax.broadcasted_iota(jnp.int32, (t, LANES), 1) < SB_HEAD_DIM
    tri = _later_key_matrix(t)
    causal = lax.broadcasted_iota(jnp.int32, (t, t), 1) < lax.broadcasted_iota(jnp.int32, (t, t), 0)
    pair_cols = lambda h: slice((h // 2) * LANES, (h // 2 + 1) * LANES)
    qh = []
    for h in heads:
        q_pair = q_ref[0, :, pair_cols(h)]
        qh.append(jnp.where(low_lanes if h % 2 == 0 else jnp.logical_not(low_lanes), q_pair, jnp.zeros_like(q_pair)))

    bands = [slice(r, r + SB_BAND) for r in range(0, t, SB_BAND)]

    def tiles(kb, mask):
        rows = pl.ds(pl.multiple_of(kb * t, t), t)
        for h in heads:
            z = _dot_nt(qh[h], k_ref[0, rows, pair_cols(h)]) + bias_ref[h]
            for band in bands:
                log_beta, log_keep = _log_beta_keep(z[band])
                if mask is not None:
                    log_keep = jnp.where(mask[band], log_keep, 0.0)
                lb_ref[h, band, :] = log_beta
                lk_ref[h, band, :] = log_keep.astype(BF16)
                tot_ref[h, band, :] = jnp.sum(log_keep, axis=-1, keepdims=True)
        for h in heads:
            suffix = _dot(lk_ref[h], tri)
            for band in bands:
                carry = carry_ref[h, band, :]
                a = jnp.exp(lb_ref[h, band, :] + (suffix[band] + carry))
                if mask is not None:
                    a = jnp.where(mask[band], a, 0.0)
                a_ref[h, band, :] = a.astype(BF16)
                carry_ref[h, band, :] = carry + tot_ref[h, band, :]
        for h in heads:
            acc_ref[h] += _dot(a_ref[h], v_ref[0, rows, pair_cols(h)])

    acc_ref[...] = jnp.zeros(acc_ref.shape, F32)
    carry_ref[...] = jnp.zeros(carry_ref.shape, F32)
    tiles(qi, causal)

    def earlier(j, _):
        tiles(qi - 1 - j, None)
        return 0

    lax.fori_loop(0, qi, earlier, 0)
    for h in range(0, SB_HEADS, 2):
        o_ref[0, :, pair_cols(h)] = jnp.where(low_lanes, acc_ref[h], acc_ref[h + 1])


def _sb_prompt(q, k, v, bias):
    b, length, width = q.shape
    assert width == SB_HEADS * SB_HEAD_DIM and 2 * SB_HEAD_DIM == LANES
    t = _row_tile(length, 256)
    q_blk = lambda i, j: (i, j, 0)
    kv_blk = lambda i, j: (i, 0, 0)
    return pl.pallas_call(
        functools.partial(_sb_prompt_kernel, t=t),
        grid=(b, length // t),
        in_specs=[pl.BlockSpec(memory_space=pltpu.SMEM), pl.BlockSpec((1, t, width), q_blk),
                  pl.BlockSpec((1, length, width), kv_blk), pl.BlockSpec((1, length, width), kv_blk)],
        out_specs=pl.BlockSpec((1, t, width), q_blk),
        scratch_shapes=[pltpu.VMEM((SB_HEADS, t, LANES), F32), pltpu.VMEM((SB_HEADS, t, 1), F32),
                        pltpu.VMEM((SB_HEADS, t, 1), F32), pltpu.VMEM((SB_HEADS, t, t), F32),
                        pltpu.VMEM((SB_HEADS, t, t), BF16), pltpu.VMEM((SB_HEADS, t, t), BF16)],
        out_shape=jax.ShapeDtypeStruct((b, length, width), F32),
        compiler_params=_cparams("parallel", "arbitrary"),
        name="stick_break_prompt",
    )(bias, q, k, v)


def _sb_sample_kernel(pt_ref, q_ref, bias_ref, kn_ref, vn_ref, *refs, n_pages, n_new):
    k_refs, v_refs, o_ref = refs[:n_pages], refs[n_pages:2 * n_pages], refs[2 * n_pages]
    kpad_ref, vpad_ref = refs[2 * n_pages + 1:]
    rows = SB_HEADS * n_new
    width = SB_HEADS * SB_HEAD_DIM
    row_i = lax.broadcasted_iota(jnp.int32, (rows, width), 0)
    lane_i = lax.broadcasted_iota(jnp.int32, (rows, width), 1)
    own_head = (row_i // n_new) == (lane_i // SB_HEAD_DIM)
    qbd = jnp.where(own_head, q_ref[0], jnp.zeros_like(q_ref[0]))
    bias = bias_ref[...]
    tri = _later_key_matrix(PAGE)

    kpad_ref[...] = jnp.zeros(kpad_ref.shape, BF16)
    vpad_ref[...] = jnp.zeros(vpad_ref.shape, BF16)
    kpad_ref[0:kn_ref.shape[1], :] = kn_ref[0].astype(BF16)
    vpad_ref[0:vn_ref.shape[1], :] = vn_ref[0].astype(BF16)
    qpos = lax.broadcasted_iota(jnp.int32, (rows, PAGE), 0) % n_new
    kpos = lax.broadcasted_iota(jnp.int32, (rows, PAGE), 1)

    pages = list(reversed(range(n_pages)))
    zs = [_dot_nt(qbd, kpad_ref[...]) + bias] + [_dot(qbd, k_refs[p][0].astype(BF16)) + bias for p in pages]
    a, _ = _stick_break_tiles(zs, [jnp.zeros((rows, 1), F32)], tri, [kpos < qpos] + [None] * n_pages, chained=True)
    out = _dot(a[0], vpad_ref[...])
    for i, p in enumerate(pages):
        out = out + _dot_nt(a[i + 1], v_refs[p][0].astype(BF16))
    out = jnp.where(own_head, out, 0.0)
    acc = out[0:n_new]
    for h in range(1, SB_HEADS):
        acc = acc + out[h * n_new:(h + 1) * n_new]
    o_ref[0] = acc


def _sb_sample(q, k_new, v_new, bias, cache_k, cache_v, page_table):
    b, n_new, width = q.shape
    n_pages = page_table.shape[1]
    n_phys = cache_k.shape[0]
    assert cache_k.shape[1:] == (PAGE, SB_HEADS, SB_HEAD_DIM) and n_new <= 8
    ck = jnp.transpose(cache_k, (0, 2, 3, 1)).reshape(n_phys, width, PAGE)
    cv = jnp.transpose(cache_v, (0, 2, 3, 1)).reshape(n_phys, width, PAGE)
    rows = SB_HEADS * n_new
    pad = ((0, 0), (0, 8 - n_new), (0, 0))
    per_b = lambda i, pt: (i, 0, 0)
    page_specs = [pl.BlockSpec((1, width, PAGE), functools.partial(lambda i, pt, p: (pt[i, p], 0, 0), p=p))
                  for p in range(n_pages)]
    return pl.pallas_call(
        functools.partial(_sb_sample_kernel, n_pages=n_pages, n_new=n_new),
        grid_spec=pltpu.PrefetchScalarGridSpec(
            num_scalar_prefetch=1,
            grid=(b,),
            in_specs=[pl.BlockSpec((1, rows, width), per_b), pl.BlockSpec((rows, 1), lambda i, pt: (0, 0)),
                      pl.BlockSpec((1, 8, width), per_b), pl.BlockSpec((1, 8, width), per_b)] + page_specs + page_specs,
            out_specs=pl.BlockSpec((1, n_new, width), per_b),
            scratch_shapes=[pltpu.VMEM((PAGE, width), BF16), pltpu.VMEM((PAGE, width), BF16)]),
        out_shape=jax.ShapeDtypeStruct((b, n_new, width), F32),
        compiler_params=_cparams("parallel"),
        name="stick_break_sample",
    )(page_table, jnp.tile(q, (1, SB_HEADS, 1)), jnp.repeat(bias, n_new)[:, None],
      jnp.pad(k_new, pad), jnp.pad(v_new, pad), *([ck] * n_pages), *([cv] * n_pages))


def _even_out_kernel(h_ref, py_ref, sb_ref, gate_ref, w_ref, y_ref):
    pw = py_ref.shape[-1]
    m_pool = (py_ref[...] * _silu(gate_ref[:, 0:pw])).astype(BF16)
    m_sb = (sb_ref[...] * _silu(gate_ref[:, pw:2 * pw])).astype(BF16)
    y_ref[...] = h_ref[...] + (_dot(m_pool, w_ref[0:pw, :]) + _dot(m_sb, w_ref[pw:2 * pw, :]))


def _even_out(h2d, pool_y, sb, gate, w_out):
    m, d = h2d.shape
    pw = pool_y.shape[-1]
    tm = _row_tile(m, 512)
    row = lambda i: (i, 0)
    return pl.pallas_call(
        _even_out_kernel,
        grid=(m // tm,),
        in_specs=[pl.BlockSpec((tm, d), row), pl.BlockSpec((tm, pw), row), pl.BlockSpec((tm, pw), row),
                  pl.BlockSpec((tm, 2 * pw), row), pl.BlockSpec((2 * pw, d), lambda i: (0, 0))],
        out_specs=pl.BlockSpec((tm, d), row),
        out_shape=jax.ShapeDtypeStruct((m, d), F32),
        compiler_params=_cparams("parallel"),
        name="even_out_proj",
    )(h2d, pool_y, sb, gate, w_out.astype(BF16))


def _even_head(h, past, norm_g, w_in, q_g, k_g, sb_bias):
    b, length, d = h.shape
    u, q, k, v, kb, vb, gate = _even_proj(h.reshape(b * length, d), norm_g, w_in, q_g, k_g, length)
    seq = lambda a: a.reshape(b, length, a.shape[-1])
    if past is None:
        sb = _sb_prompt(seq(q), seq(kb), seq(vb), sb_bias)
    else:
        sb = _sb_sample(seq(q), seq(k), seq(v), sb_bias, *past)
    if k.ndim == 3:
        heads = lambda a: jnp.transpose(a.reshape(b, SB_HEADS, SB_HEAD_DIM, length), (0, 3, 1, 2))
    else:
        heads = lambda a: a.reshape(b, length, SB_HEADS, SB_HEAD_DIM)
    return (seq(u), sb, gate), heads(k), heads(v)


def _even_tail(h, parts, start, pool_buf, w_pool, pool_scale, w_out):
    b, length, d = h.shape
    u, sb, gate = parts
    pool_y, new_buf = _pool_mix(u, pool_buf, start, w_pool, pool_scale)
    y = _even_out(h.reshape(b * length, d), pool_y.reshape(b * length, -1), sb.reshape(b * length, -1), gate, w_out)
    return y.reshape(b, length, d), new_buf


def _even_layer(h, start, pool_buf, past, norm_g, w_in, w_pool, pool_scale, q_g, k_g, sb_bias, w_out):
    parts, k, v = _even_head(h, past, norm_g, w_in, q_g, k_g, sb_bias)
    y, new_buf = _even_tail(h, parts, start, pool_buf, w_pool, pool_scale, w_out)
    return y, new_buf, k, v


def _odd_proj_kernel(x_ref, g_ref, w_ref, wba_ref, qkv_ref, z_ref, ba_ref, *, chunk):
    xn = _rms_rows(x_ref[...], g_ref[...]).astype(BF16)
    n_qkv = qkv_ref.shape[-1]
    for c0 in range(0, n_qkv, chunk):
        qkv_ref[:, c0:c0 + chunk] = _dot(xn, w_ref[:, c0:c0 + chunk])
    for c0 in range(0, z_ref.shape[-1], chunk):
        z_ref[:, c0:c0 + chunk] = _dot(xn, w_ref[:, n_qkv + c0:n_qkv + c0 + chunk])
    ba_ref[...] = _dot(xn, wba_ref[...])


def _odd_weights(w_in):
    n_main = 4 * DN_HEADS * DN_DIM
    assert w_in.shape[1] == n_main + 2 * DN_HEADS
    w_main = w_in[:, :n_main].astype(BF16)
    w_ba = jnp.pad(w_in[:, n_main:], ((0, 0), (0, LANES - 2 * DN_HEADS))).astype(BF16)
    return w_main, w_ba


def _odd_proj(x2d, norm_g, w_in):
    m, d = x2d.shape
    n_qkv, n_z = 3 * DN_HEADS * DN_DIM, DN_HEADS * DN_DIM
    tm = _row_tile(m, 256)
    w_main, w_ba = _odd_weights(w_in)
    row = lambda i: (i, 0)
    fixed = lambda i: (0, 0)
    return pl.pallas_call(
        functools.partial(_odd_proj_kernel, chunk=512),
        grid=(m // tm,),
        in_specs=[pl.BlockSpec((tm, d), row), pl.BlockSpec((1, d), fixed), pl.BlockSpec(w_main.shape, fixed),
                  pl.BlockSpec(w_ba.shape, fixed)],
        out_specs=[pl.BlockSpec((tm, n_qkv), row), pl.BlockSpec((tm, n_z), row), pl.BlockSpec((tm, LANES), row)],
        out_shape=[jax.ShapeDtypeStruct((m, n_qkv), F32), jax.ShapeDtypeStruct((m, n_z), F32),
                   jax.ShapeDtypeStruct((m, LANES), F32)],
        compiler_params=_cparams("parallel"),
        name="odd_in_proj",
    )(x2d, norm_g[None, :], w_main, w_ba)


def _conv_head_tile(ext, cw_ref, c, tl):
    cols = slice(c * LANES, (c + 1) * LANES)
    if tl % 8 == 0:
        block = ext[0:CONV_PAD + tl, cols]
        conv = block[CONV_PAD:] * cw_ref[CONV_HIST:CONV_WIDTH, cols]
        for j in range(CONV_HIST):
            conv = conv + pltpu.roll(block, CONV_HIST - j, 0)[CONV_PAD:] * cw_ref[j:j + 1, cols]
    else:
        lo = CONV_PAD - CONV_HIST
        conv = ext[lo:lo + tl, cols] * cw_ref[0:1, cols]
        for j in range(1, CONV_WIDTH):
            conv = conv + ext[lo + j:lo + j + tl, cols] * cw_ref[j:j + 1, cols]
    y = _silu(conv)
    if c < 2 * DN_HEADS:
        inv_norm = lax.rsqrt(jnp.sum(y * y, axis=-1, keepdims=True) + NORM_EPS)
        if c < DN_HEADS:
            inv_norm = inv_norm * (DN_DIM ** -0.5)
        y = y * inv_norm
    return y


def _decay_gates(ba, ab_ref):
    x = ba + ab_ref[1:2, :]
    g = -jnp.exp(ab_ref[0:1, :]) * (jnp.maximum(x, 0.0) + _softplus_neg_abs(x))
    is_beta = lax.broadcasted_iota(jnp.int32, ba.shape, 1) < DN_HEADS
    return jnp.where(is_beta, _sigmoid(ba), g)


def _conv_kernel(x_ref, buf_ref, cw_ref, ba_ref, ab_ref, q_ref, k_ref, v_ref, gb_ref, nb_ref, ext_ref, *, tl, bt):
    l = pl.program_id(1)
    lo = CONV_PAD - CONV_HIST
    width = ext_ref.shape[-1]
    outs = (q_ref, k_ref, v_ref)
    for s in range(bt):
        ext = ext_ref.at[s]

        @pl.when(l == 0)
        def _():
            ext[0:lo, :] = jnp.zeros((lo, width), F32)
            ext[lo:CONV_PAD, :] = buf_ref[s]

        @pl.when(l > 0)
        def _():
            ext[lo:CONV_PAD, :] = ext[tl + lo:tl + CONV_PAD, :]

        ext[CONV_PAD:CONV_PAD + tl, :] = x_ref[s]
        for c in range(width // LANES):
            outs[c // DN_HEADS][s, :, _head_cols(c % DN_HEADS)] = _conv_head_tile(ext, cw_ref, c, tl)
        gb_ref[s] = _decay_gates(ba_ref[s], ab_ref)

        @pl.when(l == pl.num_programs(1) - 1)
        def _():
            nb_ref[s] = ext[tl + lo:tl + CONV_PAD, :]


def _odd_proj_conv_kernel(x_ref, g_ref, w_ref, wba_ref, buf_ref, cw_ref, ab_ref,
                          q_ref, k_ref, v_ref, z_ref, gb_ref, nb_ref, ext_ref, *, chunk, tiles_per_seq):
    i = pl.program_id(0)
    tm = x_ref.shape[0]
    lo = CONV_PAD - CONV_HIST
    n_qkv = ext_ref.shape[-1]
    seq_start = i % tiles_per_seq == 0

    @pl.when(seq_start)
    def _():
        ext_ref[0:lo, :] = jnp.zeros((lo, n_qkv), F32)
        ext_ref[lo:CONV_PAD, :] = buf_ref[0]

    @pl.when(jnp.logical_not(seq_start))
    def _():
        ext_ref[lo:CONV_PAD, :] = ext_ref[tm + lo:tm + CONV_PAD, :]

    xn = _rms_rows(x_ref[...], g_ref[...]).astype(BF16)
    _project_conv(xn, w_ref, wba_ref, cw_ref, ab_ref, ext_ref, q_ref, k_ref, v_ref, z_ref, gb_ref, chunk)

    @pl.when(i % tiles_per_seq == tiles_per_seq - 1)
    def _():
        nb_ref[0] = ext_ref[tm + lo:tm + CONV_PAD, :]


def _project_conv(xn, w_ref, wba_ref, cw_ref, ab_ref, ext_ref, q_ref, k_ref, v_ref, z_ref, gb_ref, chunk):
    tm = xn.shape[0]
    n_qkv = ext_ref.shape[-1]
    outs = (q_ref, k_ref, v_ref)
    for c0 in range(0, n_qkv, chunk):
        ext_ref[CONV_PAD:CONV_PAD + tm, c0:c0 + chunk] = _dot(xn, w_ref[:, c0:c0 + chunk])
        for c in range(c0 // LANES, (c0 + chunk) // LANES):
            outs[c // DN_HEADS][:, _head_cols(c % DN_HEADS)] = _conv_head_tile(ext_ref, cw_ref, c, tm)
    for c0 in range(0, z_ref.shape[-1], chunk):
        z_ref[:, c0:c0 + chunk] = _dot(xn, w_ref[:, n_qkv + c0:n_qkv + c0 + chunk])
    gb_ref[...] = _decay_gates(_dot(xn, wba_ref[...]), ab_ref)


def _even_tail_odd_head_kernel(h_ref, u_ref, sb_ref, gate_ref, pbuf_ref, wp_ref, pscale_ref, wout_ref,
                               g_ref, w_ref, wba_ref, cbuf_ref, cw_ref, ab_ref,
                               y_ref, npb_ref, q_ref, k_ref, v_ref, z_ref, gb_ref, ncb_ref,
                               pext_ref, mix_ref, cext_ref, *, start, chunk, tiles_per_seq):
    i = pl.program_id(0)
    tm = h_ref.shape[0]
    l = i % tiles_per_seq
    plo = POOL_PAD - POOL_HIST
    clo = CONV_PAD - CONV_HIST
    pw = u_ref.shape[-1]

    @pl.when(l == 0)
    def _():
        pext_ref[0:plo, :] = jnp.zeros((plo, pw), F32)
        pext_ref[plo:POOL_PAD, :] = pbuf_ref[0]
        cext_ref[0:clo, :] = jnp.zeros((clo, cext_ref.shape[-1]), F32)
        cext_ref[clo:CONV_PAD, :] = cbuf_ref[0]

    @pl.when(l > 0)
    def _():
        pext_ref[plo:POOL_PAD, :] = pext_ref[tm + plo:tm + POOL_PAD, :]
        cext_ref[clo:CONV_PAD, :] = cext_ref[tm + clo:tm + CONV_PAD, :]

    pext_ref[POOL_PAD:POOL_PAD + tm, :] = u_ref[...]
    pos = start + l * tm + lax.broadcasted_iota(jnp.int32, (tm, 1), 0)
    for gi in range(len(POOL_WINDOWS)):
        cols = slice(gi * LANES, (gi + 1) * LANES)
        pooled = _pool_group(pext_ref, gi, pos, wp_ref, pscale_ref, tm)
        mix_ref[:, cols] = (pooled * _silu(gate_ref[:, cols])).astype(BF16)
    for c0 in range(0, pw, LANES):
        cols = slice(pw + c0, pw + c0 + LANES)
        mix_ref[:, cols] = (sb_ref[:, c0:c0 + LANES] * _silu(gate_ref[:, cols])).astype(BF16)
    y = h_ref[...] + _dot(mix_ref[...], wout_ref[...])
    y_ref[...] = y

    xn = _rms_rows(y, g_ref[...]).astype(BF16)
    _project_conv(xn, w_ref, wba_ref, cw_ref, ab_ref, cext_ref, q_ref, k_ref, v_ref, z_ref, gb_ref, chunk)

    @pl.when(l == tiles_per_seq - 1)
    def _():
        npb_ref[0] = pext_ref[tm + plo:tm + POOL_PAD, :]
        ncb_ref[0] = cext_ref[tm + clo:tm + CONV_PAD, :]


def _decay_params(a_log, dt_bias):
    park = lambda vec: jnp.pad(vec, (DN_HEADS, LANES - 2 * DN_HEADS))
    return jnp.stack([park(a_log), park(dt_bias)])


def _odd_proj_conv(x2d, seq_len, conv_buf, norm_g, w_in, conv_w, a_log, dt_bias):
    m, d = x2d.shape
    n = DN_HEADS * DN_DIM
    n_qkv = 3 * n
    tm = _row_tile(m, 256)
    assert seq_len % tm == 0 and tm >= CONV_PAD and conv_buf.shape == (m // seq_len, CONV_HIST, n_qkv)
    tiles_per_seq = seq_len // tm
    w_main, w_ba = _odd_weights(w_in)
    row = lambda i: (i, 0)
    fixed = lambda i: (0, 0)
    per_seq = lambda i: (i // tiles_per_seq, 0, 0)
    return pl.pallas_call(
        functools.partial(_odd_proj_conv_kernel, chunk=512, tiles_per_seq=tiles_per_seq),
        grid=(m // tm,),
        in_specs=[pl.BlockSpec((tm, d), row), pl.BlockSpec((1, d), fixed), pl.BlockSpec(w_main.shape, fixed),
                  pl.BlockSpec(w_ba.shape, fixed), pl.BlockSpec((1, CONV_HIST, n_qkv), per_seq),
                  pl.BlockSpec((CONV_WIDTH, n_qkv), fixed), pl.BlockSpec((2, LANES), fixed)],
        out_specs=[pl.BlockSpec((tm, n), row)] * 4 + [pl.BlockSpec((tm, LANES), row),
                                                      pl.BlockSpec((1, CONV_HIST, n_qkv), per_seq)],
        out_shape=[jax.ShapeDtypeStruct((m, n), F32)] * 4
        + [jax.ShapeDtypeStruct((m, LANES), F32), jax.ShapeDtypeStruct(conv_buf.shape, F32)],
        scratch_shapes=[pltpu.VMEM((CONV_PAD + tm, n_qkv), F32)],
        compiler_params=_cparams("arbitrary"),
        name="odd_in_proj_conv",
    )(x2d, norm_g[None, :], w_main, w_ba, conv_buf, conv_w, _decay_params(a_log, dt_bias))


def _conv_gates(qkv, conv_buf, conv_w, ba, a_log, dt_bias):
    b, length, width = qkv.shape
    n = DN_HEADS * DN_DIM
    assert width == 3 * n and conv_buf.shape == (b, CONV_HIST, width)
    tl = _row_tile(length, 256)
    assert tl == length or tl >= CONV_PAD
    ab = _decay_params(a_log, dt_bias)
    bt = _seqs_per_step(b, length)
    seq = lambda i, l: (i, l, 0)
    per_b = lambda i, l: (i, 0, 0)
    fixed = lambda i, l: (0, 0)
    return pl.pallas_call(
        functools.partial(_conv_kernel, tl=tl, bt=bt),
        grid=(b // bt, length // tl),
        in_specs=[pl.BlockSpec((bt, tl, width), seq), pl.BlockSpec((bt, CONV_HIST, width), per_b),
                  pl.BlockSpec((CONV_WIDTH, width), fixed), pl.BlockSpec((bt, tl, LANES), seq),
                  pl.BlockSpec((2, LANES), fixed)],
        out_specs=[pl.BlockSpec((bt, tl, n), seq)] * 3 + [pl.BlockSpec((bt, tl, LANES), seq),
                                                           pl.BlockSpec((bt, CONV_HIST, width), per_b)],
        out_shape=[jax.ShapeDtypeStruct((b, length, n), F32)] * 3
        + [jax.ShapeDtypeStruct((b, length, LANES), F32), jax.ShapeDtypeStruct((b, CONV_HIST, width), F32)],
        scratch_shapes=[pltpu.VMEM((bt, CONV_PAD + tl, width), F32)],
        compiler_params=_cparams("parallel", "arbitrary"),
        name="conv_norm_gates",
    )(qkv, conv_buf, conv_w, ba, ab)


def _unit_lower_inverses(lmats, c):
    eye = (lax.broadcasted_iota(jnp.int32, (c, c), 0) == lax.broadcasted_iota(jnp.int32, (c, c), 1)).astype(F32)
    powers = [-lm for lm in lmats]
    invs = [eye + p for p in powers]
    span = 2
    while span < c:
        powers = [_dot_f32(p, p) for p in powers]
        invs = [i + _dot_f32(i, p) for i, p in zip(invs, powers)]
        span *= 2
    return invs


def _head_cols(h):
    return slice(h * DN_DIM, (h + 1) * DN_DIM)


def _delta_local_kernel(q_ref, k_ref, v_ref, gb_ref, u_ref, wq_ref, kd_ref, att_ref, eg_ref, *, c, n_chunks, group):
    row = lax.broadcasted_iota(jnp.int32, (c, c), 0)
    col = lax.broadcasted_iota(jnp.int32, (c, c), 1)
    incl = row >= col
    strict = row > col
    prefix = incl.astype(BF16)
    eye_lanes = (lax.broadcasted_iota(jnp.int32, (LANES, LANES), 0)
                 == lax.broadcasted_iota(jnp.int32, (LANES, LANES), 1)).astype(BF16)
    units = [(g, h) for g in range(group) for h in range(DN_HEADS)]
    n = range(len(units))

    def chunks(step, _):
        ci = [step * group + g for g in range(group)]
        rows = [pl.ds(pl.multiple_of(x * c, c), c) for x in ci]
        gb = [gb_ref[r, :] for r in rows]
        gcs = [_dot_exact_lhs(prefix, x) for x in gb]
        gcs_t = []
        for x in gcs:
            h3, m3, l3 = _split3(x)
            gcs_t.append(_dot_nt(eye_lanes, h3) + _dot_nt(eye_lanes, m3) + _dot_nt(eye_lanes, l3))
        q = [q_ref[rows[g], _head_cols(h)] for g, h in units]
        k = [k_ref[rows[g], _head_cols(h)] for g, h in units]
        v = [v_ref[rows[g], _head_cols(h)] for g, h in units]
        beta = [gb[g][:, h:h + 1] for g, h in units]
        gc = [gcs[g][:, DN_HEADS + h:DN_HEADS + h + 1] for g, h in units]
        g_last = [gcs[g][c - 1:c, DN_HEADS + h:DN_HEADS + h + 1] for g, h in units]
        gc_row = [gcs_t[g][DN_HEADS + h:DN_HEADS + h + 1, :] for g, h in units]
        decay = [jnp.where(incl, jnp.exp(jnp.where(incl, gc[i] - gc_row[i], 0.0)), 0.0) for i in n]
        kb = [k[i] * beta[i] for i in n]
        k16 = [k[i].astype(BF16) for i in n]
        lmat = [jnp.where(strict, _dot_nt(kb[i].astype(BF16), k16[i]) * decay[i], 0.0) for i in n]
        tmat = [t.astype(BF16) for t in _unit_lower_inverses(lmat, c)]
        egc = [jnp.exp(gc[i]) for i in n]
        for i, (g, h) in enumerate(units):
            cols = _head_cols(h)
            u_ref[rows[g], cols] = _dot(tmat[i], (v[i] * beta[i]).astype(BF16))
            w = _dot(tmat[i], (kb[i] * egc[i]).astype(BF16))
            wq_ref[ci[g], :, cols] = jnp.concatenate([w, q[i] * egc[i]], axis=0).astype(BF16)
            kd_ref[rows[g], cols] = (k[i] * jnp.exp(g_last[i] - gc[i])).astype(BF16)
            att = _dot_nt(q[i].astype(BF16), k16[i]) * decay[i]
            if c < LANES:
                att = jnp.concatenate([att, jnp.zeros((c, LANES - c), F32)], axis=1)
            att_ref[rows[g], cols] = att.astype(BF16)
            eg_ref[ci[g], h:h + 1, :] = jnp.broadcast_to(jnp.exp(g_last[i]), (1, LANES))
        return 0

    assert n_chunks % group == 0
    lax.fori_loop(0, n_chunks // group, chunks, 0)


def _delta_scan_kernel(u_ref, wq_ref, kd_ref, att_ref, eg_ref, s0_ref, o_ref, s_out_ref, s_ref, *, c, n_chunks, bt):
    l = pl.program_id(1)

    @pl.when(l == 0)
    def _():
        s_ref[...] = s0_ref[...]

    eye_lanes = (lax.broadcasted_iota(jnp.int32, (LANES, LANES), 0)
                 == lax.broadcasted_iota(jnp.int32, (LANES, LANES), 1)).astype(BF16)
    units = [(i, h) for i in range(bt) for h in range(DN_HEADS)]

    def chunk(ci, _):
        rows = pl.ds(pl.multiple_of(ci * c, c), c)
        s = [s_ref[i, h] for i, h in units]
        s16 = [x.astype(BF16) for x in s]
        r = [_dot(wq_ref[i, ci, :, _head_cols(h)], s16[n]) for n, (i, h) in enumerate(units)]
        v16 = [(u_ref[i, rows, _head_cols(h)] - r[n][0:c]).astype(BF16) for n, (i, h) in enumerate(units)]
        for n, (i, h) in enumerate(units):
            cols = _head_cols(h)
            att = att_ref[i, rows, h * DN_DIM:h * DN_DIM + c]
            o_ref[i, rows, cols] = r[n][c:2 * c] + _dot(att, v16[n])
            s_ref[i, h] = s[n] * eg_ref[i, ci, h:h + 1, :] + _dot_tn(kd_ref[i, rows, cols], v16[n])
        return 0

    lax.fori_loop(0, n_chunks, chunk, 0)

    @pl.when(l == pl.num_programs(1) - 1)
    def _():
        s_out_ref[...] = s_ref[...]


def _gated_delta(q, k, v, gb, state, c):
    b, length, n = q.shape
    assert n == DN_HEADS * DN_DIM and state.shape == (b, DN_HEADS, DN_DIM, DN_DIM) and length % c == 0
    assert c % 8 == 0 and c <= LANES
    tokens = b * length
    flat = lambda a: a.reshape(tokens, a.shape[-1])
    tn = _row_tile(tokens, max(c, 256))
    row = lambda i: (i, 0)
    per_chunk = lambda i: (i, 0, 0)
    u, wq, kd, att, eg = pl.pallas_call(
        functools.partial(_delta_local_kernel, c=c, n_chunks=tn // c, group=2 if c >= DN_CHUNK else 4),
        grid=(tokens // tn,),
        in_specs=[pl.BlockSpec((tn, n), row)] * 3 + [pl.BlockSpec((tn, LANES), row)],
        out_specs=[pl.BlockSpec((tn, n), row), pl.BlockSpec((tn // c, 2 * c, n), per_chunk),
                   pl.BlockSpec((tn, n), row), pl.BlockSpec((tn, n), row),
                   pl.BlockSpec((tn // c, DN_HEADS, LANES), per_chunk)],
        out_shape=[jax.ShapeDtypeStruct((tokens, n), F32), jax.ShapeDtypeStruct((tokens // c, 2 * c, n), BF16),
                   jax.ShapeDtypeStruct((tokens, n), BF16), jax.ShapeDtypeStruct((tokens, n), BF16),
                   jax.ShapeDtypeStruct((tokens // c, DN_HEADS, LANES), F32)],
        compiler_params=_cparams("parallel"),
        name="delta_local",
    )(flat(q), flat(k), flat(v), flat(gb))

    n_seq_chunks = length // c
    tl = c * min(n_seq_chunks, 4)
    bt = 2 if n_seq_chunks > 1 else 4
    assert length % tl == 0 and b % bt == 0
    seq = lambda i, l: (i, l, 0)
    seq_chunks = lambda i, l: (i, l, 0, 0)
    per_b = lambda i, l: (i, 0, 0, 0)
    st_block = (bt,) + state.shape[1:]
    return pl.pallas_call(
        functools.partial(_delta_scan_kernel, c=c, n_chunks=tl // c, bt=bt),
        grid=(b // bt, length // tl),
        in_specs=[pl.BlockSpec((bt, tl, n), seq), pl.BlockSpec((bt, tl // c, 2 * c, n), seq_chunks),
                  pl.BlockSpec((bt, tl, n), seq), pl.BlockSpec((bt, tl, n), seq),
                  pl.BlockSpec((bt, tl // c, DN_HEADS, LANES), seq_chunks), pl.BlockSpec(st_block, per_b)],
        out_specs=[pl.BlockSpec((bt, tl, n), seq), pl.BlockSpec(st_block, per_b)],
        out_shape=[jax.ShapeDtypeStruct((b, length, n), F32), jax.ShapeDtypeStruct(state.shape, F32)],
        scratch_shapes=[pltpu.VMEM(st_block, F32)],
        compiler_params=_cparams("parallel", "arbitrary"),
        name="delta_scan",
    )(u.reshape(b, length, n), wq.reshape(b, n_seq_chunks, 2 * c, n), kd.reshape(b, length, n),
      att.reshape(b, length, n), eg.reshape(b, n_seq_chunks, DN_HEADS, LANES), state)


def _odd_out_kernel(h_ref, o_ref, z_ref, og_ref, w_ref, y_ref, gated_ref):
    for hd in range(DN_HEADS):
        cols = _head_cols(hd)
        gated_ref[:, cols] = (_rms_rows(o_ref[:, cols], og_ref[...]) * _silu(z_ref[:, cols])).astype(BF16)
    y_ref[...] = h_ref[...] + _dot(gated_ref[...], w_ref[...])


def _odd_out(h2d, o2d, z, o_g, w_out):
    m, d = h2d.shape
    n = o2d.shape[-1]
    tm = _row_tile(m, 512)
    row = lambda i: (i, 0)
    fixed = lambda i: (0, 0)
    return pl.pallas_call(
        _odd_out_kernel,
        grid=(m // tm,),
        in_specs=[pl.BlockSpec((tm, d), row), pl.BlockSpec((tm, n), row), pl.BlockSpec((tm, n), row),
                  pl.BlockSpec((1, DN_DIM), fixed), pl.BlockSpec((n, d), fixed)],
        out_specs=pl.BlockSpec((tm, d), row),
        out_shape=jax.ShapeDtypeStruct((m, d), F32),
        scratch_shapes=[pltpu.VMEM((tm, n), BF16)],
        compiler_params=_cparams("parallel"),
        name="odd_out_proj",
    )(h2d, o2d, z, o_g[None, :], w_out.astype(BF16))


def _whole_tile_sequences(b, length):
    return length % min(b * length, 256) == 0


def _odd_head(h, conv_buf, norm_g, w_in, conv_w, a_log, dt_bias):
    b, length, d = h.shape
    h2d = h.reshape(b * length, d)
    seq = lambda a: a.reshape(b, length, a.shape[-1])
    if _whole_tile_sequences(b, length):
        q, k, v, z, gb, new_buf = _odd_proj_conv(h2d, length, conv_buf, norm_g, w_in, conv_w, a_log, dt_bias)
        q, k, v, gb = seq(q), seq(k), seq(v), seq(gb)
    else:
        qkv, z, ba = _odd_proj(h2d, norm_g, w_in)
        q, k, v, gb, new_buf = _conv_gates(seq(qkv), conv_buf, conv_w, seq(ba), a_log, dt_bias)
    return (q, k, v, z, gb), new_buf


def _even_tail_odd_head(h, parts, start, pool_buf, w_pool, pool_scale, w_out_even,
                        conv_buf, norm_g, w_in, conv_w, a_log, dt_bias):
    b, length, d = h.shape
    m = b * length
    u, sb, gate = parts
    pw = u.shape[-1]
    n = DN_HEADS * DN_DIM
    n_qkv = 3 * n
    tm = _row_tile(m, 256)
    assert length % tm == 0 and tm >= POOL_PAD
    tiles_per_seq = length // tm
    w_main, w_ba = _odd_weights(w_in)
    row = lambda i: (i, 0)
    fixed = lambda i: (0, 0)
    per_seq = lambda i: (i // tiles_per_seq, 0, 0)
    y, new_pool, q, k, v, z, gb, new_conv = pl.pallas_call(
        functools.partial(_even_tail_odd_head_kernel, start=start, chunk=512, tiles_per_seq=tiles_per_seq),
        grid=(m // tm,),
        in_specs=[pl.BlockSpec((tm, d), row), pl.BlockSpec((tm, pw), row), pl.BlockSpec((tm, pw), row),
                  pl.BlockSpec((tm, 2 * pw), row), pl.BlockSpec((1, POOL_HIST, pw), per_seq),
                  pl.BlockSpec(w_pool.shape, lambda i: (0, 0, 0)), pl.BlockSpec((1, pw), fixed),
                  pl.BlockSpec((2 * pw, d), fixed), pl.BlockSpec((1, d), fixed), pl.BlockSpec(w_main.shape, fixed),
                  pl.BlockSpec(w_ba.shape, fixed), pl.BlockSpec((1, CONV_HIST, n_qkv), per_seq),
                  pl.BlockSpec((CONV_WIDTH, n_qkv), fixed), pl.BlockSpec((2, LANES), fixed)],
        out_specs=[pl.BlockSpec((tm, d), row), pl.BlockSpec((1, POOL_HIST, pw), per_seq)]
        + [pl.BlockSpec((tm, n), row)] * 4 + [pl.BlockSpec((tm, LANES), row),
                                              pl.BlockSpec((1, CONV_HIST, n_qkv), per_seq)],
        out_shape=[jax.ShapeDtypeStruct((m, d), F32), jax.ShapeDtypeStruct((b, POOL_HIST, pw), F32)]
        + [jax.ShapeDtypeStruct((m, n), F32)] * 4
        + [jax.ShapeDtypeStruct((m, LANES), F32), jax.ShapeDtypeStruct((b, CONV_HIST, n_qkv), F32)],
        scratch_shapes=[pltpu.VMEM((POOL_PAD + tm, pw), F32), pltpu.VMEM((tm, 2 * pw), BF16),
                        pltpu.VMEM((CONV_PAD + tm, n_qkv), F32)],
        compiler_params=_cparams("arbitrary"),
        name="even_tail_odd_head",
    )(h.reshape(m, d), u.reshape(m, pw), sb.reshape(m, pw), gate, pool_buf, w_pool.astype(BF16),
      pool_scale[None, :], w_out_even.astype(BF16), norm_g[None, :], w_main, w_ba, conv_buf, conv_w,
      _decay_params(a_log, dt_bias))
    seq = lambda a: a.reshape(b, length, a.shape[-1])
    return y.reshape(b, length, d), new_pool, (seq(q), seq(k), seq(v), z, seq(gb)), new_conv


def _odd_layer(h, conv_buf, state, norm_g, w_in, conv_w, a_log, dt_bias, o_g, w_out):
    parts, new_buf = _odd_head(h, conv_buf, norm_g, w_in, conv_w, a_log, dt_bias)
    y, new_state = _odd_tail(h, parts, state, o_g, w_out)
    return y, new_buf, new_state


def _odd_tail(h, parts, state, o_g, w_out):
    b, length, d = h.shape
    h2d = h.reshape(b * length, d)
    q, k, v, z, gb = parts
    c = DN_CHUNK if length % DN_CHUNK == 0 else length
    c_pad = -(-c // 8) * 8
    if c_pad != c:
        pad = lambda a: jnp.pad(a, ((0, 0), (0, c_pad - c), (0, 0)))
        o, new_state = _gated_delta(pad(q), pad(k), pad(v), pad(gb), state, c_pad)
        o = o[:, :length]
    else:
        o, new_state = _gated_delta(q, k, v, gb, state, c)
    y = _odd_out(h2d, o.reshape(b * length, -1), z, o_g, w_out)
    return y.reshape(b, length, d), new_state


def _even_odd_layers(h, start, pool_buf, past, conv_buf, state, ev, od):
    norm_e, w_in_e, w_pool, pool_scale, q_g, k_g, sb_bias, w_out_e = ev
    norm_o, w_in_o, conv_w, a_log, dt_bias, o_g, w_out_o = od
    b, length, _ = h.shape
    if not _whole_tile_sequences(b, length):
        y, new_pool, k, v = _even_layer(h, start, pool_buf, past, *ev)
        y, new_conv, new_state = _odd_layer(y, conv_buf, state, *od)
        return y, new_pool, k, v, new_conv, new_state
    parts, k, v = _even_head(h, past, norm_e, w_in_e, q_g, k_g, sb_bias)
    y, new_pool, odd_parts, new_conv = _even_tail_odd_head(
        h, parts, start, pool_buf, w_pool, pool_scale, w_out_e, conv_buf, norm_o, w_in_o, conv_w, a_log, dt_bias)
    y, new_state = _odd_tail(y, odd_parts, state, o_g, w_out_o)
    return y, new_pool, k, v, new_conv, new_state


def kernel(x_prompt, x_sample, cache_pool, cache_k, cache_v, state_conv, state_delta, page_table, norm_even, w_in_even, w_pool, pool_scale, q_norm, k_norm, sb_bias, w_out_even, norm_odd, w_in_odd, conv_w, a_log, dt_bias, o_norm, w_out_odd):
    bp = x_prompt.shape[0]
    past_len = page_table.shape[1] * cache_k.shape[2]
    n_even, n_odd = norm_even.shape[0], norm_odd.shape[0]
    assert n_odd <= n_even <= n_odd + 1
    hp, hs = x_prompt, x_sample
    outs = {name: [] for name in ("pool_p", "pool_s", "k_p", "v_p", "k_s", "v_s", "conv_p", "conv_s", "st_p", "st_s")}
    for i in range(n_even):
        ev = (norm_even[i], w_in_even[i], w_pool[i], pool_scale[i], q_norm[i], k_norm[i], sb_bias[i], w_out_even[i])
        zero_buf = jnp.zeros((bp, POOL_HIST, cache_pool.shape[-1]), F32)
        past = (cache_k[i], cache_v[i], page_table)
        if i < n_odd:
            od = (norm_odd[i], w_in_odd[i], conv_w[i], a_log[i], dt_bias[i], o_norm[i], w_out_odd[i])
            zero_conv = jnp.zeros((bp, CONV_HIST, state_conv.shape[-1]), F32)
            zero_state = jnp.zeros((bp,) + state_delta.shape[2:], F32)
            hp, pb, kn, vn, cb, st = _even_odd_layers(hp, 0, zero_buf, None, zero_conv, zero_state, ev, od)
            outs["conv_p"].append(cb), outs["st_p"].append(st)
            outs["pool_p"].append(pb), outs["k_p"].append(kn), outs["v_p"].append(vn)
            hs, pb, kn, vn, cb, st = _even_odd_layers(hs, past_len, cache_pool[i], past, state_conv[i],
                                                      state_delta[i], ev, od)
            outs["conv_s"].append(cb), outs["st_s"].append(st)
            outs["pool_s"].append(pb), outs["k_s"].append(kn), outs["v_s"].append(vn)
        else:
            hp, pb, kn, vn = _even_layer(hp, 0, zero_buf, None, *ev)
            outs["pool_p"].append(pb), outs["k_p"].append(kn), outs["v_p"].append(vn)
            hs, pb, kn, vn = _even_layer(hs, past_len, cache_pool[i], past, *ev)
            outs["pool_s"].append(pb), outs["k_s"].append(kn), outs["v_s"].append(vn)
    stack = lambda name: jnp.stack(outs[name])
    return (hp, hs, stack("pool_p"), stack("pool_s"), stack("k_p"), stack("v_p"), stack("k_s"), stack("v_s"),
            stack("conv_p"), stack("conv_s"), stack("st_p"), stack("st_s"))
```

```python
import functools
import math

import jax
import jax.numpy as jnp
from jax import lax
from jax.experimental import pallas as pl
from jax.experimental.pallas import tpu as pltpu

F32 = jnp.float32
BF16 = jnp.bfloat16

NORM_EPS = 1e-6
POOL_WINDOWS = (2, 4, 8, 16)
POOL_HIST = max(POOL_WINDOWS) - 1
POOL_PAD = 16
SB_HEADS = 8
SB_HEAD_DIM = 64
SB_BAND = 32
DN_HEADS = 8
DN_DIM = 128
CONV_WIDTH = 4
CONV_HIST = CONV_WIDTH - 1
CONV_PAD = 8
DN_CHUNK = 64
LANES = 128
MXU_DIM = 256
PAGE = 128
VMEM_LIMIT_BYTES = 48 * 1024 * 1024


def _cparams(*sem):
    return pltpu.CompilerParams(dimension_semantics=sem, vmem_limit_bytes=VMEM_LIMIT_BYTES)


def _dot(a, b):
    return jnp.dot(a, b, preferred_element_type=F32)


def _dot_nt(a, b):
    return lax.dot_general(a, b, (((1,), (1,)), ((), ())), preferred_element_type=F32)


def _dot_tn(a, b):
    return lax.dot_general(a, b, (((0,), (0,)), ((), ())), preferred_element_type=F32)


def _split2(x):
    hi = x.astype(BF16)
    return hi, (x - hi.astype(F32)).astype(BF16)


def _split3(x):
    hi = x.astype(BF16)
    r = x - hi.astype(F32)
    mid = r.astype(BF16)
    return hi, mid, (r - mid.astype(F32)).astype(BF16)


def _dot_exact_rhs(a, b_bf16):
    hi, mid, lo = _split3(a)
    return _dot(hi, b_bf16) + _dot(mid, b_bf16) + _dot(lo, b_bf16)


def _dot_exact_lhs(a_bf16, b):
    hi, mid, lo = _split3(b)
    return _dot(a_bf16, hi) + _dot(a_bf16, mid) + _dot(a_bf16, lo)


def _dot_f32(a, b):
    ah, al = _split2(a)
    bh, bl = _split2(b)
    return _dot(ah, bh) + (_dot(ah, bl) + _dot(al, bh))


def _sigmoid(x):
    return 1.0 / (1.0 + jnp.exp2(x * (-math.log2(math.e))))


def _silu(x):
    return x * _sigmoid(x)


def _softplus_neg_abs(x):
    return jnp.log1p(jnp.exp(-jnp.abs(x)))


def _rms_rows(x, g):
    return x * lax.rsqrt(jnp.mean(x * x, axis=-1, keepdims=True) + NORM_EPS) * g


def _row_tile(m, want):
    t = min(m, want)
    assert m % t == 0, (m, t)
    return t


def _even_proj_kernel(x_ref, g_ref, w_ref, seg_ref, qg_ref, kg_ref,
                      u_ref, q_ref, k_ref, v_ref, kb_ref, vb_ref, gate_ref, *, feature_major):
    xn = _rms_rows(x_ref[...], g_ref[...]).astype(BF16)
    pw = u_ref.shape[-1]

    def proj(c0, c1):
        return _dot(xn, w_ref[:, c0:c1])

    def head_norm(y, gain):
        sq = (y * y).astype(BF16)
        ms = jnp.concatenate([_dot(sq[:, c0:c0 + MXU_DIM], seg_ref[c0:c0 + MXU_DIM, c0:c0 + MXU_DIM])
                              for c0 in range(0, pw, MXU_DIM)], axis=1) * (1.0 / SB_HEAD_DIM)
        return y * lax.rsqrt(ms + NORM_EPS) * gain

    u_ref[...] = proj(0, pw)
    q = head_norm(proj(pw, 2 * pw), qg_ref[...])
    q_ref[...] = (q * (SB_HEAD_DIM ** -0.5)).astype(BF16)
    k = head_norm(proj(2 * pw, 3 * pw), kg_ref[...])
    kb_ref[...] = k.astype(BF16)
    v = proj(3 * pw, 4 * pw)
    vb_ref[...] = v.astype(BF16)
    if feature_major:
        k_ref[0] = k.T
        v_ref[0] = v.T
    else:
        k_ref[...] = k
        v_ref[...] = v
    gate_ref[...] = proj(4 * pw, 6 * pw)


def _even_proj(x2d, norm_g, w_in, q_g, k_g, seq_len):
    m, d = x2d.shape
    pw = SB_HEADS * SB_HEAD_DIM
    assert w_in.shape == (d, 6 * pw)
    tm = _row_tile(m, 512)
    feature_major = seq_len % tm == 0
    lane_head = jnp.arange(pw) // SB_HEAD_DIM
    seg = (lane_head[:, None] == lane_head[None, :]).astype(BF16)
    row = lambda i: (i, 0)
    fixed = lambda i: (0, 0)
    outs = [((m, pw), F32), ((m, pw), BF16), ((m, pw), F32), ((m, pw), F32), ((m, pw), BF16), ((m, pw), BF16),
            ((m, 2 * pw), F32)]
    out_specs = [pl.BlockSpec((tm, s[-1]), row) for s, _ in outs]
    out_shape = [jax.ShapeDtypeStruct(s, t) for s, t in outs]
    if feature_major:
        tiles = seq_len // tm
        for i in (2, 3):
            out_specs[i] = pl.BlockSpec((1, pw, tm), lambda j: (j // tiles, 0, j % tiles))
            out_shape[i] = jax.ShapeDtypeStruct((m // seq_len, pw, seq_len), F32)
    return pl.pallas_call(
        functools.partial(_even_proj_kernel, feature_major=feature_major),
        grid=(m // tm,),
        in_specs=[pl.BlockSpec((tm, d), row), pl.BlockSpec((1, d), fixed), pl.BlockSpec((d, 6 * pw), fixed),
                  pl.BlockSpec((pw, pw), fixed), pl.BlockSpec((1, pw), fixed), pl.BlockSpec((1, pw), fixed)],
        out_specs=out_specs,
        out_shape=out_shape,
        compiler_params=_cparams("parallel"),
        name="even_in_proj",
    )(x2d, norm_g[None, :], w_in.astype(BF16), seg,
      jnp.tile(q_g, SB_HEADS)[None, :], jnp.tile(k_g, SB_HEADS)[None, :])


def _seqs_per_step(b, length):
    bt = 8 if length <= 8 else 1
    assert b % bt == 0
    return bt


def _pool_group(ext, gi, pos, wp_ref, scale_ref, tl):
    w = POOL_WINDOWS[gi]
    cols = slice(gi * LANES, (gi + 1) * LANES)
    if tl % 8 == 0:
        block = ext[0:POOL_PAD + tl, cols]
        win = block
        span = 1
        while span < w:
            win = win + pltpu.roll(win, span, 0)
            span *= 2
        win, cur = win[POOL_PAD:], block[POOL_PAD:]
    else:
        cur = ext[POOL_PAD:POOL_PAD + tl, cols]
        win = cur
        for i in range(1, w):
            win = win + ext[POOL_PAD - i:POOL_PAD - i + tl, cols]
    cnt = jnp.minimum(pos + 1, w).astype(F32)
    d = win / cnt - cur
    return _dot(d.astype(BF16), wp_ref[gi]) * scale_ref[:, cols]


def _pool_kernel(u_ref, buf_ref, wp_ref, scale_ref, y_ref, nb_ref, ext_ref, *, start, tl, bt):
    l = pl.program_id(1)
    lo = POOL_PAD - POOL_HIST
    pos = start + l * tl + lax.broadcasted_iota(jnp.int32, (tl, 1), 0)
    for s in range(bt):
        ext = ext_ref.at[s]

        @pl.when(l == 0)
        def _():
            ext[0:lo, :] = jnp.zeros((lo, ext.shape[-1]), F32)
            ext[lo:POOL_PAD, :] = buf_ref[s]

        @pl.when(l > 0)
        def _():
            ext[lo:POOL_PAD, :] = ext[tl + lo:tl + POOL_PAD, :]

        ext[POOL_PAD:POOL_PAD + tl, :] = u_ref[s]
        for gi in range(len(POOL_WINDOWS)):
            y_ref[s, :, gi * LANES:(gi + 1) * LANES] = _pool_group(ext, gi, pos, wp_ref, scale_ref, tl)

        @pl.when(l == pl.num_programs(1) - 1)
        def _():
            nb_ref[s] = ext[tl + lo:tl + POOL_PAD, :]


def _pool_mix(u, buf, start, w_pool, scale):
    b, length, p = u.shape
    assert p == len(POOL_WINDOWS) * LANES and buf.shape == (b, POOL_HIST, p)
    tl = _row_tile(length, 512)
    assert tl == length or tl >= POOL_PAD
    bt = _seqs_per_step(b, length)
    seq = lambda i, l: (i, l, 0)
    per_b = lambda i, l: (i, 0, 0)
    return pl.pallas_call(
        functools.partial(_pool_kernel, start=start, tl=tl, bt=bt),
        grid=(b // bt, length // tl),
        in_specs=[pl.BlockSpec((bt, tl, p), seq), pl.BlockSpec((bt, POOL_HIST, p), per_b),
                  pl.BlockSpec(w_pool.shape, lambda i, l: (0, 0, 0)), pl.BlockSpec((1, p), lambda i, l: (0, 0))],
        out_specs=[pl.BlockSpec((bt, tl, p), seq), pl.BlockSpec((bt, POOL_HIST, p), per_b)],
        out_shape=[jax.ShapeDtypeStruct((b, length, p), F32), jax.ShapeDtypeStruct((b, POOL_HIST, p), F32)],
        scratch_shapes=[pltpu.VMEM((bt, POOL_PAD + tl, p), F32)],
        compiler_params=_cparams("parallel", "arbitrary"),
        name="pool_mix",
    )(u, buf, w_pool.astype(BF16), scale[None, :])


def _log_beta_keep(z):
    m = jnp.minimum(z, 0.0)
    d = m - z
    s = jnp.log(1.0 + jnp.exp(m + d))
    return m - s, d - s


def _stick_break_tiles(zs, carries, tri, masks, chained=False):
    n = range(len(zs))
    log_beta, log_keep = zip(*[_log_beta_keep(z) for z in zs])
    log_keep = [lk if m is None else jnp.where(m, lk, 0.0) for lk, m in zip(log_keep, masks)]
    totals = [jnp.sum(lk, axis=-1, keepdims=True) for lk in log_keep]
    if chained:
        carry_in = [carries[0]]
        for i in n:
            carry_in.append(carry_in[i] + totals[i])
        new_carries = carry_in[-1:]
    else:
        carry_in = carries
        new_carries = [carries[i] + totals[i] for i in n]
    suffix = [_dot(lk.astype(BF16), tri) + carry_in[i] for i, lk in enumerate(log_keep)]
    a = [jnp.exp(log_beta[i] + suffix[i]) for i in n]
    a = [x if m is None else jnp.where(m, x, 0.0) for x, m in zip(a, masks)]
    return [x.astype(BF16) for x in a], new_carries


def _later_key_matrix(n):
    return (lax.broadcasted_iota(jnp.int32, (n, n), 0) > lax.broadcasted_iota(jnp.int32, (n, n), 1)).astype(BF16)


def _sb_prompt_kernel(bias_ref, q_ref, k_ref, v_ref, o_ref, acc_ref, carry_ref, tot_ref, lb_ref, lk_ref, a_ref, *, t):
    qi = pl.program_id(1)
    heads = range(SB_HEADS)
    low_lanes = lax.broadcasted_iota(jnp.int32, (t, LANES), 1) < SB_HEAD_DIM
    tri = _later_key_matrix(t)
    causal = lax.broadcasted_iota(jnp.int32, (t, t), 1) < lax.broadcasted_iota(jnp.int32, (t, t), 0)
    pair_cols = lambda h: slice((h // 2) * LANES, (h // 2 + 1) * LANES)
    qh = []
    for h in heads:
        q_pair = q_ref[0, :, pair_cols(h)]
        qh.append(jnp.where(low_lanes if h % 2 == 0 else jnp.logical_not(low_lanes), q_pair, jnp.zeros_like(q_pair)))

    bands = [slice(r, r + SB_BAND) for r in range(0, t, SB_BAND)]

    def tiles(kb, mask):
        rows = pl.ds(pl.multiple_of(kb * t, t), t)
        for h in heads:
            z = _dot_nt(qh[h], k_ref[0, rows, pair_cols(h)]) + bias_ref[h]
            for band in bands:
                log_beta, log_keep = _log_beta_keep(z[band])
                if mask is not None:
                    log_keep = jnp.where(mask[band], log_keep, 0.0)
                lb_ref[h, band, :] = log_beta
                lk_ref[h, band, :] = log_keep.astype(BF16)
                tot_ref[h, band, :] = jnp.sum(log_keep, axis=-1, keepdims=True)
        for h in heads:
            suffix = _dot(lk_ref[h], tri)
            for band in bands:
                carry = carry_ref[h, band, :]
                a = jnp.exp(lb_ref[h, band, :] + (suffix[band] + carry))
                if mask is not None:
                    a = jnp.where(mask[band], a, 0.0)
                a_ref[h, band, :] = a.astype(BF16)
                carry_ref[h, band, :] = carry + tot_ref[h, band, :]
        for h in heads:
            acc_ref[h] += _dot(a_ref[h], v_ref[0, rows, pair_cols(h)])

    acc_ref[...] = jnp.zeros(acc_ref.shape, F32)
    carry_ref[...] = jnp.zeros(carry_ref.shape, F32)
    tiles(qi, causal)

    def earlier(j, _):
        tiles(qi - 1 - j, None)
        return 0

    lax.fori_loop(0, qi, earlier, 0)
    for h in range(0, SB_HEADS, 2):
        o_ref[0, :, pair_cols(h)] = jnp.where(low_lanes, acc_ref[h], acc_ref[h + 1])


def _sb_prompt(q, k, v, bias):
    b, length, width = q.shape
    assert width == SB_HEADS * SB_HEAD_DIM and 2 * SB_HEAD_DIM == LANES
    t = _row_tile(length, 256)
    q_blk = lambda i, j: (i, j, 0)
    kv_blk = lambda i, j: (i, 0, 0)
    return pl.pallas_call(
        functools.partial(_sb_prompt_kernel, t=t),
        grid=(b, length // t),
        in_specs=[pl.BlockSpec(memory_space=pltpu.SMEM), pl.BlockSpec((1, t, width), q_blk),
                  pl.BlockSpec((1, length, width), kv_blk), pl.BlockSpec((1, length, width), kv_blk)],
        out_specs=pl.BlockSpec((1, t, width), q_blk),
        scratch_shapes=[pltpu.VMEM((SB_HEADS, t, LANES), F32), pltpu.VMEM((SB_HEADS, t, 1), F32),
                        pltpu.VMEM((SB_HEADS, t, 1), F32), pltpu.VMEM((SB_HEADS, t, t), F32),
                        pltpu.VMEM((SB_HEADS, t, t), BF16), pltpu.VMEM((SB_HEADS, t, t), BF16)],
        out_shape=jax.ShapeDtypeStruct((b, length, width), F32),
        compiler_params=_cparams("parallel", "arbitrary"),
        name="stick_break_prompt",
    )(bias, q, k, v)


def _sb_sample_kernel(pt_ref, q_ref, bias_ref, kn_ref, vn_ref, *refs, n_pages, n_new):
    k_refs, v_refs, o_ref = refs[:n_pages], refs[n_pages:2 * n_pages], refs[2 * n_pages]
    kpad_ref, vpad_ref = refs[2 * n_pages + 1:]
    rows = SB_HEADS * n_new
    width = SB_HEADS * SB_HEAD_DIM
    row_i = lax.broadcasted_iota(jnp.int32, (rows, width), 0)
    lane_i = lax.broadcasted_iota(jnp.int32, (rows, width), 1)
    own_head = (row_i // n_new) == (lane_i // SB_HEAD_DIM)
    qbd = jnp.where(own_head, q_ref[0], jnp.zeros_like(q_ref[0]))
    bias = bias_ref[...]
    tri = _later_key_matrix(PAGE)

    kpad_ref[...] = jnp.zeros(kpad_ref.shape, BF16)
    vpad_ref[...] = jnp.zeros(vpad_ref.shape, BF16)
    kpad_ref[0:kn_ref.shape[1], :] = kn_ref[0].astype(BF16)
    vpad_ref[0:vn_ref.shape[1], :] = vn_ref[0].astype(BF16)
    qpos = lax.broadcasted_iota(jnp.int32, (rows, PAGE), 0) % n_new
    kpos = lax.broadcasted_iota(jnp.int32, (rows, PAGE), 1)

    pages = list(reversed(range(n_pages)))
    zs = [_dot_nt(qbd, kpad_ref[...]) + bias] + [_dot(qbd, k_refs[p][0].astype(BF16)) + bias for p in pages]
    a, _ = _stick_break_tiles(zs, [jnp.zeros((rows, 1), F32)], tri, [kpos < qpos] + [None] * n_pages, chained=True)
    out = _dot(a[0], vpad_ref[...])
    for i, p in enumerate(pages):
        out = out + _dot_nt(a[i + 1], v_refs[p][0].astype(BF16))
    out = jnp.where(own_head, out, 0.0)
    acc = out[0:n_new]
    for h in range(1, SB_HEADS):
        acc = acc + out[h * n_new:(h + 1) * n_new]
    o_ref[0] = acc


def _sb_sample(q, k_new, v_new, bias, cache_k, cache_v, page_table):
    b, n_new, width = q.shape
    n_pages = page_table.shape[1]
    n_phys = cache_k.shape[0]
    assert cache_k.shape[1:] == (PAGE, SB_HEADS, SB_HEAD_DIM) and n_new <= 8
    ck = jnp.transpose(cache_k, (0, 2, 3, 1)).reshape(n_phys, width, PAGE)
    cv = jnp.transpose(cache_v, (0, 2, 3, 1)).reshape(n_phys, width, PAGE)
    rows = SB_HEADS * n_new
    pad = ((0, 0), (0, 8 - n_new), (0, 0))
    per_b = lambda i, pt: (i, 0, 0)
    page_specs = [pl.BlockSpec((1, width, PAGE), functools.partial(lambda i, pt, p: (pt[i, p], 0, 0), p=p))
                  for p in range(n_pages)]
    return pl.pallas_call(
        functools.partial(_sb_sample_kernel, n_pages=n_pages, n_new=n_new),
        grid_spec=pltpu.PrefetchScalarGridSpec(
            num_scalar_prefetch=1,
            grid=(b,),
            in_specs=[pl.BlockSpec((1, rows, width), per_b), pl.BlockSpec((rows, 1), lambda i, pt: (0, 0)),
                      pl.BlockSpec((1, 8, width), per_b), pl.BlockSpec((1, 8, width), per_b)] + page_specs + page_specs,
            out_specs=pl.BlockSpec((1, n_new, width), per_b),
            scratch_shapes=[pltpu.VMEM((PAGE, width), BF16), pltpu.VMEM((PAGE, width), BF16)]),
        out_shape=jax.ShapeDtypeStruct((b, n_new, width), F32),
        compiler_params=_cparams("parallel"),
        name="stick_break_sample",
    )(page_table, jnp.tile(q, (1, SB_HEADS, 1)), jnp.repeat(bias, n_new)[:, None],
      jnp.pad(k_new, pad), jnp.pad(v_new, pad), *([ck] * n_pages), *([cv] * n_pages))


def _even_out_kernel(h_ref, py_ref, sb_ref, gate_ref, w_ref, y_ref):
    pw = py_ref.shape[-1]
    m_pool = (py_ref[...] * _silu(gate_ref[:, 0:pw])).astype(BF16)
    m_sb = (sb_ref[...] * _silu(gate_ref[:, pw:2 * pw])).astype(BF16)
    y_ref[...] = h_ref[...] + (_dot(m_pool, w_ref[0:pw, :]) + _dot(m_sb, w_ref[pw:2 * pw, :]))


def _even_out(h2d, pool_y, sb, gate, w_out):
    m, d = h2d.shape
    pw = pool_y.shape[-1]
    tm = _row_tile(m, 512)
    row = lambda i: (i, 0)
    return pl.pallas_call(
        _even_out_kernel,
        grid=(m // tm,),
        in_specs=[pl.BlockSpec((tm, d), row), pl.BlockSpec((tm, pw), row), pl.BlockSpec((tm, pw), row),
                  pl.BlockSpec((tm, 2 * pw), row), pl.BlockSpec((2 * pw, d), lambda i: (0, 0))],
        out_specs=pl.BlockSpec((tm, d), row),
        out_shape=jax.ShapeDtypeStruct((m, d), F32),
        compiler_params=_cparams("parallel"),
        name="even_out_proj",
    )(h2d, pool_y, sb, gate, w_out.astype(BF16))


def _even_head(h, past, norm_g, w_in, q_g, k_g, sb_bias):
    b, length, d = h.shape
    u, q, k, v, kb, vb, gate = _even_proj(h.reshape(b * length, d), norm_g, w_in, q_g, k_g, length)
    seq = lambda a: a.reshape(b, length, a.shape[-1])
    if past is None:
        sb = _sb_prompt(seq(q), seq(kb), seq(vb), sb_bias)
    else:
        sb = _sb_sample(seq(q), seq(k), seq(v), sb_bias, *past)
    if k.ndim == 3:
        heads = lambda a: jnp.transpose(a.reshape(b, SB_HEADS, SB_HEAD_DIM, length), (0, 3, 1, 2))
    else:
        heads = lambda a: a.reshape(b, length, SB_HEADS, SB_HEAD_DIM)
    return (seq(u), sb, gate), heads(k), heads(v)


def _even_tail(h, parts, start, pool_buf, w_pool, pool_scale, w_out):
    b, length, d = h.shape
    u, sb, gate = parts
    pool_y, new_buf = _pool_mix(u, pool_buf, start, w_pool, pool_scale)
    y = _even_out(h.reshape(b * length, d), pool_y.reshape(b * length, -1), sb.reshape(b * length, -1), gate, w_out)
    return y.reshape(b, length, d), new_buf


def _even_layer(h, start, pool_buf, past, norm_g, w_in, w_pool, pool_scale, q_g, k_g, sb_bias, w_out):
    parts, k, v = _even_head(h, past, norm_g, w_in, q_g, k_g, sb_bias)
    y, new_buf = _even_tail(h, parts, start, pool_buf, w_pool, pool_scale, w_out)
    return y, new_buf, k, v


def _odd_proj_kernel(x_ref, g_ref, w_ref, wba_ref, qkv_ref, z_ref, ba_ref, *, chunk):
    xn = _rms_rows(x_ref[...], g_ref[...]).astype(BF16)
    n_qkv = qkv_ref.shape[-1]
    for c0 in range(0, n_qkv, chunk):
        qkv_ref[:, c0:c0 + chunk] = _dot(xn, w_ref[:, c0:c0 + chunk])
    for c0 in range(0, z_ref.shape[-1], chunk):
        z_ref[:, c0:c0 + chunk] = _dot(xn, w_ref[:, n_qkv + c0:n_qkv + c0 + chunk])
    ba_ref[...] = _dot(xn, wba_ref[...])


def _odd_weights(w_in):
    n_main = 4 * DN_HEADS * DN_DIM
    assert w_in.shape[1] == n_main + 2 * DN_HEADS
    w_main = w_in[:, :n_main].astype(BF16)
    w_ba = jnp.pad(w_in[:, n_main:], ((0, 0), (0, LANES - 2 * DN_HEADS))).astype(BF16)
    return w_main, w_ba


def _odd_proj(x2d, norm_g, w_in):
    m, d = x2d.shape
    n_qkv, n_z = 3 * DN_HEADS * DN_DIM, DN_HEADS * DN_DIM
    tm = _row_tile(m, 256)
    w_main, w_ba = _odd_weights(w_in)
    row = lambda i: (i, 0)
    fixed = lambda i: (0, 0)
    return pl.pallas_call(
        functools.partial(_odd_proj_kernel, chunk=512),
        grid=(m // tm,),
        in_specs=[pl.BlockSpec((tm, d), row), pl.BlockSpec((1, d), fixed), pl.BlockSpec(w_main.shape, fixed),
                  pl.BlockSpec(w_ba.shape, fixed)],
        out_specs=[pl.BlockSpec((tm, n_qkv), row), pl.BlockSpec((tm, n_z), row), pl.BlockSpec((tm, LANES), row)],
        out_shape=[jax.ShapeDtypeStruct((m, n_qkv), F32), jax.ShapeDtypeStruct((m, n_z), F32),
                   jax.ShapeDtypeStruct((m, LANES), F32)],
        compiler_params=_cparams("parallel"),
        name="odd_in_proj",
    )(x2d, norm_g[None, :], w_main, w_ba)


def _conv_head_tile(ext, cw_ref, c, tl):
    cols = slice(c * LANES, (c + 1) * LANES)
    if tl % 8 == 0:
        block = ext[0:CONV_PAD + tl, cols]
        conv = block[CONV_PAD:] * cw_ref[CONV_HIST:CONV_WIDTH, cols]
        for j in range(CONV_HIST):
            conv = conv + pltpu.roll(block, CONV_HIST - j, 0)[CONV_PAD:] * cw_ref[j:j + 1, cols]
    else:
        lo = CONV_PAD - CONV_HIST
        conv = ext[lo:lo + tl, cols] * cw_ref[0:1, cols]
        for j in range(1, CONV_WIDTH):
            conv = conv + ext[lo + j:lo + j + tl, cols] * cw_ref[j:j + 1, cols]
    y = _silu(conv)
    if c < 2 * DN_HEADS:
        inv_norm = lax.rsqrt(jnp.sum(y * y, axis=-1, keepdims=True) + NORM_EPS)
        if c < DN_HEADS:
            inv_norm = inv_norm * (DN_DIM ** -0.5)
        y = y * inv_norm
    return y


def _decay_gates(ba, ab_ref):
    x = ba + ab_ref[1:2, :]
    g = -jnp.exp(ab_ref[0:1, :]) * (jnp.maximum(x, 0.0) + _softplus_neg_abs(x))
    is_beta = lax.broadcasted_iota(jnp.int32, ba.shape, 1) < DN_HEADS
    return jnp.where(is_beta, _sigmoid(ba), g)


def _conv_kernel(x_ref, buf_ref, cw_ref, ba_ref, ab_ref, q_ref, k_ref, v_ref, gb_ref, nb_ref, ext_ref, *, tl, bt):
    l = pl.program_id(1)
    lo = CONV_PAD - CONV_HIST
    width = ext_ref.shape[-1]
    outs = (q_ref, k_ref, v_ref)
    for s in range(bt):
        ext = ext_ref.at[s]

        @pl.when(l == 0)
        def _():
            ext[0:lo, :] = jnp.zeros((lo, width), F32)
            ext[lo:CONV_PAD, :] = buf_ref[s]

        @pl.when(l > 0)
        def _():
            ext[lo:CONV_PAD, :] = ext[tl + lo:tl + CONV_PAD, :]

        ext[CONV_PAD:CONV_PAD + tl, :] = x_ref[s]
        for c in range(width // LANES):
            outs[c // DN_HEADS][s, :, _head_cols(c % DN_HEADS)] = _conv_head_tile(ext, cw_ref, c, tl)
        gb_ref[s] = _decay_gates(ba_ref[s], ab_ref)

        @pl.when(l == pl.num_programs(1) - 1)
        def _():
            nb_ref[s] = ext[tl + lo:tl + CONV_PAD, :]


def _odd_proj_conv_kernel(x_ref, g_ref, w_ref, wba_ref, buf_ref, cw_ref, ab_ref,
                          q_ref, k_ref, v_ref, z_ref, gb_ref, nb_ref, ext_ref, *, chunk, tiles_per_seq):
    i = pl.program_id(0)
    tm = x_ref.shape[0]
    lo = CONV_PAD - CONV_HIST
    n_qkv = ext_ref.shape[-1]
    seq_start = i % tiles_per_seq == 0

    @pl.when(seq_start)
    def _():
        ext_ref[0:lo, :] = jnp.zeros((lo, n_qkv), F32)
        ext_ref[lo:CONV_PAD, :] = buf_ref[0]

    @pl.when(jnp.logical_not(seq_start))
    def _():
        ext_ref[lo:CONV_PAD, :] = ext_ref[tm + lo:tm + CONV_PAD, :]

    xn = _rms_rows(x_ref[...], g_ref[...]).astype(BF16)
    _project_conv(xn, w_ref, wba_ref, cw_ref, ab_ref, ext_ref, q_ref, k_ref, v_ref, z_ref, gb_ref, chunk)

    @pl.when(i % tiles_per_seq == tiles_per_seq - 1)
    def _():
        nb_ref[0] = ext_ref[tm + lo:tm + CONV_PAD, :]


def _project_conv(xn, w_ref, wba_ref, cw_ref, ab_ref, ext_ref, q_ref, k_ref, v_ref, z_ref, gb_ref, chunk):
    tm = xn.shape[0]
    n_qkv = ext_ref.shape[-1]
    outs = (q_ref, k_ref, v_ref)
    for c0 in range(0, n_qkv, chunk):
        ext_ref[CONV_PAD:CONV_PAD + tm, c0:c0 + chunk] = _dot(xn, w_ref[:, c0:c0 + chunk])
        for c in range(c0 // LANES, (c0 + chunk) // LANES):
            outs[c // DN_HEADS][:, _head_cols(c % DN_HEADS)] = _conv_head_tile(ext_ref, cw_ref, c, tm)
    for c0 in range(0, z_ref.shape[-1], chunk):
        z_ref[:, c0:c0 + chunk] = _dot(xn, w_ref[:, n_qkv + c0:n_qkv + c0 + chunk])
    gb_ref[...] = _decay_gates(_dot(xn, wba_ref[...]), ab_ref)


def _even_tail_odd_head_kernel(h_ref, u_ref, sb_ref, gate_ref, pbuf_ref, wp_ref, pscale_ref, wout_ref,
                               g_ref, w_ref, wba_ref, cbuf_ref, cw_ref, ab_ref,
                               y_ref, npb_ref, q_ref, k_ref, v_ref, z_ref, gb_ref, ncb_ref,
                               pext_ref, mix_ref, cext_ref, *, start, chunk, tiles_per_seq):
    i = pl.program_id(0)
    tm = h_ref.shape[0]
    l = i % tiles_per_seq
    plo = POOL_PAD - POOL_HIST
    clo = CONV_PAD - CONV_HIST
    pw = u_ref.shape[-1]

    @pl.when(l == 0)
    def _():
        pext_ref[0:plo, :] = jnp.zeros((plo, pw), F32)
        pext_ref[plo:POOL_PAD, :] = pbuf_ref[0]
        cext_ref[0:clo, :] = jnp.zeros((clo, cext_ref.shape[-1]), F32)
        cext_ref[clo:CONV_PAD, :] = cbuf_ref[0]

    @pl.when(l > 0)
    def _():
        pext_ref[plo:POOL_PAD, :] = pext_ref[tm + plo:tm + POOL_PAD, :]
        cext_ref[clo:CONV_PAD, :] = cext_ref[tm + clo:tm + CONV_PAD, :]

    pext_ref[POOL_PAD:POOL_PAD + tm, :] = u_ref[...]
    pos = start + l * tm + lax.broadcasted_iota(jnp.int32, (tm, 1), 0)
    for gi in range(len(POOL_WINDOWS)):
        cols = slice(gi * LANES, (gi + 1) * LANES)
        pooled = _pool_group(pext_ref, gi, pos, wp_ref, pscale_ref, tm)
        mix_ref[:, cols] = (pooled * _silu(gate_ref[:, cols])).astype(BF16)
    for c0 in range(0, pw, LANES):
        cols = slice(pw + c0, pw + c0 + LANES)
        mix_ref[:, cols] = (sb_ref[:, c0:c0 + LANES] * _silu(gate_ref[:, cols])).astype(BF16)
    y = h_ref[...] + _dot(mix_ref[...], wout_ref[...])
    y_ref[...] = y

    xn = _rms_rows(y, g_ref[...]).astype(BF16)
    _project_conv(xn, w_ref, wba_ref, cw_ref, ab_ref, cext_ref, q_ref, k_ref, v_ref, z_ref, gb_ref, chunk)

    @pl.when(l == tiles_per_seq - 1)
    def _():
        npb_ref[0] = pext_ref[tm + plo:tm + POOL_PAD, :]
        ncb_ref[0] = cext_ref[tm + clo:tm + CONV_PAD, :]


def _decay_params(a_log, dt_bias):
    park = lambda vec: jnp.pad(vec, (DN_HEADS, LANES - 2 * DN_HEADS))
    return jnp.stack([park(a_log), park(dt_bias)])


def _odd_proj_conv(x2d, seq_len, conv_buf, norm_g, w_in, conv_w, a_log, dt_bias):
    m, d = x2d.shape
    n = DN_HEADS * DN_DIM
    n_qkv = 3 * n
    tm = _row_tile(m, 256)
    assert seq_len % tm == 0 and tm >= CONV_PAD and conv_buf.shape == (m // seq_len, CONV_HIST, n_qkv)
    tiles_per_seq = seq_len // tm
    w_main, w_ba = _odd_weights(w_in)
    row = lambda i: (i, 0)
    fixed = lambda i: (0, 0)
    per_seq = lambda i: (i // tiles_per_seq, 0, 0)
    return pl.pallas_call(
        functools.partial(_odd_proj_conv_kernel, chunk=512, tiles_per_seq=tiles_per_seq),
        grid=(m // tm,),
        in_specs=[pl.BlockSpec((tm, d), row), pl.BlockSpec((1, d), fixed), pl.BlockSpec(w_main.shape, fixed),
                  pl.BlockSpec(w_ba.shape, fixed), pl.BlockSpec((1, CONV_HIST, n_qkv), per_seq),
                  pl.BlockSpec((CONV_WIDTH, n_qkv), fixed), pl.BlockSpec((2, LANES), fixed)],
        out_specs=[pl.BlockSpec((tm, n), row)] * 4 + [pl.BlockSpec((tm, LANES), row),
                                                      pl.BlockSpec((1, CONV_HIST, n_qkv), per_seq)],
        out_shape=[jax.ShapeDtypeStruct((m, n), F32)] * 4
        + [jax.ShapeDtypeStruct((m, LANES), F32), jax.ShapeDtypeStruct(conv_buf.shape, F32)],
        scratch_shapes=[pltpu.VMEM((CONV_PAD + tm, n_qkv), F32)],
        compiler_params=_cparams("arbitrary"),
        name="odd_in_proj_conv",
    )(x2d, norm_g[None, :], w_main, w_ba, conv_buf, conv_w, _decay_params(a_log, dt_bias))


def _conv_gates(qkv, conv_buf, conv_w, ba, a_log, dt_bias):
    b, length, width = qkv.shape
    n = DN_HEADS * DN_DIM
    assert width == 3 * n and conv_buf.shape == (b, CONV_HIST, width)
    tl = _row_tile(length, 256)
    assert tl == length or tl >= CONV_PAD
    ab = _decay_params(a_log, dt_bias)
    bt = _seqs_per_step(b, length)
    seq = lambda i, l: (i, l, 0)
    per_b = lambda i, l: (i, 0, 0)
    fixed = lambda i, l: (0, 0)
    return pl.pallas_call(
        functools.partial(_conv_kernel, tl=tl, bt=bt),
        grid=(b // bt, length // tl),
        in_specs=[pl.BlockSpec((bt, tl, width), seq), pl.BlockSpec((bt, CONV_HIST, width), per_b),
                  pl.BlockSpec((CONV_WIDTH, width), fixed), pl.BlockSpec((bt, tl, LANES), seq),
                  pl.BlockSpec((2, LANES), fixed)],
        out_specs=[pl.BlockSpec((bt, tl, n), seq)] * 3 + [pl.BlockSpec((bt, tl, LANES), seq),
                                                           pl.BlockSpec((bt, CONV_HIST, width), per_b)],
        out_shape=[jax.ShapeDtypeStruct((b, length, n), F32)] * 3
        + [jax.ShapeDtypeStruct((b, length, LANES), F32), jax.ShapeDtypeStruct((b, CONV_HIST, width), F32)],
        scratch_shapes=[pltpu.VMEM((bt, CONV_PAD + tl, width), F32)],
        compiler_params=_cparams("parallel", "arbitrary"),
        name="conv_norm_gates",
    )(qkv, conv_buf, conv_w, ba, ab)


def _unit_lower_inverses(lmats, c):
    eye = (lax.broadcasted_iota(jnp.int32, (c, c), 0) == lax.broadcasted_iota(jnp.int32, (c, c), 1)).astype(F32)
    powers = [-lm for lm in lmats]
    invs = [eye + p for p in powers]
    span = 2
    while span < c:
        powers = [_dot_f32(p, p) for p in powers]
        invs = [i + _dot_f32(i, p) for i, p in zip(invs, powers)]
        span *= 2
    return invs


def _head_cols(h):
    return slice(h * DN_DIM, (h + 1) * DN_DIM)


def _delta_local_kernel(q_ref, k_ref, v_ref, gb_ref, u_ref, wq_ref, kd_ref, att_ref, eg_ref, *, c, n_chunks, group):
    row = lax.broadcasted_iota(jnp.int32, (c, c), 0)
    col = lax.broadcasted_iota(jnp.int32, (c, c), 1)
    incl = row >= col
    strict = row > col
    prefix = incl.astype(BF16)
    eye_lanes = (lax.broadcasted_iota(jnp.int32, (LANES, LANES), 0)
                 == lax.broadcasted_iota(jnp.int32, (LANES, LANES), 1)).astype(BF16)
    units = [(g, h) for g in range(group) for h in range(DN_HEADS)]
    n = range(len(units))

    def chunks(step, _):
        ci = [step * group + g for g in range(group)]
        rows = [pl.ds(pl.multiple_of(x * c, c), c) for x in ci]
        gb = [gb_ref[r, :] for r in rows]
        gcs = [_dot_exact_lhs(prefix, x) for x in gb]
        gcs_t = []
        for x in gcs:
            h3, m3, l3 = _split3(x)
            gcs_t.append(_dot_nt(eye_lanes, h3) + _dot_nt(eye_lanes, m3) + _dot_nt(eye_lanes, l3))
        q = [q_ref[rows[g], _head_cols(h)] for g, h in units]
        k = [k_ref[rows[g], _head_cols(h)] for g, h in units]
        v = [v_ref[rows[g], _head_cols(h)] for g, h in units]
        beta = [gb[g][:, h:h + 1] for g, h in units]
        gc = [gcs[g][:, DN_HEADS + h:DN_HEADS + h + 1] for g, h in units]
        g_last = [gcs[g][c - 1:c, DN_HEADS + h:DN_HEADS + h + 1] for g, h in units]
        gc_row = [gcs_t[g][DN_HEADS + h:DN_HEADS + h + 1, :] for g, h in units]
        decay = [jnp.where(incl, jnp.exp(jnp.where(incl, gc[i] - gc_row[i], 0.0)), 0.0) for i in n]
        kb = [k[i] * beta[i] for i in n]
        k16 = [k[i].astype(BF16) for i in n]
        lmat = [jnp.where(strict, _dot_nt(kb[i].astype(BF16), k16[i]) * decay[i], 0.0) for i in n]
        tmat = [t.astype(BF16) for t in _unit_lower_inverses(lmat, c)]
        egc = [jnp.exp(gc[i]) for i in n]
        for i, (g, h) in enumerate(units):
            cols = _head_cols(h)
            u_ref[rows[g], cols] = _dot(tmat[i], (v[i] * beta[i]).astype(BF16))
            w = _dot(tmat[i], (kb[i] * egc[i]).astype(BF16))
            wq_ref[ci[g], :, cols] = jnp.concatenate([w, q[i] * egc[i]], axis=0).astype(BF16)
            kd_ref[rows[g], cols] = (k[i] * jnp.exp(g_last[i] - gc[i])).astype(BF16)
            att = _dot_nt(q[i].astype(BF16), k16[i]) * decay[i]
            if c < LANES:
                att = jnp.concatenate([att, jnp.zeros((c, LANES - c), F32)], axis=1)
            att_ref[rows[g], cols] = att.astype(BF16)
            eg_ref[ci[g], h:h + 1, :] = jnp.broadcast_to(jnp.exp(g_last[i]), (1, LANES))
        return 0

    assert n_chunks % group == 0
    lax.fori_loop(0, n_chunks // group, chunks, 0)


def _delta_scan_kernel(u_ref, wq_ref, kd_ref, att_ref, eg_ref, s0_ref, h_ref, z_ref, og_ref, wout_ref,
                       y_ref, s_out_ref, s_ref, o_ref, gated_ref, *, c, n_chunks, bt):
    l = pl.program_id(1)

    @pl.when(l == 0)
    def _():
        s_ref[...] = s0_ref[...]

    units = [(i, h) for i in range(bt) for h in range(DN_HEADS)]

    def chunk(ci, _):
        rows = pl.ds(pl.multiple_of(ci * c, c), c)
        s = [s_ref[i, h] for i, h in units]
        s16 = [x.astype(BF16) for x in s]
        r = [_dot(wq_ref[i, ci, :, _head_cols(h)], s16[n]) for n, (i, h) in enumerate(units)]
        v16 = [(u_ref[i, rows, _head_cols(h)] - r[n][0:c]).astype(BF16) for n, (i, h) in enumerate(units)]
        for n, (i, h) in enumerate(units):
            cols = _head_cols(h)
            att = att_ref[i, rows, h * DN_DIM:h * DN_DIM + c]
            o_ref[i, rows, cols] = r[n][c:2 * c] + _dot(att, v16[n])
            s_ref[i, h] = s[n] * eg_ref[i, ci, h:h + 1, :] + _dot_tn(kd_ref[i, rows, cols], v16[n])
        return 0

    lax.fori_loop(0, n_chunks, chunk, 0)
    for i in range(bt):
        for hd in range(DN_HEADS):
            cols = _head_cols(hd)
            gated_ref[:, cols] = (_rms_rows(o_ref[i, :, cols], og_ref[...]) * _silu(z_ref[i, :, cols])).astype(BF16)
        y_ref[i] = h_ref[i] + _dot(gated_ref[...], wout_ref[...])

    @pl.when(l == pl.num_programs(1) - 1)
    def _():
        s_out_ref[...] = s_ref[...]


def _gated_delta_out(q, k, v, gb, state, c, h, z, o_g, w_out):
    b, length, n = q.shape
    assert n == DN_HEADS * DN_DIM and state.shape == (b, DN_HEADS, DN_DIM, DN_DIM) and length % c == 0
    assert c % 8 == 0 and c <= LANES
    tokens = b * length
    flat = lambda a: a.reshape(tokens, a.shape[-1])
    tn = _row_tile(tokens, max(c, 256))
    row = lambda i: (i, 0)
    per_chunk = lambda i: (i, 0, 0)
    u, wq, kd, att, eg = pl.pallas_call(
        functools.partial(_delta_local_kernel, c=c, n_chunks=tn // c, group=2 if c >= DN_CHUNK else 4),
        grid=(tokens // tn,),
        in_specs=[pl.BlockSpec((tn, n), row)] * 3 + [pl.BlockSpec((tn, LANES), row)],
        out_specs=[pl.BlockSpec((tn, n), row), pl.BlockSpec((tn // c, 2 * c, n), per_chunk),
                   pl.BlockSpec((tn, n), row), pl.BlockSpec((tn, n), row),
                   pl.BlockSpec((tn // c, DN_HEADS, LANES), per_chunk)],
        out_shape=[jax.ShapeDtypeStruct((tokens, n), F32), jax.ShapeDtypeStruct((tokens // c, 2 * c, n), BF16),
                   jax.ShapeDtypeStruct((tokens, n), BF16), jax.ShapeDtypeStruct((tokens, n), BF16),
                   jax.ShapeDtypeStruct((tokens // c, DN_HEADS, LANES), F32)],
        compiler_params=_cparams("parallel"),
        name="delta_local",
    )(flat(q), flat(k), flat(v), flat(gb))

    n_seq_chunks = length // c
    tl = c * min(n_seq_chunks, 4)
    bt = 2 if n_seq_chunks > 1 else 4
    assert length % tl == 0 and b % bt == 0
    seq = lambda i, l: (i, l, 0)
    seq_chunks = lambda i, l: (i, l, 0, 0)
    per_b = lambda i, l: (i, 0, 0, 0)
    st_block = (bt,) + state.shape[1:]
    d = h.shape[-1]
    fixed = lambda i, l: (0, 0)
    return pl.pallas_call(
        functools.partial(_delta_scan_kernel, c=c, n_chunks=tl // c, bt=bt),
        grid=(b // bt, length // tl),
        in_specs=[pl.BlockSpec((bt, tl, n), seq), pl.BlockSpec((bt, tl // c, 2 * c, n), seq_chunks),
                  pl.BlockSpec((bt, tl, n), seq), pl.BlockSpec((bt, tl, n), seq),
                  pl.BlockSpec((bt, tl // c, DN_HEADS, LANES), seq_chunks), pl.BlockSpec(st_block, per_b),
                  pl.BlockSpec((bt, tl, d), seq), pl.BlockSpec((bt, tl, n), seq),
                  pl.BlockSpec((1, DN_DIM), fixed), pl.BlockSpec((n, d), fixed)],
        out_specs=[pl.BlockSpec((bt, tl, d), seq), pl.BlockSpec(st_block, per_b)],
        out_shape=[jax.ShapeDtypeStruct((b, length, d), F32), jax.ShapeDtypeStruct(state.shape, F32)],
        scratch_shapes=[pltpu.VMEM(st_block, F32), pltpu.VMEM((bt, tl, n), F32), pltpu.VMEM((tl, n), BF16)],
        compiler_params=_cparams("parallel", "arbitrary"),
        name="delta_scan_out",
    )(u.reshape(b, length, n), wq.reshape(b, n_seq_chunks, 2 * c, n), kd.reshape(b, length, n),
      att.reshape(b, length, n), eg.reshape(b, n_seq_chunks, DN_HEADS, LANES), state,
      h, z, o_g[None, :], w_out.astype(BF16))


def _whole_tile_sequences(b, length):
    return length % min(b * length, 256) == 0


def _odd_head(h, conv_buf, norm_g, w_in, conv_w, a_log, dt_bias):
    b, length, d = h.shape
    h2d = h.reshape(b * length, d)
    seq = lambda a: a.reshape(b, length, a.shape[-1])
    if _whole_tile_sequences(b, length):
        q, k, v, z, gb, new_buf = _odd_proj_conv(h2d, length, conv_buf, norm_g, w_in, conv_w, a_log, dt_bias)
        q, k, v, gb = seq(q), seq(k), seq(v), seq(gb)
    else:
        qkv, z, ba = _odd_proj(h2d, norm_g, w_in)
        q, k, v, gb, new_buf = _conv_gates(seq(qkv), conv_buf, conv_w, seq(ba), a_log, dt_bias)
    return (q, k, v, z, gb), new_buf


def _even_tail_odd_head(h, parts, start, pool_buf, w_pool, pool_scale, w_out_even,
                        conv_buf, norm_g, w_in, conv_w, a_log, dt_bias):
    b, length, d = h.shape
    m = b * length
    u, sb, gate = parts
    pw = u.shape[-1]
    n = DN_HEADS * DN_DIM
    n_qkv = 3 * n
    tm = _row_tile(m, 256)
    assert length % tm == 0 and tm >= POOL_PAD
    tiles_per_seq = length // tm
    w_main, w_ba = _odd_weights(w_in)
    row = lambda i: (i, 0)
    fixed = lambda i: (0, 0)
    per_seq = lambda i: (i // tiles_per_seq, 0, 0)
    y, new_pool, q, k, v, z, gb, new_conv = pl.pallas_call(
        functools.partial(_even_tail_odd_head_kernel, start=start, chunk=512, tiles_per_seq=tiles_per_seq),
        grid=(m // tm,),
        in_specs=[pl.BlockSpec((tm, d), row), pl.BlockSpec((tm, pw), row), pl.BlockSpec((tm, pw), row),
                  pl.BlockSpec((tm, 2 * pw), row), pl.BlockSpec((1, POOL_HIST, pw), per_seq),
                  pl.BlockSpec(w_pool.shape, lambda i: (0, 0, 0)), pl.BlockSpec((1, pw), fixed),
                  pl.BlockSpec((2 * pw, d), fixed), pl.BlockSpec((1, d), fixed), pl.BlockSpec(w_main.shape, fixed),
                  pl.BlockSpec(w_ba.shape, fixed), pl.BlockSpec((1, CONV_HIST, n_qkv), per_seq),
                  pl.BlockSpec((CONV_WIDTH, n_qkv), fixed), pl.BlockSpec((2, LANES), fixed)],
        out_specs=[pl.BlockSpec((tm, d), row), pl.BlockSpec((1, POOL_HIST, pw), per_seq)]
        + [pl.BlockSpec((tm, n), row)] * 4 + [pl.BlockSpec((tm, LANES), row),
                                              pl.BlockSpec((1, CONV_HIST, n_qkv), per_seq)],
        out_shape=[jax.ShapeDtypeStruct((m, d), F32), jax.ShapeDtypeStruct((b, POOL_HIST, pw), F32)]
        + [jax.ShapeDtypeStruct((m, n), F32)] * 4
        + [jax.ShapeDtypeStruct((m, LANES), F32), jax.ShapeDtypeStruct((b, CONV_HIST, n_qkv), F32)],
        scratch_shapes=[pltpu.VMEM((POOL_PAD + tm, pw), F32), pltpu.VMEM((tm, 2 * pw), BF16),
                        pltpu.VMEM((CONV_PAD + tm, n_qkv), F32)],
        compiler_params=_cparams("arbitrary"),
        name="even_tail_odd_head",
    )(h.reshape(m, d), u.reshape(m, pw), sb.reshape(m, pw), gate, pool_buf, w_pool.astype(BF16),
      pool_scale[None, :], w_out_even.astype(BF16), norm_g[None, :], w_main, w_ba, conv_buf, conv_w,
      _decay_params(a_log, dt_bias))
    seq = lambda a: a.reshape(b, length, a.shape[-1])
    return y.reshape(b, length, d), new_pool, (seq(q), seq(k), seq(v), z, seq(gb)), new_conv


def _odd_layer(h, conv_buf, state, norm_g, w_in, conv_w, a_log, dt_bias, o_g, w_out):
    parts, new_buf = _odd_head(h, conv_buf, norm_g, w_in, conv_w, a_log, dt_bias)
    y, new_state = _odd_tail(h, parts, state, o_g, w_out)
    return y, new_buf, new_state


def _odd_tail(h, parts, state, o_g, w_out):
    b, length, d = h.shape
    q, k, v, z, gb = parts
    z = z.reshape(b, length, z.shape[-1])
    c = DN_CHUNK if length % DN_CHUNK == 0 else length
    c_pad = -(-c // 8) * 8
    if c_pad != c:
        pad = lambda a: jnp.pad(a, ((0, 0), (0, c_pad - c), (0, 0)))
        y, new_state = _gated_delta_out(pad(q), pad(k), pad(v), pad(gb), state, c_pad, pad(h), pad(z), o_g, w_out)
        return y[:, :length], new_state
    return _gated_delta_out(q, k, v, gb, state, c, h, z, o_g, w_out)


def _even_odd_layers(h, start, pool_buf, past, conv_buf, state, ev, od):
    norm_e, w_in_e, w_pool, pool_scale, q_g, k_g, sb_bias, w_out_e = ev
    norm_o, w_in_o, conv_w, a_log, dt_bias, o_g, w_out_o = od
    b, length, _ = h.shape
    if not _whole_tile_sequences(b, length):
        y, new_pool, k, v = _even_layer(h, start, pool_buf, past, *ev)
        y, new_conv, new_state = _odd_layer(y, conv_buf, state, *od)
        return y, new_pool, k, v, new_conv, new_state
    parts, k, v = _even_head(h, past, norm_e, w_in_e, q_g, k_g, sb_bias)
    y, new_pool, odd_parts, new_conv = _even_tail_odd_head(
        h, parts, start, pool_buf, w_pool, pool_scale, w_out_e, conv_buf, norm_o, w_in_o, conv_w, a_log, dt_bias)
    y, new_state = _odd_tail(y, odd_parts, state, o_g, w_out_o)
    return y, new_pool, k, v, new_conv, new_state


def kernel(x_prompt, x_sample, cache_pool, cache_k, cache_v, state_conv, state_delta, page_table, norm_even, w_in_even, w_pool, pool_scale, q_norm, k_norm, sb_bias, w_out_even, norm_odd, w_in_odd, conv_w, a_log, dt_bias, o_norm, w_out_odd):
    bp = x_prompt.shape[0]
    past_len = page_table.shape[1] * cache_k.shape[2]
    n_even, n_odd = norm_even.shape[0], norm_odd.shape[0]
    assert n_odd <= n_even <= n_odd + 1
    hp, hs = x_prompt, x_sample
    outs = {name: [] for name in ("pool_p", "pool_s", "k_p", "v_p", "k_s", "v_s", "conv_p", "conv_s", "st_p", "st_s")}
    for i in range(n_even):
        ev = (norm_even[i], w_in_even[i], w_pool[i], pool_scale[i], q_norm[i], k_norm[i], sb_bias[i], w_out_even[i])
        zero_buf = jnp.zeros((bp, POOL_HIST, cache_pool.shape[-1]), F32)
        past = (cache_k[i], cache_v[i], page_table)
        if i < n_odd:
            od = (norm_odd[i], w_in_odd[i], conv_w[i], a_log[i], dt_bias[i], o_norm[i], w_out_odd[i])
            zero_conv = jnp.zeros((bp, CONV_HIST, state_conv.shape[-1]), F32)
            zero_state = jnp.zeros((bp,) + state_delta.shape[2:], F32)
            hp, pb, kn, vn, cb, st = _even_odd_layers(hp, 0, zero_buf, None, zero_conv, zero_state, ev, od)
            outs["conv_p"].append(cb), outs["st_p"].append(st)
            outs["pool_p"].append(pb), outs["k_p"].append(kn), outs["v_p"].append(vn)
            hs, pb, kn, vn, cb, st = _even_odd_layers(hs, past_len, cache_pool[i], past, state_conv[i],
                                                      state_delta[i], ev, od)
            outs["conv_s"].append(cb), outs["st_s"].append(st)
            outs["pool_s"].append(pb), outs["k_s"].append(kn), outs["v_s"].append(vn)
        else:
            hp, pb, kn, vn = _even_layer(hp, 0, zero_buf, None, *ev)
            outs["pool_p"].append(pb), outs["k_p"].append(kn), outs["v_p"].append(vn)
            hs, pb, kn, vn = _even_layer(hs, past_len, cache_pool[i], past, *ev)
            outs["pool_s"].append(pb), outs["k_s"].append(kn), outs["v_s"].append(vn)
    stack = lambda name: jnp.stack(outs[name])
    return (hp, hs, stack("pool_p"), stack("pool_s"), stack("k_p"), stack("v_p"), stack("k_s"), stack("v_s"),
            stack("conv_p"), stack("conv_s"), stack("st_p"), stack("st_s"))
```

```python
import functools
import math

import jax
import jax.numpy as jnp
from jax import lax
from jax.experimental import pallas as pl
from jax.experimental.pallas import tpu as pltpu

F32 = jnp.float32
BF16 = jnp.bfloat16

NORM_EPS = 1e-6
POOL_WINDOWS = (2, 4, 8, 16)
POOL_HIST = max(POOL_WINDOWS) - 1
POOL_PAD = 16
SB_HEADS = 8
SB_HEAD_DIM = 64
SB_BAND = 32
DN_HEADS = 8
DN_DIM = 128
CONV_WIDTH = 4
CONV_HIST = CONV_WIDTH - 1
CONV_PAD = 8
DN_CHUNK = 64
LANES = 128
MXU_DIM = 256
PAGE = 128
VMEM_LIMIT_BYTES = 48 * 1024 * 1024


def _cparams(*sem):
    return pltpu.CompilerParams(dimension_semantics=sem, vmem_limit_bytes=VMEM_LIMIT_BYTES)


def _dot(a, b):
    return jnp.dot(a, b, preferred_element_type=F32)


def _dot_nt(a, b):
    return lax.dot_general(a, b, (((1,), (1,)), ((), ())), preferred_element_type=F32)


def _dot_tn(a, b):
    return lax.dot_general(a, b, (((0,), (0,)), ((), ())), preferred_element_type=F32)


def _split2(x):
    hi = x.astype(BF16)
    return hi, (x - hi.astype(F32)).astype(BF16)


def _split3(x):
    hi = x.astype(BF16)
    r = x - hi.astype(F32)
    mid = r.astype(BF16)
    return hi, mid, (r - mid.astype(F32)).astype(BF16)


def _dot_exact_rhs(a, b_bf16):
    hi, mid, lo = _split3(a)
    return _dot(hi, b_bf16) + _dot(mid, b_bf16) + _dot(lo, b_bf16)


def _dot_exact_lhs(a_bf16, b):
    hi, mid, lo = _split3(b)
    return _dot(a_bf16, hi) + _dot(a_bf16, mid) + _dot(a_bf16, lo)


def _dot_f32(a, b):
    ah, al = _split2(a)
    bh, bl = _split2(b)
    return _dot(ah, bh) + (_dot(ah, bl) + _dot(al, bh))


def _sigmoid(x):
    return 1.0 / (1.0 + jnp.exp2(x * (-math.log2(math.e))))


def _silu(x):
    return x * _sigmoid(x)


def _softplus_neg_abs(x):
    return jnp.log1p(jnp.exp(-jnp.abs(x)))


def _rms_rows(x, g):
    return x * lax.rsqrt(jnp.mean(x * x, axis=-1, keepdims=True) + NORM_EPS) * g


def _row_tile(m, want):
    t = min(m, want)
    assert m % t == 0, (m, t)
    return t


def _even_proj_kernel(x_ref, g_ref, w_ref, seg_ref, qg_ref, kg_ref,
                      u_ref, q_ref, k_ref, v_ref, kb_ref, vb_ref, gate_ref, *, feature_major):
    xn = _rms_rows(x_ref[...], g_ref[...]).astype(BF16)
    pw = u_ref.shape[-1]

    def proj(c0, c1):
        return _dot(xn, w_ref[:, c0:c1])

    def head_norm(y, gain):
        sq = (y * y).astype(BF16)
        ms = jnp.concatenate([_dot(sq[:, c0:c0 + MXU_DIM], seg_ref[c0:c0 + MXU_DIM, c0:c0 + MXU_DIM])
                              for c0 in range(0, pw, MXU_DIM)], axis=1) * (1.0 / SB_HEAD_DIM)
        return y * lax.rsqrt(ms + NORM_EPS) * gain

    u_ref[...] = proj(0, pw)
    q = head_norm(proj(pw, 2 * pw), qg_ref[...])
    q_ref[...] = (q * (SB_HEAD_DIM ** -0.5)).astype(BF16)
    k = head_norm(proj(2 * pw, 3 * pw), kg_ref[...])
    kb_ref[...] = k.astype(BF16)
    v = proj(3 * pw, 4 * pw)
    vb_ref[...] = v.astype(BF16)
    if feature_major:
        k_ref[0] = k.T
        v_ref[0] = v.T
    else:
        k_ref[...] = k
        v_ref[...] = v
    gate_ref[...] = proj(4 * pw, 6 * pw)


def _even_proj(x2d, norm_g, w_in, q_g, k_g, seq_len):
    m, d = x2d.shape
    pw = SB_HEADS * SB_HEAD_DIM
    assert w_in.shape == (d, 6 * pw)
    tm = _row_tile(m, 512)
    feature_major = seq_len % tm == 0
    lane_head = jnp.arange(pw) // SB_HEAD_DIM
    seg = (lane_head[:, None] == lane_head[None, :]).astype(BF16)
    row = lambda i: (i, 0)
    fixed = lambda i: (0, 0)
    outs = [((m, pw), F32), ((m, pw), BF16), ((m, pw), F32), ((m, pw), F32), ((m, pw), BF16), ((m, pw), BF16),
            ((m, 2 * pw), F32)]
    out_specs = [pl.BlockSpec((tm, s[-1]), row) for s, _ in outs]
    out_shape = [jax.ShapeDtypeStruct(s, t) for s, t in outs]
    if feature_major:
        tiles = seq_len // tm
        for i in (2, 3):
            out_specs[i] = pl.BlockSpec((1, pw, tm), lambda j: (j // tiles, 0, j % tiles))
            out_shape[i] = jax.ShapeDtypeStruct((m // seq_len, pw, seq_len), F32)
    return pl.pallas_call(
        functools.partial(_even_proj_kernel, feature_major=feature_major),
        grid=(m // tm,),
        in_specs=[pl.BlockSpec((tm, d), row), pl.BlockSpec((1, d), fixed), pl.BlockSpec((d, 6 * pw), fixed),
                  pl.BlockSpec((pw, pw), fixed), pl.BlockSpec((1, pw), fixed), pl.BlockSpec((1, pw), fixed)],
        out_specs=out_specs,
        out_shape=out_shape,
        compiler_params=_cparams("parallel"),
        name="even_in_proj",
    )(x2d, norm_g[None, :], w_in.astype(BF16), seg,
      jnp.tile(q_g, SB_HEADS)[None, :], jnp.tile(k_g, SB_HEADS)[None, :])


def _seqs_per_step(b, length):
    bt = 8 if length <= 8 else 1
    assert b % bt == 0
    return bt


def _pool_group(ext, gi, pos, wp_ref, scale_ref, tl):
    w = POOL_WINDOWS[gi]
    cols = slice(gi * LANES, (gi + 1) * LANES)
    if tl % 8 == 0:
        block = ext[0:POOL_PAD + tl, cols]
        win = block
        span = 1
        while span < w:
            win = win + pltpu.roll(win, span, 0)
            span *= 2
        win, cur = win[POOL_PAD:], block[POOL_PAD:]
    else:
        cur = ext[POOL_PAD:POOL_PAD + tl, cols]
        win = cur
        for i in range(1, w):
            win = win + ext[POOL_PAD - i:POOL_PAD - i + tl, cols]
    cnt = jnp.minimum(pos + 1, w).astype(F32)
    d = win / cnt - cur
    return _dot(d.astype(BF16), wp_ref[gi]) * scale_ref[:, cols]


def _pool_kernel(u_ref, buf_ref, wp_ref, scale_ref, y_ref, nb_ref, ext_ref, *, start, tl, bt):
    l = pl.program_id(1)
    lo = POOL_PAD - POOL_HIST
    pos = start + l * tl + lax.broadcasted_iota(jnp.int32, (tl, 1), 0)
    for s in range(bt):
        ext = ext_ref.at[s]

        @pl.when(l == 0)
        def _():
            ext[0:lo, :] = jnp.zeros((lo, ext.shape[-1]), F32)
            ext[lo:POOL_PAD, :] = buf_ref[s]

        @pl.when(l > 0)
        def _():
            ext[lo:POOL_PAD, :] = ext[tl + lo:tl + POOL_PAD, :]

        ext[POOL_PAD:POOL_PAD + tl, :] = u_ref[s]
        for gi in range(len(POOL_WINDOWS)):
            y_ref[s, :, gi * LANES:(gi + 1) * LANES] = _pool_group(ext, gi, pos, wp_ref, scale_ref, tl)

        @pl.when(l == pl.num_programs(1) - 1)
        def _():
            nb_ref[s] = ext[tl + lo:tl + POOL_PAD, :]


def _pool_mix(u, buf, start, w_pool, scale):
    b, length, p = u.shape
    assert p == len(POOL_WINDOWS) * LANES and buf.shape == (b, POOL_HIST, p)
    tl = _row_tile(length, 512)
    assert tl == length or tl >= POOL_PAD
    bt = _seqs_per_step(b, length)
    seq = lambda i, l: (i, l, 0)
    per_b = lambda i, l: (i, 0, 0)
    return pl.pallas_call(
        functools.partial(_pool_kernel, start=start, tl=tl, bt=bt),
        grid=(b // bt, length // tl),
        in_specs=[pl.BlockSpec((bt, tl, p), seq), pl.BlockSpec((bt, POOL_HIST, p), per_b),
                  pl.BlockSpec(w_pool.shape, lambda i, l: (0, 0, 0)), pl.BlockSpec((1, p), lambda i, l: (0, 0))],
        out_specs=[pl.BlockSpec((bt, tl, p), seq), pl.BlockSpec((bt, POOL_HIST, p), per_b)],
        out_shape=[jax.ShapeDtypeStruct((b, length, p), F32), jax.ShapeDtypeStruct((b, POOL_HIST, p), F32)],
        scratch_shapes=[pltpu.VMEM((bt, POOL_PAD + tl, p), F32)],
        compiler_params=_cparams("parallel", "arbitrary"),
        name="pool_mix",
    )(u, buf, w_pool.astype(BF16), scale[None, :])


def _log_beta_keep(z):
    m = jnp.minimum(z, 0.0)
    d = m - z
    s = jnp.log(1.0 + jnp.exp(m + d))
    return m - s, d - s


def _stick_break_tiles(zs, carries, tri, masks, chained=False):
    n = range(len(zs))
    log_beta, log_keep = zip(*[_log_beta_keep(z) for z in zs])
    log_keep = [lk if m is None else jnp.where(m, lk, 0.0) for lk, m in zip(log_keep, masks)]
    totals = [jnp.sum(lk, axis=-1, keepdims=True) for lk in log_keep]
    if chained:
        carry_in = [carries[0]]
        for i in n:
            carry_in.append(carry_in[i] + totals[i])
        new_carries = carry_in[-1:]
    else:
        carry_in = carries
        new_carries = [carries[i] + totals[i] for i in n]
    suffix = [_dot(lk.astype(BF16), tri) + carry_in[i] for i, lk in enumerate(log_keep)]
    a = [jnp.exp(log_beta[i] + suffix[i]) for i in n]
    a = [x if m is None else jnp.where(m, x, 0.0) for x, m in zip(a, masks)]
    return [x.astype(BF16) for x in a], new_carries


def _later_key_matrix(n):
    return (lax.broadcasted_iota(jnp.int32, (n, n), 0) > lax.broadcasted_iota(jnp.int32, (n, n), 1)).astype(BF16)


def _sb_prompt_kernel(bias_ref, q_ref, k_ref, v_ref, o_ref, acc_ref, carry_ref, tot_ref, lb_ref, lk_ref, a_ref, *, t):
    qi = pl.program_id(1)
    heads = range(SB_HEADS)
    low_lanes = lax.broadcasted_iota(jnp.int32, (t, LANES), 1) < SB_HEAD_DIM
    tri = _later_key_matrix(t)
    causal = lax.broadcasted_iota(jnp.int32, (t, t), 1) < lax.broadcasted_iota(jnp.int32, (t, t), 0)
    pair_cols = lambda h: slice((h // 2) * LANES, (h // 2 + 1) * LANES)
    qh = []
    for h in heads:
        q_pair = q_ref[0, :, pair_cols(h)]
        qh.append(jnp.where(low_lanes if h % 2 == 0 else jnp.logical_not(low_lanes), q_pair, jnp.zeros_like(q_pair)))

    bands = [slice(r, r + SB_BAND) for r in range(0, t, SB_BAND)]

    def tiles(kb, mask):
        rows = pl.ds(pl.multiple_of(kb * t, t), t)
        for h in heads:
            z = _dot_nt(qh[h], k_ref[0, rows, pair_cols(h)]) + bias_ref[h]
            for band in bands:
                log_beta, log_keep = _log_beta_keep(z[band])
                if mask is not None:
                    log_keep = jnp.where(mask[band], log_keep, 0.0)
                lb_ref[h, band, :] = log_beta
                lk_ref[h, band, :] = log_keep.astype(BF16)
                tot_ref[h, band, :] = jnp.sum(log_keep, axis=-1, keepdims=True)
        for h in heads:
            suffix = _dot(lk_ref[h], tri)
            for band in bands:
                carry = carry_ref[h, band, :]
                a = jnp.exp(lb_ref[h, band, :] + (suffix[band] + carry))
                if mask is not None:
                    a = jnp.where(mask[band], a, 0.0)
                a_ref[h, band, :] = a.astype(BF16)
                carry_ref[h, band, :] = carry + tot_ref[h, band, :]
        for h in heads:
            acc_ref[h] += _dot(a_ref[h], v_ref[0, rows, pair_cols(h)])

    acc_ref[...] = jnp.zeros(acc_ref.shape, F32)
    carry_ref[...] = jnp.zeros(carry_ref.shape, F32)
    tiles(qi, causal)

    def earlier(j, _):
        tiles(qi - 1 - j, None)
        return 0

    lax.fori_loop(0, qi, earlier, 0)
    for h in range(0, SB_HEADS, 2):
        o_ref[0, :, pair_cols(h)] = jnp.where(low_lanes, acc_ref[h], acc_ref[h + 1])


def _sb_prompt(q, k, v, bias):
    b, length, width = q.shape
    assert width == SB_HEADS * SB_HEAD_DIM and 2 * SB_HEAD_DIM == LANES
    t = _row_tile(length, 256)
    q_blk = lambda i, j: (i, j, 0)
    kv_blk = lambda i, j: (i, 0, 0)
    return pl.pallas_call(
        functools.partial(_sb_prompt_kernel, t=t),
        grid=(b, length // t),
        in_specs=[pl.BlockSpec(memory_space=pltpu.SMEM), pl.BlockSpec((1, t, width), q_blk),
                  pl.BlockSpec((1, length, width), kv_blk), pl.BlockSpec((1, length, width), kv_blk)],
        out_specs=pl.BlockSpec((1, t, width), q_blk),
        scratch_shapes=[pltpu.VMEM((SB_HEADS, t, LANES), F32), pltpu.VMEM((SB_HEADS, t, 1), F32),
                        pltpu.VMEM((SB_HEADS, t, 1), F32), pltpu.VMEM((SB_HEADS, t, t), F32),
                        pltpu.VMEM((SB_HEADS, t, t), BF16), pltpu.VMEM((SB_HEADS, t, t), BF16)],
        out_shape=jax.ShapeDtypeStruct((b, length, width), F32),
        compiler_params=_cparams("parallel", "arbitrary"),
        name="stick_break_prompt",
    )(bias, q, k, v)


def _sb_sample_kernel(pt_ref, q_ref, bias_ref, kn_ref, vn_ref, *refs, n_pages, n_new):
    k_refs, v_refs, o_ref = refs[:n_pages], refs[n_pages:2 * n_pages], refs[2 * n_pages]
    kpad_ref, vpad_ref = refs[2 * n_pages + 1:]
    rows = SB_HEADS * n_new
    width = SB_HEADS * SB_HEAD_DIM
    row_i = lax.broadcasted_iota(jnp.int32, (rows, width), 0)
    lane_i = lax.broadcasted_iota(jnp.int32, (rows, width), 1)
    own_head = (row_i // n_new) == (lane_i // SB_HEAD_DIM)
    qbd = jnp.where(own_head, q_ref[0], jnp.zeros_like(q_ref[0]))
    bias = bias_ref[...]
    tri = _later_key_matrix(PAGE)

    kpad_ref[...] = jnp.zeros(kpad_ref.shape, BF16)
    vpad_ref[...] = jnp.zeros(vpad_ref.shape, BF16)
    kpad_ref[0:kn_ref.shape[1], :] = kn_ref[0].astype(BF16)
    vpad_ref[0:vn_ref.shape[1], :] = vn_ref[0].astype(BF16)
    qpos = lax.broadcasted_iota(jnp.int32, (rows, PAGE), 0) % n_new
    kpos = lax.broadcasted_iota(jnp.int32, (rows, PAGE), 1)

    pages = list(reversed(range(n_pages)))
    zs = [_dot_nt(qbd, kpad_ref[...]) + bias] + [_dot(qbd, k_refs[p][0].astype(BF16)) + bias for p in pages]
    a, _ = _stick_break_tiles(zs, [jnp.zeros((rows, 1), F32)], tri, [kpos < qpos] + [None] * n_pages, chained=True)
    out = _dot(a[0], vpad_ref[...])
    for i, p in enumerate(pages):
        out = out + _dot_nt(a[i + 1], v_refs[p][0].astype(BF16))
    out = jnp.where(own_head, out, 0.0)
    acc = out[0:n_new]
    for h in range(1, SB_HEADS):
        acc = acc + out[h * n_new:(h + 1) * n_new]
    o_ref[0] = acc


def _sb_sample(q, k_new, v_new, bias, cache_k, cache_v, page_table):
    b, n_new, width = q.shape
    n_pages = page_table.shape[1]
    n_phys = cache_k.shape[0]
    assert cache_k.shape[1:] == (PAGE, SB_HEADS, SB_HEAD_DIM) and n_new <= 8
    ck = jnp.transpose(cache_k, (0, 2, 3, 1)).reshape(n_phys, width, PAGE)
    cv = jnp.transpose(cache_v, (0, 2, 3, 1)).reshape(n_phys, width, PAGE)
    rows = SB_HEADS * n_new
    pad = ((0, 0), (0, 8 - n_new), (0, 0))
    per_b = lambda i, pt: (i, 0, 0)
    page_specs = [pl.BlockSpec((1, width, PAGE), functools.partial(lambda i, pt, p: (pt[i, p], 0, 0), p=p))
                  for p in range(n_pages)]
    return pl.pallas_call(
        functools.partial(_sb_sample_kernel, n_pages=n_pages, n_new=n_new),
        grid_spec=pltpu.PrefetchScalarGridSpec(
            num_scalar_prefetch=1,
            grid=(b,),
            in_specs=[pl.BlockSpec((1, rows, width), per_b), pl.BlockSpec((rows, 1), lambda i, pt: (0, 0)),
                      pl.BlockSpec((1, 8, width), per_b), pl.BlockSpec((1, 8, width), per_b)] + page_specs + page_specs,
            out_specs=pl.BlockSpec((1, n_new, width), per_b),
            scratch_shapes=[pltpu.VMEM((PAGE, width), BF16), pltpu.VMEM((PAGE, width), BF16)]),
        out_shape=jax.ShapeDtypeStruct((b, n_new, width), F32),
        compiler_params=_cparams("parallel"),
        name="stick_break_sample",
    )(page_table, jnp.tile(q, (1, SB_HEADS, 1)), jnp.repeat(bias, n_new)[:, None],
      jnp.pad(k_new, pad), jnp.pad(v_new, pad), *([ck] * n_pages), *([cv] * n_pages))


def _even_out_kernel(h_ref, py_ref, sb_ref, gate_ref, w_ref, y_ref):
    pw = py_ref.shape[-1]
    m_pool = (py_ref[...] * _silu(gate_ref[:, 0:pw])).astype(BF16)
    m_sb = (sb_ref[...] * _silu(gate_ref[:, pw:2 * pw])).astype(BF16)
    y_ref[...] = h_ref[...] + (_dot(m_pool, w_ref[0:pw, :]) + _dot(m_sb, w_ref[pw:2 * pw, :]))


def _even_out(h2d, pool_y, sb, gate, w_out):
    m, d = h2d.shape
    pw = pool_y.shape[-1]
    tm = _row_tile(m, 512)
    row = lambda i: (i, 0)
    return pl.pallas_call(
        _even_out_kernel,
        grid=(m // tm,),
        in_specs=[pl.BlockSpec((tm, d), row), pl.BlockSpec((tm, pw), row), pl.BlockSpec((tm, pw), row),
                  pl.BlockSpec((tm, 2 * pw), row), pl.BlockSpec((2 * pw, d), lambda i: (0, 0))],
        out_specs=pl.BlockSpec((tm, d), row),
        out_shape=jax.ShapeDtypeStruct((m, d), F32),
        compiler_params=_cparams("parallel"),
        name="even_out_proj",
    )(h2d, pool_y, sb, gate, w_out.astype(BF16))


def _even_head(h, past, norm_g, w_in, q_g, k_g, sb_bias):
    b, length, d = h.shape
    u, q, k, v, kb, vb, gate = _even_proj(h.reshape(b * length, d), norm_g, w_in, q_g, k_g, length)
    seq = lambda a: a.reshape(b, length, a.shape[-1])
    if past is None:
        sb = _sb_prompt(seq(q), seq(kb), seq(vb), sb_bias)
    else:
        sb = _sb_sample(seq(q), seq(k), seq(v), sb_bias, *past)
    if k.ndim == 3:
        heads = lambda a: jnp.transpose(a.reshape(b, SB_HEADS, SB_HEAD_DIM, length), (0, 3, 1, 2))
    else:
        heads = lambda a: a.reshape(b, length, SB_HEADS, SB_HEAD_DIM)
    return (seq(u), sb, gate), heads(k), heads(v)


def _even_tail(h, parts, start, pool_buf, w_pool, pool_scale, w_out):
    b, length, d = h.shape
    u, sb, gate = parts
    pool_y, new_buf = _pool_mix(u, pool_buf, start, w_pool, pool_scale)
    y = _even_out(h.reshape(b * length, d), pool_y.reshape(b * length, -1), sb.reshape(b * length, -1), gate, w_out)
    return y.reshape(b, length, d), new_buf


def _even_layer(h, start, pool_buf, past, norm_g, w_in, w_pool, pool_scale, q_g, k_g, sb_bias, w_out):
    parts, k, v = _even_head(h, past, norm_g, w_in, q_g, k_g, sb_bias)
    y, new_buf = _even_tail(h, parts, start, pool_buf, w_pool, pool_scale, w_out)
    return y, new_buf, k, v


def _odd_proj_kernel(x_ref, g_ref, w_ref, wba_ref, qkv_ref, z_ref, ba_ref, *, chunk):
    xn = _rms_rows(x_ref[...], g_ref[...]).astype(BF16)
    n_qkv = qkv_ref.shape[-1]
    for c0 in range(0, n_qkv, chunk):
        qkv_ref[:, c0:c0 + chunk] = _dot(xn, w_ref[:, c0:c0 + chunk])
    for c0 in range(0, z_ref.shape[-1], chunk):
        z_ref[:, c0:c0 + chunk] = _dot(xn, w_ref[:, n_qkv + c0:n_qkv + c0 + chunk])
    ba_ref[...] = _dot(xn, wba_ref[...])


def _odd_weights(w_in):
    n_main = 4 * DN_HEADS * DN_DIM
    assert w_in.shape[1] == n_main + 2 * DN_HEADS
    w_main = w_in[:, :n_main].astype(BF16)
    w_ba = jnp.pad(w_in[:, n_main:], ((0, 0), (0, LANES - 2 * DN_HEADS))).astype(BF16)
    return w_main, w_ba


def _odd_proj(x2d, norm_g, w_in):
    m, d = x2d.shape
    n_qkv, n_z = 3 * DN_HEADS * DN_DIM, DN_HEADS * DN_DIM
    tm = _row_tile(m, 256)
    w_main, w_ba = _odd_weights(w_in)
    row = lambda i: (i, 0)
    fixed = lambda i: (0, 0)
    return pl.pallas_call(
        functools.partial(_odd_proj_kernel, chunk=512),
        grid=(m // tm,),
        in_specs=[pl.BlockSpec((tm, d), row), pl.BlockSpec((1, d), fixed), pl.BlockSpec(w_main.shape, fixed),
                  pl.BlockSpec(w_ba.shape, fixed)],
        out_specs=[pl.BlockSpec((tm, n_qkv), row), pl.BlockSpec((tm, n_z), row), pl.BlockSpec((tm, LANES), row)],
        out_shape=[jax.ShapeDtypeStruct((m, n_qkv), F32), jax.ShapeDtypeStruct((m, n_z), F32),
                   jax.ShapeDtypeStruct((m, LANES), F32)],
        compiler_params=_cparams("parallel"),
        name="odd_in_proj",
    )(x2d, norm_g[None, :], w_main, w_ba)


def _conv_head_tile(ext, cw_ref, c, tl):
    cols = slice(c * LANES, (c + 1) * LANES)
    if tl % 8 == 0:
        block = ext[0:CONV_PAD + tl, cols]
        conv = block[CONV_PAD:] * cw_ref[CONV_HIST:CONV_WIDTH, cols]
        for j in range(CONV_HIST):
            conv = conv + pltpu.roll(block, CONV_HIST - j, 0)[CONV_PAD:] * cw_ref[j:j + 1, cols]
    else:
        lo = CONV_PAD - CONV_HIST
        conv = ext[lo:lo + tl, cols] * cw_ref[0:1, cols]
        for j in range(1, CONV_WIDTH):
            conv = conv + ext[lo + j:lo + j + tl, cols] * cw_ref[j:j + 1, cols]
    y = _silu(conv)
    if c < 2 * DN_HEADS:
        inv_norm = lax.rsqrt(jnp.sum(y * y, axis=-1, keepdims=True) + NORM_EPS)
        if c < DN_HEADS:
            inv_norm = inv_norm * (DN_DIM ** -0.5)
        y = y * inv_norm
    return y


def _decay_gates(ba, ab_ref):
    x = ba + ab_ref[1:2, :]
    g = -jnp.exp(ab_ref[0:1, :]) * (jnp.maximum(x, 0.0) + _softplus_neg_abs(x))
    is_beta = lax.broadcasted_iota(jnp.int32, ba.shape, 1) < DN_HEADS
    return jnp.where(is_beta, _sigmoid(ba), g)


def _conv_kernel(x_ref, buf_ref, cw_ref, ba_ref, ab_ref, q_ref, k_ref, v_ref, gb_ref, nb_ref, ext_ref, *, tl, bt):
    l = pl.program_id(1)
    lo = CONV_PAD - CONV_HIST
    width = ext_ref.shape[-1]
    outs = (q_ref, k_ref, v_ref)
    for s in range(bt):
        ext = ext_ref.at[s]

        @pl.when(l == 0)
        def _():
            ext[0:lo, :] = jnp.zeros((lo, width), F32)
            ext[lo:CONV_PAD, :] = buf_ref[s]

        @pl.when(l > 0)
        def _():
            ext[lo:CONV_PAD, :] = ext[tl + lo:tl + CONV_PAD, :]

        ext[CONV_PAD:CONV_PAD + tl, :] = x_ref[s]
        for c in range(width // LANES):
            outs[c // DN_HEADS][s, :, _head_cols(c % DN_HEADS)] = _conv_head_tile(ext, cw_ref, c, tl)
        gb_ref[s] = _decay_gates(ba_ref[s], ab_ref)

        @pl.when(l == pl.num_programs(1) - 1)
        def _():
            nb_ref[s] = ext[tl + lo:tl + CONV_PAD, :]


def _odd_proj_conv_kernel(x_ref, g_ref, w_ref, wba_ref, buf_ref, cw_ref, ab_ref,
                          q_ref, k_ref, v_ref, z_ref, gb_ref, nb_ref, ext_ref, *, chunk, tiles_per_seq):
    i = pl.program_id(0)
    tm = x_ref.shape[0]
    lo = CONV_PAD - CONV_HIST
    n_qkv = ext_ref.shape[-1]
    seq_start = i % tiles_per_seq == 0

    @pl.when(seq_start)
    def _():
        ext_ref[0:lo, :] = jnp.zeros((lo, n_qkv), F32)
        ext_ref[lo:CONV_PAD, :] = buf_ref[0]

    @pl.when(jnp.logical_not(seq_start))
    def _():
        ext_ref[lo:CONV_PAD, :] = ext_ref[tm + lo:tm + CONV_PAD, :]

    xn = _rms_rows(x_ref[...], g_ref[...]).astype(BF16)
    _project_conv(xn, w_ref, wba_ref, cw_ref, ab_ref, ext_ref, q_ref, k_ref, v_ref, z_ref, gb_ref, chunk)

    @pl.when(i % tiles_per_seq == tiles_per_seq - 1)
    def _():
        nb_ref[0] = ext_ref[tm + lo:tm + CONV_PAD, :]


def _project_conv(xn, w_ref, wba_ref, cw_ref, ab_ref, ext_ref, q_ref, k_ref, v_ref, z_ref, gb_ref, chunk):
    tm = xn.shape[0]
    n_qkv = ext_ref.shape[-1]
    outs = (q_ref, k_ref, v_ref)
    for c0 in range(0, n_qkv, chunk):
        ext_ref[CONV_PAD:CONV_PAD + tm, c0:c0 + chunk] = _dot(xn, w_ref[:, c0:c0 + chunk])
        for c in range(c0 // LANES, (c0 + chunk) // LANES):
            outs[c // DN_HEADS][:, _head_cols(c % DN_HEADS)] = _conv_head_tile(ext_ref, cw_ref, c, tm)
    for c0 in range(0, z_ref.shape[-1], chunk):
        z_ref[:, c0:c0 + chunk] = _dot(xn, w_ref[:, n_qkv + c0:n_qkv + c0 + chunk])
    gb_ref[...] = _decay_gates(_dot(xn, wba_ref[...]), ab_ref)


def _even_tail_odd_head_kernel(h_ref, u_ref, sb_ref, gate_ref, pbuf_ref, wp_ref, pscale_ref, wout_ref,
                               g_ref, w_ref, wba_ref, cbuf_ref, cw_ref, ab_ref,
                               y_ref, npb_ref, q_ref, k_ref, v_ref, z_ref, gb_ref, ncb_ref,
                               pext_ref, mix_ref, cext_ref, *, start, chunk, tiles_per_seq):
    i = pl.program_id(0)
    tm = h_ref.shape[0]
    l = i % tiles_per_seq
    plo = POOL_PAD - POOL_HIST
    clo = CONV_PAD - CONV_HIST
    pw = u_ref.shape[-1]

    @pl.when(l == 0)
    def _():
        pext_ref[0:plo, :] = jnp.zeros((plo, pw), F32)
        pext_ref[plo:POOL_PAD, :] = pbuf_ref[0]
        cext_ref[0:clo, :] = jnp.zeros((clo, cext_ref.shape[-1]), F32)
        cext_ref[clo:CONV_PAD, :] = cbuf_ref[0]

    @pl.when(l > 0)
    def _():
        pext_ref[plo:POOL_PAD, :] = pext_ref[tm + plo:tm + POOL_PAD, :]
        cext_ref[clo:CONV_PAD, :] = cext_ref[tm + clo:tm + CONV_PAD, :]

    pext_ref[POOL_PAD:POOL_PAD + tm, :] = u_ref[...]
    pos = start + l * tm + lax.broadcasted_iota(jnp.int32, (tm, 1), 0)
    for gi in range(len(POOL_WINDOWS)):
        cols = slice(gi * LANES, (gi + 1) * LANES)
        pooled = _pool_group(pext_ref, gi, pos, wp_ref, pscale_ref, tm)
        mix_ref[:, cols] = (pooled * _silu(gate_ref[:, cols])).astype(BF16)
    for c0 in range(0, pw, LANES):
        cols = slice(pw + c0, pw + c0 + LANES)
        mix_ref[:, cols] = (sb_ref[:, c0:c0 + LANES] * _silu(gate_ref[:, cols])).astype(BF16)
    y = h_ref[...] + _dot(mix_ref[...], wout_ref[...])
    y_ref[...] = y

    xn = _rms_rows(y, g_ref[...]).astype(BF16)
    _project_conv(xn, w_ref, wba_ref, cw_ref, ab_ref, cext_ref, q_ref, k_ref, v_ref, z_ref, gb_ref, chunk)

    @pl.when(l == tiles_per_seq - 1)
    def _():
        npb_ref[0] = pext_ref[tm + plo:tm + POOL_PAD, :]
        ncb_ref[0] = cext_ref[tm + clo:tm + CONV_PAD, :]


def _decay_params(a_log, dt_bias):
    park = lambda vec: jnp.pad(vec, (DN_HEADS, LANES - 2 * DN_HEADS))
    return jnp.stack([park(a_log), park(dt_bias)])


def _odd_proj_conv(x2d, seq_len, conv_buf, norm_g, w_in, conv_w, a_log, dt_bias):
    m, d = x2d.shape
    n = DN_HEADS * DN_DIM
    n_qkv = 3 * n
    tm = _row_tile(m, 256)
    assert seq_len % tm == 0 and tm >= CONV_PAD and conv_buf.shape == (m // seq_len, CONV_HIST, n_qkv)
    tiles_per_seq = seq_len // tm
    w_main, w_ba = _odd_weights(w_in)
    row = lambda i: (i, 0)
    fixed = lambda i: (0, 0)
    per_seq = lambda i: (i // tiles_per_seq, 0, 0)
    return pl.pallas_call(
        functools.partial(_odd_proj_conv_kernel, chunk=512, tiles_per_seq=tiles_per_seq),
        grid=(m // tm,),
        in_specs=[pl.BlockSpec((tm, d), row), pl.BlockSpec((1, d), fixed), pl.BlockSpec(w_main.shape, fixed),
                  pl.BlockSpec(w_ba.shape, fixed), pl.BlockSpec((1, CONV_HIST, n_qkv), per_seq),
                  pl.BlockSpec((CONV_WIDTH, n_qkv), fixed), pl.BlockSpec((2, LANES), fixed)],
        out_specs=[pl.BlockSpec((tm, n), row)] * 4 + [pl.BlockSpec((tm, LANES), row),
                                                      pl.BlockSpec((1, CONV_HIST, n_qkv), per_seq)],
        out_shape=[jax.ShapeDtypeStruct((m, n), F32)] * 4
        + [jax.ShapeDtypeStruct((m, LANES), F32), jax.ShapeDtypeStruct(conv_buf.shape, F32)],
        scratch_shapes=[pltpu.VMEM((CONV_PAD + tm, n_qkv), F32)],
        compiler_params=_cparams("arbitrary"),
        name="odd_in_proj_conv",
    )(x2d, norm_g[None, :], w_main, w_ba, conv_buf, conv_w, _decay_params(a_log, dt_bias))


def _conv_gates(qkv, conv_buf, conv_w, ba, a_log, dt_bias):
    b, length, width = qkv.shape
    n = DN_HEADS * DN_DIM
    assert width == 3 * n and conv_buf.shape == (b, CONV_HIST, width)
    tl = _row_tile(length, 256)
    assert tl == length or tl >= CONV_PAD
    ab = _decay_params(a_log, dt_bias)
    bt = _seqs_per_step(b, length)
    seq = lambda i, l: (i, l, 0)
    per_b = lambda i, l: (i, 0, 0)
    fixed = lambda i, l: (0, 0)
    return pl.pallas_call(
        functools.partial(_conv_kernel, tl=tl, bt=bt),
        grid=(b // bt, length // tl),
        in_specs=[pl.BlockSpec((bt, tl, width), seq), pl.BlockSpec((bt, CONV_HIST, width), per_b),
                  pl.BlockSpec((CONV_WIDTH, width), fixed), pl.BlockSpec((bt, tl, LANES), seq),
                  pl.BlockSpec((2, LANES), fixed)],
        out_specs=[pl.BlockSpec((bt, tl, n), seq)] * 3 + [pl.BlockSpec((bt, tl, LANES), seq),
                                                           pl.BlockSpec((bt, CONV_HIST, width), per_b)],
        out_shape=[jax.ShapeDtypeStruct((b, length, n), F32)] * 3
        + [jax.ShapeDtypeStruct((b, length, LANES), F32), jax.ShapeDtypeStruct((b, CONV_HIST, width), F32)],
        scratch_shapes=[pltpu.VMEM((bt, CONV_PAD + tl, width), F32)],
        compiler_params=_cparams("parallel", "arbitrary"),
        name="conv_norm_gates",
    )(qkv, conv_buf, conv_w, ba, ab)


def _unit_lower_inverses(lmats, c):
    eye = (lax.broadcasted_iota(jnp.int32, (c, c), 0) == lax.broadcasted_iota(jnp.int32, (c, c), 1)).astype(F32)
    powers = [-lm for lm in lmats]
    invs = [eye + p for p in powers]
    span = 2
    while span < c:
        powers = [_dot_f32(p, p) for p in powers]
        invs = [i + _dot_f32(i, p) for i, p in zip(invs, powers)]
        span *= 2
    return invs


def _head_cols(h):
    return slice(h * DN_DIM, (h + 1) * DN_DIM)


def _delta_local_kernel(q_ref, k_ref, v_ref, gb_ref, u_ref, wq_ref, kd_ref, att_ref, eg_ref, *, c, n_chunks, group):
    row = lax.broadcasted_iota(jnp.int32, (c, c), 0)
    col = lax.broadcasted_iota(jnp.int32, (c, c), 1)
    incl = row >= col
    strict = row > col
    prefix = incl.astype(BF16)
    eye_lanes = (lax.broadcasted_iota(jnp.int32, (LANES, LANES), 0)
                 == lax.broadcasted_iota(jnp.int32, (LANES, LANES), 1)).astype(BF16)
    units = [(g, h) for g in range(group) for h in range(DN_HEADS)]
    n = range(len(units))

    def chunks(step, _):
        ci = [step * group + g for g in range(group)]
        rows = [pl.ds(pl.multiple_of(x * c, c), c) for x in ci]
        gb = [gb_ref[r, :] for r in rows]
        gcs = [_dot_exact_lhs(prefix, x) for x in gb]
        gcs_t = []
        for x in gcs:
            h3, m3, l3 = _split3(x)
            gcs_t.append(_dot_nt(eye_lanes, h3) + _dot_nt(eye_lanes, m3) + _dot_nt(eye_lanes, l3))
        q = [q_ref[rows[g], _head_cols(h)] for g, h in units]
        k = [k_ref[rows[g], _head_cols(h)] for g, h in units]
        v = [v_ref[rows[g], _head_cols(h)] for g, h in units]
        beta = [gb[g][:, h:h + 1] for g, h in units]
        gc = [gcs[g][:, DN_HEADS + h:DN_HEADS + h + 1] for g, h in units]
        g_last = [gcs[g][c - 1:c, DN_HEADS + h:DN_HEADS + h + 1] for g, h in units]
        gc_row = [gcs_t[g][DN_HEADS + h:DN_HEADS + h + 1, :] for g, h in units]
        decay = [jnp.where(incl, jnp.exp(jnp.where(incl, gc[i] - gc_row[i], 0.0)), 0.0) for i in n]
        kb = [k[i] * beta[i] for i in n]
        k16 = [k[i].astype(BF16) for i in n]
        kbq_kt = [_dot_nt(jnp.concatenate([kb[i], q[i]], axis=0).astype(BF16), k16[i]) for i in n]
        lmat = [jnp.where(strict, kbq_kt[i][0:c] * decay[i], 0.0) for i in n]
        tmat = [t.astype(BF16) for t in _unit_lower_inverses(lmat, c)]
        egc = [jnp.exp(gc[i]) for i in n]
        for i, (g, h) in enumerate(units):
            cols = _head_cols(h)
            uw = _dot(tmat[i], jnp.concatenate([v[i] * beta[i], kb[i] * egc[i]], axis=1).astype(BF16))
            u_ref[rows[g], cols] = uw[:, 0:DN_DIM]
            wq_ref[ci[g], :, cols] = jnp.concatenate([uw[:, DN_DIM:], q[i] * egc[i]], axis=0).astype(BF16)
            kd_ref[rows[g], cols] = (k[i] * jnp.exp(g_last[i] - gc[i])).astype(BF16)
            att = kbq_kt[i][c:2 * c] * decay[i]
            if c < LANES:
                att = jnp.concatenate([att, jnp.zeros((c, LANES - c), F32)], axis=1)
            att_ref[rows[g], cols] = att.astype(BF16)
            eg_ref[ci[g], h:h + 1, :] = jnp.broadcast_to(jnp.exp(g_last[i]), (1, LANES))
        return 0

    assert n_chunks % group == 0
    lax.fori_loop(0, n_chunks // group, chunks, 0)


def _delta_scan_kernel(u_ref, wq_ref, kd_ref, att_ref, eg_ref, s0_ref, h_ref, z_ref, og_ref, wout_ref,
                       y_ref, s_out_ref, s_ref, o_ref, gated_ref, *, c, n_chunks, bt):
    l = pl.program_id(1)

    @pl.when(l == 0)
    def _():
        s_ref[...] = s0_ref[...]

    units = [(i, h) for i in range(bt) for h in range(DN_HEADS)]

    def chunk(ci, _):
        rows = pl.ds(pl.multiple_of(ci * c, c), c)
        s = [s_ref[i, h] for i, h in units]
        s16 = [x.astype(BF16) for x in s]
        r = [_dot(wq_ref[i, ci, :, _head_cols(h)], s16[n]) for n, (i, h) in enumerate(units)]
        v16 = [(u_ref[i, rows, _head_cols(h)] - r[n][0:c]).astype(BF16) for n, (i, h) in enumerate(units)]
        for n, (i, h) in enumerate(units):
            cols = _head_cols(h)
            att = att_ref[i, rows, h * DN_DIM:h * DN_DIM + c]
            o_ref[i, rows, cols] = r[n][c:2 * c] + _dot(att, v16[n])
            s_ref[i, h] = s[n] * eg_ref[i, ci, h:h + 1, :] + _dot_tn(kd_ref[i, rows, cols], v16[n])
        return 0

    lax.fori_loop(0, n_chunks, chunk, 0)
    for i in range(bt):
        for hd in range(DN_HEADS):
            cols = _head_cols(hd)
            gated_ref[:, cols] = (_rms_rows(o_ref[i, :, cols], og_ref[...]) * _silu(z_ref[i, :, cols])).astype(BF16)
        y_ref[i] = h_ref[i] + _dot(gated_ref[...], wout_ref[...])

    @pl.when(l == pl.num_programs(1) - 1)
    def _():
        s_out_ref[...] = s_ref[...]


def _gated_delta_out(q, k, v, gb, state, c, h, z, o_g, w_out):
    b, length, n = q.shape
    assert n == DN_HEADS * DN_DIM and state.shape == (b, DN_HEADS, DN_DIM, DN_DIM) and length % c == 0
    assert c % 8 == 0 and c <= LANES
    tokens = b * length
    flat = lambda a: a.reshape(tokens, a.shape[-1])
    tn = _row_tile(tokens, max(c, 256))
    row = lambda i: (i, 0)
    per_chunk = lambda i: (i, 0, 0)
    u, wq, kd, att, eg = pl.pallas_call(
        functools.partial(_delta_local_kernel, c=c, n_chunks=tn // c, group=4),
        grid=(tokens // tn,),
        in_specs=[pl.BlockSpec((tn, n), row)] * 3 + [pl.BlockSpec((tn, LANES), row)],
        out_specs=[pl.BlockSpec((tn, n), row), pl.BlockSpec((tn // c, 2 * c, n), per_chunk),
                   pl.BlockSpec((tn, n), row), pl.BlockSpec((tn, n), row),
                   pl.BlockSpec((tn // c, DN_HEADS, LANES), per_chunk)],
        out_shape=[jax.ShapeDtypeStruct((tokens, n), F32), jax.ShapeDtypeStruct((tokens // c, 2 * c, n), BF16),
                   jax.ShapeDtypeStruct((tokens, n), BF16), jax.ShapeDtypeStruct((tokens, n), BF16),
                   jax.ShapeDtypeStruct((tokens // c, DN_HEADS, LANES), F32)],
        compiler_params=_cparams("parallel"),
        name="delta_local",
    )(flat(q), flat(k), flat(v), flat(gb))

    n_seq_chunks = length // c
    tl = c * min(n_seq_chunks, 4)
    bt = 2 if n_seq_chunks > 1 else 4
    assert length % tl == 0 and b % bt == 0
    seq = lambda i, l: (i, l, 0)
    seq_chunks = lambda i, l: (i, l, 0, 0)
    per_b = lambda i, l: (i, 0, 0, 0)
    st_block = (bt,) + state.shape[1:]
    d = h.shape[-1]
    fixed = lambda i, l: (0, 0)
    return pl.pallas_call(
        functools.partial(_delta_scan_kernel, c=c, n_chunks=tl // c, bt=bt),
        grid=(b // bt, length // tl),
        in_specs=[pl.BlockSpec((bt, tl, n), seq), pl.BlockSpec((bt, tl // c, 2 * c, n), seq_chunks),
                  pl.BlockSpec((bt, tl, n), seq), pl.BlockSpec((bt, tl, n), seq),
                  pl.BlockSpec((bt, tl // c, DN_HEADS, LANES), seq_chunks), pl.BlockSpec(st_block, per_b),
                  pl.BlockSpec((bt, tl, d), seq), pl.BlockSpec((bt, tl, n), seq),
                  pl.BlockSpec((1, DN_DIM), fixed), pl.BlockSpec((n, d), fixed)],
        out_specs=[pl.BlockSpec((bt, tl, d), seq), pl.BlockSpec(st_block, per_b)],
        out_shape=[jax.ShapeDtypeStruct((b, length, d), F32), jax.ShapeDtypeStruct(state.shape, F32)],
        scratch_shapes=[pltpu.VMEM(st_block, F32), pltpu.VMEM((bt, tl, n), F32), pltpu.VMEM((tl, n), BF16)],
        compiler_params=_cparams("parallel", "arbitrary"),
        name="delta_scan_out",
    )(u.reshape(b, length, n), wq.reshape(b, n_seq_chunks, 2 * c, n), kd.reshape(b, length, n),
      att.reshape(b, length, n), eg.reshape(b, n_seq_chunks, DN_HEADS, LANES), state,
      h, z, o_g[None, :], w_out.astype(BF16))


def _whole_tile_sequences(b, length):
    return length % min(b * length, 256) == 0


def _odd_head(h, conv_buf, norm_g, w_in, conv_w, a_log, dt_bias):
    b, length, d = h.shape
    h2d = h.reshape(b * length, d)
    seq = lambda a: a.reshape(b, length, a.shape[-1])
    if _whole_tile_sequences(b, length):
        q, k, v, z, gb, new_buf = _odd_proj_conv(h2d, length, conv_buf, norm_g, w_in, conv_w, a_log, dt_bias)
        q, k, v, gb = seq(q), seq(k), seq(v), seq(gb)
    else:
        qkv, z, ba = _odd_proj(h2d, norm_g, w_in)
        q, k, v, gb, new_buf = _conv_gates(seq(qkv), conv_buf, conv_w, seq(ba), a_log, dt_bias)
    return (q, k, v, z, gb), new_buf


def _even_tail_odd_head(h, parts, start, pool_buf, w_pool, pool_scale, w_out_even,
                        conv_buf, norm_g, w_in, conv_w, a_log, dt_bias):
    b, length, d = h.shape
    m = b * length
    u, sb, gate = parts
    pw = u.shape[-1]
    n = DN_HEADS * DN_DIM
    n_qkv = 3 * n
    tm = _row_tile(m, 256)
    assert length % tm == 0 and tm >= POOL_PAD
    tiles_per_seq = length // tm
    w_main, w_ba = _odd_weights(w_in)
    row = lambda i: (i, 0)
    fixed = lambda i: (0, 0)
    per_seq = lambda i: (i // tiles_per_seq, 0, 0)
    y, new_pool, q, k, v, z, gb, new_conv = pl.pallas_call(
        functools.partial(_even_tail_odd_head_kernel, start=start, chunk=512, tiles_per_seq=tiles_per_seq),
        grid=(m // tm,),
        in_specs=[pl.BlockSpec((tm, d), row), pl.BlockSpec((tm, pw), row), pl.BlockSpec((tm, pw), row),
                  pl.BlockSpec((tm, 2 * pw), row), pl.BlockSpec((1, POOL_HIST, pw), per_seq),
                  pl.BlockSpec(w_pool.shape, lambda i: (0, 0, 0)), pl.BlockSpec((1, pw), fixed),
                  pl.BlockSpec((2 * pw, d), fixed), pl.BlockSpec((1, d), fixed), pl.BlockSpec(w_main.shape, fixed),
                  pl.BlockSpec(w_ba.shape, fixed), pl.BlockSpec((1, CONV_HIST, n_qkv), per_seq),
                  pl.BlockSpec((CONV_WIDTH, n_qkv), fixed), pl.BlockSpec((2, LANES), fixed)],
        out_specs=[pl.BlockSpec((tm, d), row), pl.BlockSpec((1, POOL_HIST, pw), per_seq)]
        + [pl.BlockSpec((tm, n), row)] * 4 + [pl.BlockSpec((tm, LANES), row),
                                              pl.BlockSpec((1, CONV_HIST, n_qkv), per_seq)],
        out_shape=[jax.ShapeDtypeStruct((m, d), F32), jax.ShapeDtypeStruct((b, POOL_HIST, pw), F32)]
        + [jax.ShapeDtypeStruct((m, n), F32)] * 4
        + [jax.ShapeDtypeStruct((m, LANES), F32), jax.ShapeDtypeStruct((b, CONV_HIST, n_qkv), F32)],
        scratch_shapes=[pltpu.VMEM((POOL_PAD + tm, pw), F32), pltpu.VMEM((tm, 2 * pw), BF16),
                        pltpu.VMEM((CONV_PAD + tm, n_qkv), F32)],
        compiler_params=_cparams("arbitrary"),
        name="even_tail_odd_head",
    )(h.reshape(m, d), u.reshape(m, pw), sb.reshape(m, pw), gate, pool_buf, w_pool.astype(BF16),
      pool_scale[None, :], w_out_even.astype(BF16), norm_g[None, :], w_main, w_ba, conv_buf, conv_w,
      _decay_params(a_log, dt_bias))
    seq = lambda a: a.reshape(b, length, a.shape[-1])
    return y.reshape(b, length, d), new_pool, (seq(q), seq(k), seq(v), z, seq(gb)), new_conv


def _odd_layer(h, conv_buf, state, norm_g, w_in, conv_w, a_log, dt_bias, o_g, w_out):
    parts, new_buf = _odd_head(h, conv_buf, norm_g, w_in, conv_w, a_log, dt_bias)
    y, new_state = _odd_tail(h, parts, state, o_g, w_out)
    return y, new_buf, new_state


def _odd_tail(h, parts, state, o_g, w_out):
    b, length, d = h.shape
    q, k, v, z, gb = parts
    z = z.reshape(b, length, z.shape[-1])
    c = DN_CHUNK if length % DN_CHUNK == 0 else length
    c_pad = -(-c // 8) * 8
    if c_pad != c:
        pad = lambda a: jnp.pad(a, ((0, 0), (0, c_pad - c), (0, 0)))
        y, new_state = _gated_delta_out(pad(q), pad(k), pad(v), pad(gb), state, c_pad, pad(h), pad(z), o_g, w_out)
        return y[:, :length], new_state
    return _gated_delta_out(q, k, v, gb, state, c, h, z, o_g, w_out)


def _even_odd_layers(h, start, pool_buf, past, conv_buf, state, ev, od):
    norm_e, w_in_e, w_pool, pool_scale, q_g, k_g, sb_bias, w_out_e = ev
    norm_o, w_in_o, conv_w, a_log, dt_bias, o_g, w_out_o = od
    b, length, _ = h.shape
    if not _whole_tile_sequences(b, length):
        y, new_pool, k, v = _even_layer(h, start, pool_buf, past, *ev)
        y, new_conv, new_state = _odd_layer(y, conv_buf, state, *od)
        return y, new_pool, k, v, new_conv, new_state
    parts, k, v = _even_head(h, past, norm_e, w_in_e, q_g, k_g, sb_bias)
    y, new_pool, odd_parts, new_conv = _even_tail_odd_head(
        h, parts, start, pool_buf, w_pool, pool_scale, w_out_e, conv_buf, norm_o, w_in_o, conv_w, a_log, dt_bias)
    y, new_state = _odd_tail(y, odd_parts, state, o_g, w_out_o)
    return y, new_pool, k, v, new_conv, new_state


def kernel(x_prompt, x_sample, cache_pool, cache_k, cache_v, state_conv, state_delta, page_table, norm_even, w_in_even, w_pool, pool_scale, q_norm, k_norm, sb_bias, w_out_even, norm_odd, w_in_odd, conv_w, a_log, dt_bias, o_norm, w_out_odd):
    bp = x_prompt.shape[0]
    past_len = page_table.shape[1] * cache_k.shape[2]
    n_even, n_odd = norm_even.shape[0], norm_odd.shape[0]
    assert n_odd <= n_even <= n_odd + 1
    hp, hs = x_prompt, x_sample
    outs = {name: [] for name in ("pool_p", "pool_s", "k_p", "v_p", "k_s", "v_s", "conv_p", "conv_s", "st_p", "st_s")}
    for i in range(n_even):
        ev = (norm_even[i], w_in_even[i], w_pool[i], pool_scale[i], q_norm[i], k_norm[i], sb_bias[i], w_out_even[i])
        zero_buf = jnp.zeros((bp, POOL_HIST, cache_pool.shape[-1]), F32)
        past = (cache_k[i], cache_v[i], page_table)
        if i < n_odd:
            od = (norm_odd[i], w_in_odd[i], conv_w[i], a_log[i], dt_bias[i], o_norm[i], w_out_odd[i])
            zero_conv = jnp.zeros((bp, CONV_HIST, state_conv.shape[-1]), F32)
            zero_state = jnp.zeros((bp,) + state_delta.shape[2:], F32)
            hp, pb, kn, vn, cb, st = _even_odd_layers(hp, 0, zero_buf, None, zero_conv, zero_state, ev, od)
            outs["conv_p"].append(cb), outs["st_p"].append(st)
            outs["pool_p"].append(pb), outs["k_p"].append(kn), outs["v_p"].append(vn)
            hs, pb, kn, vn, cb, st = _even_odd_layers(hs, past_len, cache_pool[i], past, state_conv[i],
                                                      state_delta[i], ev, od)
            outs["conv_s"].append(cb), outs["st_s"].append(st)
            outs["pool_s"].append(pb), outs["k_s"].append(kn), outs["v_s"].append(vn)
        else:
            hp, pb, kn, vn = _even_layer(hp, 0, zero_buf, None, *ev)
            outs["pool_p"].append(pb), outs["k_p"].append(kn), outs["v_p"].append(vn)
            hs, pb, kn, vn = _even_layer(hs, past_len, cache_pool[i], past, *ev)
            outs["pool_s"].append(pb), outs["k_s"].append(kn), outs["v_s"].append(vn)
    stack = lambda name: jnp.stack(outs[name])
    return (hp, hs, stack("pool_p"), stack("pool_s"), stack("k_p"), stack("v_p"), stack("k_s"), stack("v_s"),
            stack("conv_p"), stack("conv_s"), stack("st_p"), stack("st_s"))
```
